```python
import jax, jax.numpy as jnp
from jax import lax
import numpy as np

D_MODEL = 2048
BATCH = 4
SEQ = 2048
DEPTH = 2

GRID_W = 64
CTX_LEN = 256
HEAD_DIM = 128
RET_HEADS = D_MODEL // (2 * HEAD_DIM)
RET_DK = HEAD_DIM
RET_DV = HEAD_DIM
RET_CHUNK = 128
MLA_HEADS = D_MODEL // (2 * HEAD_DIM)
MLA_DN = HEAD_DIM
MLA_DR = 64
MLA_DV = HEAD_DIM
MLA_Q_RANK = D_MODEL // 4
MLA_KV_RANK = D_MODEL // 8
MLA_SCALE = (MLA_DN + MLA_DR) ** -0.5
Q_BLOCK = 128
RET_W = RET_HEADS * RET_DK
RET_VW = RET_HEADS * RET_DV
IN_SPLITS = [RET_W, 2 * RET_W, 2 * RET_W + RET_VW, 2 * RET_W + 2 * RET_VW,
             2 * RET_W + 2 * RET_VW + MLA_Q_RANK, 2 * RET_W + 2 * RET_VW + MLA_Q_RANK + MLA_KV_RANK]
IN_COLS = 2 * RET_W + 2 * RET_VW + MLA_Q_RANK + MLA_KV_RANK + MLA_DR
MIX_W = RET_VW + MLA_HEADS * MLA_DV
POOL_WINDOWS = (2, 4, 8, 16)
POOL_G = D_MODEL // len(POOL_WINDOWS)
FFN_DIM = ((8 * D_MODEL // 3 + 255) // 256) * 256
ROPE_BASE = 10000.0
EPS = 1e-6

kernel_name = "hybrid_retention_mla_pool_diffusion_block"


def rmsnorm(x, g):
    x32 = x.astype(jnp.float32)
    y = x32 * lax.rsqrt(jnp.mean(x32 * x32, axis=-1, keepdims=True) + EPS)
    return (y * g).astype(x.dtype)


def modulate(h, shift, scale):
    return h * (1.0 + scale) + shift


def ada_mod(cvec, w, b):
    m = jax.nn.silu(cvec) @ w + b
    return jnp.split(m, 6, axis=-1)


def axial_rope(rows, dim):
    row = jnp.broadcast_to(jnp.arange(rows, dtype=jnp.float32)[:, None], (rows, GRID_W)).reshape(-1)
    col = jnp.broadcast_to(jnp.arange(GRID_W, dtype=jnp.float32)[None, :], (rows, GRID_W)).reshape(-1)
    n_freq = dim // 4
    inv = ROPE_BASE ** (-jnp.arange(n_freq, dtype=jnp.float32) / n_freq)
    ang = jnp.concatenate([row[:, None] * inv, col[:, None] * inv], axis=-1)
    return jnp.cos(ang), jnp.sin(ang)


def apply_rope(x, cos, sin):
    if x.ndim == 4:
        cos, sin = cos[:, None, :], sin[:, None, :]
    half = x.shape[-1] // 2
    x1, x2 = x[..., :half], x[..., half:]
    return jnp.concatenate([x1 * cos - x2 * sin, x1 * sin + x2 * cos], axis=-1).astype(x.dtype)


def retention_scan(q, k, v, log_g, s0, strict):
    B, L, H, _ = q.shape
    n = L // RET_CHUNK

    def chunks(a):
        return a.reshape(B, n, RET_CHUNK, H, a.shape[-1]).transpose(1, 0, 3, 2, 4)

    idx = jnp.arange(RET_CHUNK, dtype=jnp.float32)
    diff = idx[:, None] - idx[None, :]
    mask = (diff > 0) if strict else (diff >= 0)
    decay_in = jnp.where(mask, jnp.exp(jnp.where(mask, diff, 0.0) * log_g[:, None, None]), 0.0)
    q_dec = jnp.exp((idx + 1.0) * log_g[:, None])[..., None]
    k_dec = jnp.exp((RET_CHUNK - 1.0 - idx) * log_g[:, None])[..., None]
    c_dec = jnp.exp(RET_CHUNK * log_g)[:, None, None]

    def step(s, qkv):
        qc, kc, vc = qkv
        inner = jnp.einsum('bhnd,bhmd->bhnm', qc, kc) * decay_in
        o = jnp.einsum('bhnm,bhme->bhne', inner, vc) + jnp.einsum('bhnd,bhde->bhne', qc * q_dec, s)
        s = s * c_dec + jnp.einsum('bhmd,bhme->bhde', kc * k_dec, vc)
        return s, o

    s, o = lax.scan(step, s0, (chunks(q), chunks(k), chunks(v)))
    return s, o.transpose(1, 0, 3, 2, 4).reshape(B, L, H, v.shape[-1])


def bidir_retention(rc, rl, log_f, log_b):
    qc, kc, vc = rc
    ql, kl, vl = rl
    fl = lambda a: jnp.flip(a, axis=1)
    s0 = jnp.zeros((qc.shape[0], RET_HEADS, RET_DK, RET_DV), jnp.float32)
    s_cf, o_cf = retention_scan(qc, kc, vc, log_f, s0, False)
    s_cb, o_cb = retention_scan(fl(qc), fl(kc), fl(vc), log_b, s0, True)
    _, o_lf = retention_scan(ql, kl, vl, log_f, s_cf, False)
    _, o_lb = retention_scan(fl(ql), fl(kl), fl(vl), log_b, s_cb, True)
    return o_cf + fl(o_cb), o_lf + fl(o_lb)


def mla_attention(qn, qr, kn, kr, v):
    B, Lq, H, _ = qn.shape
    nb = Lq // Q_BLOCK

    def to_blocks(a):
        return a.reshape(B, nb, Q_BLOCK, H, a.shape[-1]).swapaxes(0, 1)

    def one_block(qb):
        qn_b, qr_b = qb
        s = jnp.einsum('bqhd,bkhd->bhqk', qn_b, kn) + jnp.einsum('bqhd,bkd->bhqk', qr_b, kr)
        p = jax.nn.softmax(s.astype(jnp.float32) * MLA_SCALE, axis=-1).astype(v.dtype)
        return jnp.einsum('bhqk,bkhd->bqhd', p, v)

    o = lax.map(one_block, (to_blocks(qn), to_blocks(qr)))
    return o.swapaxes(0, 1).reshape(B, Lq, H, v.shape[-1])


def retention_mla_mixer(h_ctx, h_lat, rope_ret, rope_mla, w_in, q_norm_g, w_uq, kv_norm_g, w_ukv,
                        decay_f, decay_b, w_out, with_ctx):
    def project(h, rope_r, rope_m):
        B, L, _ = h.shape
        z = h @ w_in
        rq, rk, rv, rg, cq, ckv, kr = jnp.split(z, IN_SPLITS, axis=-1)
        rq = rq.reshape(B, L, RET_HEADS, RET_DK)
        rk = rk.reshape(B, L, RET_HEADS, RET_DK)
        rv = rv.reshape(B, L, RET_HEADS, RET_DV)
        q = (rmsnorm(cq, q_norm_g) @ w_uq).reshape(B, L, MLA_HEADS, MLA_DN + MLA_DR)
        kv = (rmsnorm(ckv, kv_norm_g) @ w_ukv).reshape(B, L, MLA_HEADS, MLA_DN + MLA_DV)
        qn, qr = q[..., :MLA_DN], q[..., MLA_DN:]
        kn, v = kv[..., :MLA_DN], kv[..., MLA_DN:]
        if rope_r is not None:
            rq, rk = apply_rope(rq, *rope_r), apply_rope(rk, *rope_r)
            qr, kr = apply_rope(qr, *rope_m), apply_rope(kr, *rope_m)
        ret = (rq.astype(jnp.float32), (rk * RET_DK ** -0.5).astype(jnp.float32), rv.astype(jnp.float32))
        return ret, rg, (qn, qr, kn, kr, v)

    ret_c, g_c, mla_c = project(h_ctx, None, None)
    ret_l, g_l, mla_l = project(h_lat, rope_ret, rope_mla)
    log_f = jax.nn.log_sigmoid(decay_f.astype(jnp.float32))
    log_b = jax.nn.log_sigmoid(decay_b.astype(jnp.float32))
    o_ret_c, o_ret_l = bidir_retention(ret_c, ret_l, log_f, log_b)

    def finish(o_ret, gate, o_mla, dtype):
        B, L = o_ret.shape[:2]
        o_ret = o_ret * lax.rsqrt(jnp.mean(o_ret * o_ret, axis=-1, keepdims=True) + EPS)
        o_ret = o_ret.astype(dtype).reshape(B, L, RET_VW) * jax.nn.silu(gate)
        return jnp.concatenate([o_ret, o_mla.reshape(B, L, MLA_HEADS * MLA_DV)], axis=-1) @ w_out

    qn_c, qr_c, kn_c, kr_c, v_c = mla_c
    qn_l, qr_l, kn_l, kr_l, v_l = mla_l
    o_mla_l = mla_attention(qn_l, qr_l, jnp.concatenate([kn_c, kn_l], axis=1),
                            jnp.concatenate([kr_c, kr_l], axis=1), jnp.concatenate([v_c, v_l], axis=1))
    y_lat = finish(o_ret_l, g_l, o_mla_l, h_lat.dtype)
    y_ctx = None
    if with_ctx:
        o_mla_c = mla_attention(qn_c, qr_c, kn_c, kr_c, v_c)
        y_ctx = finish(o_ret_c, g_c, o_mla_c, h_ctx.dtype)
    return y_ctx, y_lat


def multiscale_pool(h, w_pool, scale):
    B, L, D = h.shape
    hf = h.astype(jnp.float32)
    cs = jnp.concatenate([jnp.zeros((B, 1, D), jnp.float32), jnp.cumsum(hf, axis=1)], axis=1)
    t = jnp.arange(L)
    groups = []
    for gi, w in enumerate(POOL_WINDOWS):
        sl = slice(gi * POOL_G, (gi + 1) * POOL_G)
        lo = jnp.clip(t - w // 2, 0, L)
        hi = jnp.clip(t - w // 2 + w, 0, L)
        cnt = (hi - lo).astype(jnp.float32)[:, None]
        groups.append((cs[:, hi, sl] - cs[:, lo, sl]) / cnt - hf[:, :, sl])
    p = jnp.stack(groups, axis=2).astype(h.dtype)
    y = jnp.einsum('blgc,gcd->blgd', p, w_pool).reshape(B, L, D)
    return y * scale


def conv_ffn(h, w_up, conv_w, conv_b, w_down):
    u = h @ w_up
    up = jnp.pad(u, ((0, 0), (1, 1), (0, 0)))
    u = up[:, :-2] * conv_w[0] + up[:, 1:-1] * conv_w[1] + up[:, 2:] * conv_w[2] + conv_b
    a, g = jnp.split(u, 2, axis=-1)
    return (jax.nn.silu(g) * a) @ w_down


def setup_inputs(seed: int = 0) -> dict:
    key = jax.random.key(seed)
    ks = jax.random.split(key, 24)
    n_even = (DEPTH + 1) // 2
    n_odd = DEPTH // 2
    nrm = lambda k, s, sc: jax.random.normal(k, s, jnp.float32) * sc
    base_logit = jnp.asarray(np.log(2.0 ** (5 + np.arange(RET_HEADS)) - 1.0).astype(np.float32))
    return {
        "x": nrm(ks[0], (BATCH, SEQ, D_MODEL), 1.0),
        "c": nrm(ks[1], (BATCH, D_MODEL), 1.0),
        "ctx": nrm(ks[2], (BATCH, CTX_LEN, D_MODEL), 1.0),
        "c_ctx": nrm(ks[3], (D_MODEL,), 1.0),
        "ada_w": nrm(ks[4], (DEPTH, D_MODEL, 6 * D_MODEL), 0.02),
        "ada_b": nrm(ks[5], (DEPTH, 6 * D_MODEL), 0.01),
        "norm1_g": 1.0 + nrm(ks[6], (DEPTH, D_MODEL), 0.02),
        "norm2_g": 1.0 + nrm(ks[7], (DEPTH, D_MODEL), 0.02),
        "ffn_w_up": nrm(ks[8], (DEPTH, D_MODEL, 2 * FFN_DIM), D_MODEL ** -0.5),
        "ffn_conv_w": nrm(ks[9], (DEPTH, 3, 2 * FFN_DIM), 3 ** -0.5),
        "ffn_conv_b": nrm(ks[10], (DEPTH, 2 * FFN_DIM), 0.01),
        "ffn_w_down": nrm(ks[11], (DEPTH, FFN_DIM, D_MODEL), FFN_DIM ** -0.5),
        "mix_w_in": nrm(ks[12], (n_even, D_MODEL, IN_COLS), D_MODEL ** -0.5),
        "mla_q_norm_g": 1.0 + nrm(ks[13], (n_even, MLA_Q_RANK), 0.02),
        "mla_w_uq": nrm(ks[14], (n_even, MLA_Q_RANK, MLA_HEADS * (MLA_DN + MLA_DR)), MLA_Q_RANK ** -0.5),
        "mla_kv_norm_g": 1.0 + nrm(ks[15], (n_even, MLA_KV_RANK), 0.02),
        "mla_w_ukv": nrm(ks[16], (n_even, MLA_KV_RANK, MLA_HEADS * (MLA_DN + MLA_DV)), MLA_KV_RANK ** -0.5),
        "ret_decay_f": base_logit + nrm(ks[17], (n_even, RET_HEADS), 0.1),
        "ret_decay_b": base_logit + nrm(ks[18], (n_even, RET_HEADS), 0.1),
        "mix_w_out": nrm(ks[19], (n_even, MIX_W, D_MODEL), MIX_W ** -0.5),
        "pool_w": nrm(ks[20], (n_odd, len(POOL_WINDOWS), POOL_G, POOL_G), POOL_G ** -0.5),
        "pool_scale": 1.0 + nrm(ks[21], (n_odd, D_MODEL), 0.1),
        "final_g": 1.0 + nrm(ks[22], (D_MODEL,), 0.02),
    }


def reference(x, c, ctx, c_ctx, ada_w, ada_b, norm1_g, norm2_g, ffn_w_up, ffn_conv_w, ffn_conv_b, ffn_w_down,
              mix_w_in, mla_q_norm_g, mla_w_uq, mla_kv_norm_g, mla_w_ukv, ret_decay_f, ret_decay_b, mix_w_out,
              pool_w, pool_scale, final_g):
    rows = x.shape[1] // GRID_W
    rope_ret = axial_rope(rows, RET_DK)
    rope_mla = axial_rope(rows, MLA_DR)
    x_lat, x_ctx = x, ctx
    for l in range(DEPTH):
        j = l // 2
        with_ctx = l < DEPTH - 1
        sh1, sc1, g1, sh2, sc2, g2 = [m[:, None, :] for m in ada_mod(c, ada_w[l], ada_b[l])]
        h_lat = modulate(rmsnorm(x_lat, norm1_g[l]), sh1, sc1)
        if l % 2 == 0 or with_ctx:
            csh1, csc1, cg1, csh2, csc2, cg2 = ada_mod(c_ctx, ada_w[l], ada_b[l])
            h_ctx = modulate(rmsnorm(x_ctx, norm1_g[l]), csh1, csc1)
        if l % 2 == 0:
            o_ctx, o_lat = retention_mla_mixer(h_ctx, h_lat, rope_ret, rope_mla, mix_w_in[j], mla_q_norm_g[j],
                                               mla_w_uq[j], mla_kv_norm_g[j], mla_w_ukv[j], ret_decay_f[j],
                                               ret_decay_b[j], mix_w_out[j], with_ctx)
        else:
            o_lat = multiscale_pool(h_lat, pool_w[j], pool_scale[j])
            o_ctx = multiscale_pool(h_ctx, pool_w[j], pool_scale[j]) if with_ctx else None
        x_lat = x_lat + g1 * o_lat
        x_lat = x_lat + g2 * conv_ffn(modulate(rmsnorm(x_lat, norm2_g[l]), sh2, sc2),
                                      ffn_w_up[l], ffn_conv_w[l], ffn_conv_b[l], ffn_w_down[l])
        if with_ctx:
            x_ctx = x_ctx + cg1 * o_ctx
            x_ctx = x_ctx + cg2 * conv_ffn(modulate(rmsnorm(x_ctx, norm2_g[l]), csh2, csc2),
                                          ffn_w_up[l], ffn_conv_w[l], ffn_conv_b[l], ffn_w_down[l])
    return rmsnorm(x_lat, final_g)
```

```python
import functools

import jax
import jax.numpy as jnp
from jax import lax
from jax.experimental import pallas as pl
from jax.experimental.pallas import tpu as pltpu

F32 = jnp.float32
BF16 = jnp.bfloat16

GRID_W = 64
HEAD_DIM = 128
RET_CHUNK = 128
MLA_DR = 64
POOL_WINDOWS = (2, 4, 8, 16)
ROPE_BASE = 10000.0
EPS = 1e-6

LANES = 128
HALO = 16
V7X_VMEM_BYTES = 64 * 1024 * 1024
VMEM_CAP = V7X_VMEM_BYTES - 8 * 1024 * 1024


def _cparams(sem, vmem_bytes):
    return pltpu.CompilerParams(dimension_semantics=sem, vmem_limit_bytes=min(int(vmem_bytes), VMEM_CAP))


def _silu(x):
    return x / (1.0 + jnp.exp(-x))


def _norm_mod(x, g, shift, scale):
    y = x * lax.rsqrt(jnp.mean(x * x, axis=-1, keepdims=True) + EPS) * g
    return y * (1.0 + scale) + shift


def _dot(a, b):
    return jnp.dot(a, b, preferred_element_type=F32)


def _dot_nt(a, b):
    return lax.dot_general(a, b, (((1,), (1,)), ((), ())), preferred_element_type=F32)


def _dot_tn(a, b):
    return lax.dot_general(a, b, (((0,), (0,)), ((), ())), preferred_element_type=F32)


def _ada_kernel(c_ref, w_ref, b_ref, o_ref):
    s = _silu(c_ref[...]).astype(BF16)
    o_ref[0] = _dot(s, w_ref[0].astype(BF16)) + b_ref[0]


def _ada(cc, ada_w, ada_b):
    depth, d, n6 = ada_w.shape
    rows = cc.shape[0]
    tn = 1024
    return pl.pallas_call(
        _ada_kernel,
        grid=(depth, n6 // tn),
        in_specs=[
            pl.BlockSpec((rows, d), lambda l, n: (0, 0)),
            pl.BlockSpec((1, d, tn), lambda l, n: (l, 0, n)),
            pl.BlockSpec((1, 1, tn), lambda l, n: (l, 0, n)),
        ],
        out_specs=pl.BlockSpec((1, rows, tn), lambda l, n: (l, 0, n)),
        out_shape=jax.ShapeDtypeStruct((depth, rows, n6), F32),
        compiler_params=_cparams(("parallel", "parallel"), 2 * d * tn * 4 + d * tn * 2 + (8 << 20)),
        name="ada_mod",
    )(cc, ada_w, ada_b.reshape(depth, 1, n6))


def _inproj_ret_kernel(*refs, rope, heads_per_tile, k_scale):
    if rope:
        x_ref, mod_ref, g_ref, cos_ref, sin_ref, w_ref, o_ref, h_scr = refs
    else:
        x_ref, mod_ref, g_ref, w_ref, o_ref, h_scr = refs
    n = pl.program_id(2)

    @pl.when(n == 0)
    def _():
        h_scr[...] = _norm_mod(x_ref[0], g_ref[...], mod_ref[0, 0:1, :], mod_ref[0, 1:2, :]).astype(BF16)

    z = _dot(h_scr[...], w_ref[...].astype(BF16))
    scale = jnp.where(n == 1, k_scale, 1.0).astype(F32)
    if rope:
        @pl.when(n < 2)
        def _():
            cos = cos_ref[...]
            sin = sin_ref[...]
            for h in range(heads_per_tile):
                sl = slice(h * HEAD_DIM, (h + 1) * HEAD_DIM)
                zh = z[:, sl]
                r = zh * cos + pltpu.roll(zh, HEAD_DIM // 2, 1) * sin
                o_ref[0, :, sl] = (r * scale).astype(BF16)

        @pl.when(n >= 2)
        def _():
            o_ref[0] = z.astype(BF16)
    else:
        o_ref[0] = (z * scale).astype(BF16)


def _inproj_ret(x, mod, mod_row, g, w_in, rope_tabs, ret_w, tm):
    b, l, d = x.shape
    tn = ret_w
    rope = rope_tabs is not None
    mod_idx = (lambda bi, m, n: (bi, 0, 0)) if mod_row is None else (lambda bi, m, n: (mod_row, 0, 0))
    in_specs = [
        pl.BlockSpec((1, tm, d), lambda bi, m, n: (bi, m, 0)),
        pl.BlockSpec((1, 6, d), mod_idx),
        pl.BlockSpec((1, d), lambda bi, m, n: (0, 0)),
    ]
    args = [x, mod, g]
    if rope:
        in_specs += [pl.BlockSpec((tm, HEAD_DIM), lambda bi, m, n: (m, 0))] * 2
        args += list(rope_tabs)
    in_specs.append(pl.BlockSpec((d, tn), lambda bi, m, n: (0, n)))
    args.append(w_in)
    vmem = 2 * tm * d * 4 + 2 * d * tn * 4 + d * tn * 2 + tm * d * 2 + 2 * tm * tn * 2 + 3 * tm * tn * 4 + (6 << 20)
    return pl.pallas_call(
        functools.partial(_inproj_ret_kernel, rope=rope, heads_per_tile=tn // HEAD_DIM, k_scale=HEAD_DIM ** -0.5),
        grid=(b, l // tm, 4),
        in_specs=in_specs,
        out_specs=pl.BlockSpec((1, tm, tn), lambda bi, m, n: (bi, m, n)),
        out_shape=jax.ShapeDtypeStruct((b, l, 4 * tn), BF16),
        scratch_shapes=[pltpu.VMEM((tm, d), BF16)],
        compiler_params=_cparams(("parallel", "parallel", "arbitrary"), vmem),
        name="inproj_ret_rope" if rope else "inproj_ret",
    )(*args)


def _rope_half_padded(x, cos, sin):
    rot = pltpu.roll(x, MLA_DR // 2, 1) + pltpu.roll(x, LANES - MLA_DR // 2, 1)
    return x * cos + rot * sin


def _inproj_mla_kernel(*refs, rope, heads, q_rank, kv_rank, q_scale):
    if rope:
        (x_ref, mod_ref, g_ref, cos_ref, sin_ref, wt_ref, gq_ref, wq_ref, gkv_ref, wkn_ref, wv_ref,
         q_out, k_out, v_out) = refs
    else:
        x_ref, mod_ref, g_ref, wt_ref, gq_ref, wq_ref, gkv_ref, wkn_ref, wv_ref, q_out, k_out, v_out = refs
    h = _norm_mod(x_ref[0], g_ref[...], mod_ref[0, 0:1, :], mod_ref[0, 1:2, :]).astype(BF16)
    zt = _dot(h, wt_ref[...])
    cq = zt[:, :q_rank]
    ckv = zt[:, q_rank:q_rank + kv_rank]
    kr = zt[:, q_rank + kv_rank:]
    cqn = (cq * lax.rsqrt(jnp.mean(cq * cq, axis=-1, keepdims=True) + EPS) * gq_ref[...]).astype(BF16)
    ckvn = (ckv * lax.rsqrt(jnp.mean(ckv * ckv, axis=-1, keepdims=True) + EPS) * gkv_ref[...]).astype(BF16)
    q = _dot(cqn, wq_ref[...])
    kn = _dot(ckvn, wkn_ref[...])
    v_out[0] = _dot(ckvn, wv_ref[...]).astype(BF16)
    if rope:
        cos = cos_ref[...]
        sin = sin_ref[...]
        kr = _rope_half_padded(kr, cos, sin)
    kr = kr.astype(BF16)
    for hd in range(heads):
        base = hd * 2 * HEAD_DIM
        qn = q[:, base:base + HEAD_DIM]
        qr = q[:, base + HEAD_DIM:base + 2 * HEAD_DIM]
        if rope:
            qr = _rope_half_padded(qr, cos, sin)
        q_out[0, :, base:base + HEAD_DIM] = (qn * q_scale).astype(BF16)
        q_out[0, :, base + HEAD_DIM:base + 2 * HEAD_DIM] = (qr * q_scale).astype(BF16)
        k_out[0, :, base:base + HEAD_DIM] = kn[:, hd * HEAD_DIM:(hd + 1) * HEAD_DIM].astype(BF16)
        k_out[0, :, base + HEAD_DIM:base + 2 * HEAD_DIM] = kr


def _inproj_mla(x, mod, mod_row, g, w_tail, gq, wq, gkv, wkn, wv, rope_tabs, heads, tm):
    b, l, d = x.shape
    q_rank, kv_rank = gq.shape[1], gkv.shape[1]
    rope = rope_tabs is not None
    mod_idx = (lambda bi, m: (bi, 0, 0)) if mod_row is None else (lambda bi, m: (mod_row, 0, 0))
    full = lambda a: pl.BlockSpec(a.shape, lambda bi, m: (0,) * a.ndim)
    in_specs = [pl.BlockSpec((1, tm, d), lambda bi, m: (bi, m, 0)), pl.BlockSpec((1, 6, d), mod_idx), full(g)]
    args = [x, mod, g]
    if rope:
        in_specs += [pl.BlockSpec((tm, LANES), lambda bi, m: (m, 0))] * 2
        args += list(rope_tabs)
    for a in (w_tail, gq, wq, gkv, wkn, wv):
        in_specs.append(full(a))
        args.append(a)
    qw = heads * 2 * HEAD_DIM
    vw = heads * HEAD_DIM
    out_spec = lambda w: pl.BlockSpec((1, tm, w), lambda bi, m: (bi, m, 0))
    wbytes = sum(a.size * a.dtype.itemsize for a in (w_tail, wq, wkn, wv))
    vmem = 2 * tm * d * 4 + 2 * wbytes + 2 * tm * (2 * qw + vw) * 2 + tm * (d * 6 + 4 * qw * 4) + (6 << 20)
    return pl.pallas_call(
        functools.partial(_inproj_mla_kernel, rope=rope, heads=heads, q_rank=q_rank, kv_rank=kv_rank,
                          q_scale=(HEAD_DIM + MLA_DR) ** -0.5),
        grid=(b, l // tm),
        in_specs=in_specs,
        out_specs=[out_spec(qw), out_spec(qw), out_spec(vw)],
        out_shape=[jax.ShapeDtypeStruct((b, l, qw), BF16), jax.ShapeDtypeStruct((b, l, qw), BF16),
                   jax.ShapeDtypeStruct((b, l, vw), BF16)],
        compiler_params=_cparams(("parallel", "parallel"), vmem),
        name="inproj_mla_rope" if rope else "inproj_mla",
    )(*args)


def _log_sigmoid(x):
    return jnp.minimum(x, 0.0) - jnp.log1p(jnp.exp(-jnp.abs(x)))


def _retention_kernel(dec_ref, ql_ref, kl_ref, vl_ref, gl_ref, qc_ref, kc_ref, vc_ref, gc_ref, ol_ref, oc_ref,
                      kv_scr, *, n_ctx, n_lat):
    c = RET_CHUNK
    hd = pl.program_id(1)
    lf = _log_sigmoid(jnp.full((c, c), dec_ref[0, hd], F32))
    lb = _log_sigmoid(jnp.full((c, c), dec_ref[1, hd], F32))
    ii = lax.broadcasted_iota(jnp.int32, (c, c), 0).astype(F32)
    jj = lax.broadcasted_iota(jnp.int32, (c, c), 1).astype(F32)
    diff = ii - jj
    fwd = diff >= 0
    decay = jnp.where(fwd, jnp.exp(jnp.where(fwd, diff, 0.0) * lf), jnp.exp(jnp.where(fwd, 0.0, -diff) * lb))
    q_dec_f = jnp.exp((ii + 1.0) * lf)
    k_dec_f = jnp.exp((c - 1.0 - ii) * lf)
    c_dec_f = jnp.exp(c * lf)
    q_dec_b = jnp.exp((c - ii) * lb)
    k_dec_b = jnp.exp(ii * lb)
    c_dec_b = jnp.exp(c * lb)

    def chunk(ref, i):
        return ref[0, pl.ds(pl.multiple_of(i * c, c), c), :]

    def kv_store(slot, k, v):
        kf = k.astype(F32)
        kv_scr[0, slot] = _dot_tn((kf * k_dec_f).astype(BF16), v)
        kv_scr[1, slot] = _dot_tn((kf * k_dec_b).astype(BF16), v)

    for i in range(n_ctx):
        kv_store(i, chunk(kc_ref, i), chunk(vc_ref, i))

    def kv_body(i, carry):
        kv_store(n_ctx + i, chunk(kl_ref, i), chunk(vl_ref, i))
        return carry

    lax.fori_loop(0, n_lat, kv_body, 0)

    s = jnp.zeros((c, c), F32)
    for slot in range(n_ctx + n_lat):
        kv = kv_scr[0, slot]
        kv_scr[0, slot] = s
        s = s * c_dec_f + kv
    s = jnp.zeros((c, c), F32)
    for slot in list(range(n_ctx - 1, -1, -1)) + list(range(n_ctx + n_lat - 1, n_ctx - 1, -1)):
        kv = kv_scr[1, slot]
        kv_scr[1, slot] = s
        s = s * c_dec_b + kv

    def out_chunk(slot, q, k, v, gate):
        p = (_dot_nt(q, k) * decay).astype(BF16)
        qf = q.astype(F32)
        o = _dot(p, v)
        o = o + _dot((qf * q_dec_f).astype(BF16), kv_scr[0, slot].astype(BF16))
        o = o + _dot((qf * q_dec_b).astype(BF16), kv_scr[1, slot].astype(BF16))
        o = o * lax.rsqrt(jnp.mean(o * o, axis=-1, keepdims=True) + EPS)
        return (o * _silu(gate.astype(F32))).astype(BF16)

    for i in range(n_ctx):
        oc_ref[0, i * c:(i + 1) * c, :] = out_chunk(i, chunk(qc_ref, i), chunk(kc_ref, i), chunk(vc_ref, i),
                                                    chunk(gc_ref, i))

    def out_body(i, carry):
        ol_ref[0, pl.ds(pl.multiple_of(i * c, c), c), :] = out_chunk(
            n_ctx + i, chunk(ql_ref, i), chunk(kl_ref, i), chunk(vl_ref, i), chunk(gl_ref, i))
        return carry

    lax.fori_loop(0, n_lat, out_body, 0)


def _retention(dec, ret_lat, ret_ctx, heads):
    b, l, _ = ret_lat.shape
    lc = ret_ctx.shape[1]
    n_lat, n_ctx = l // RET_CHUNK, lc // RET_CHUNK
    col = lambda rows, group: pl.BlockSpec((1, rows, HEAD_DIM), lambda bi, h: (bi, 0, group * heads + h))
    in_specs = [pl.BlockSpec(memory_space=pltpu.SMEM)]
    in_specs += [col(l, grp) for grp in range(4)] + [col(lc, grp) for grp in range(4)]
    out_col = lambda rows: pl.BlockSpec((1, rows, HEAD_DIM), lambda bi, h: (bi, 0, h))
    vmem = 2 * 5 * (l + lc) * HEAD_DIM * 2 + 2 * (n_lat + n_ctx) * RET_CHUNK * HEAD_DIM * 4 + (8 << 20)
    return pl.pallas_call(
        functools.partial(_retention_kernel, n_ctx=n_ctx, n_lat=n_lat),
        grid=(b, heads),
        in_specs=in_specs,
        out_specs=[out_col(l), out_col(lc)],
        out_shape=[jax.ShapeDtypeStruct((b, l, heads * HEAD_DIM), BF16),
                   jax.ShapeDtypeStruct((b, lc, heads * HEAD_DIM), BF16)],
        scratch_shapes=[pltpu.VMEM((2, n_lat + n_ctx, RET_CHUNK, HEAD_DIM), F32)],
        compiler_params=_cparams(("parallel", "parallel"), vmem),
        name="retention",
    )(dec, ret_lat, ret_lat, ret_lat, ret_lat, ret_ctx, ret_ctx, ret_ctx, ret_ctx)


def _attn_kernel(*refs, with_lat):
    if with_lat:
        q_ref, kc_ref, vc_ref, kl_ref, vl_ref, o_ref = refs
    else:
        q_ref, kc_ref, vc_ref, o_ref = refs
    q = q_ref[0]
    sc = _dot_nt(q, kc_ref[0])
    m = jnp.max(sc, axis=-1, keepdims=True)
    if with_lat:
        sl = _dot_nt(q, kl_ref[0])
        m = jnp.maximum(m, jnp.max(sl, axis=-1, keepdims=True))
    pc = jnp.exp(sc - m)
    den = jnp.sum(pc, axis=-1, keepdims=True)
    o = _dot(pc.astype(BF16), vc_ref[0])
    if with_lat:
        p_lat = jnp.exp(sl - m)
        den = den + jnp.sum(p_lat, axis=-1, keepdims=True)
        o = o + _dot(p_lat.astype(BF16), vl_ref[0])
    o_ref[0] = (o / den).astype(BF16)


def _attention(q, k_ctx, v_ctx, k_lat, v_lat, heads, tq):
    b, lq, _ = q.shape
    lc = k_ctx.shape[1]
    with_lat = k_lat is not None
    kw = 2 * HEAD_DIM
    in_specs = [
        pl.BlockSpec((1, tq, kw), lambda bi, h, i: (bi, i, h)),
        pl.BlockSpec((1, lc, kw), lambda bi, h, i: (bi, 0, h)),
        pl.BlockSpec((1, lc, HEAD_DIM), lambda bi, h, i: (bi, 0, h)),
    ]
    args = [q, k_ctx, v_ctx]
    lk = lc
    if with_lat:
        ll = k_lat.shape[1]
        lk += ll
        in_specs += [pl.BlockSpec((1, ll, kw), lambda bi, h, i: (bi, 0, h)),
                     pl.BlockSpec((1, ll, HEAD_DIM), lambda bi, h, i: (bi, 0, h))]
        args += [k_lat, v_lat]
    vmem = 2 * (tq * kw + lk * (kw + HEAD_DIM) + tq * HEAD_DIM) * 2 + 3 * tq * lk * 4 + (8 << 20)
    return pl.pallas_call(
        functools.partial(_attn_kernel, with_lat=with_lat),
        grid=(b, heads, lq // tq),
        in_specs=in_specs,
        out_specs=pl.BlockSpec((1, tq, HEAD_DIM), lambda bi, h, i: (bi, i, h)),
        out_shape=jax.ShapeDtypeStruct((b, lq, heads * HEAD_DIM), BF16),
        compiler_params=_cparams(("parallel", "parallel", "arbitrary"), vmem),
        name="mla_attn" if with_lat else "mla_attn_ctx",
    )(*args)


def _mix_out_kernel(a1_ref, a2_ref, w_ref, x_ref, mod_ref, o_ref, *, gate_row):
    k1 = a1_ref.shape[2]
    w = w_ref[...].astype(BF16)
    y = _dot(a1_ref[0], w[:k1]) + _dot(a2_ref[0], w[k1:])
    o_ref[0] = x_ref[0] + mod_ref[0, gate_row:gate_row + 1, :] * y


def _mix_out(a1, a2, w, x, mod, mod_row, tm, tn):
    b, l, d = x.shape
    k1, k2 = a1.shape[2], a2.shape[2]
    mod_idx = (lambda bi, m, n: (bi, 0, n)) if mod_row is None else (lambda bi, m, n: (mod_row, 0, n))
    vmem = 2 * tm * (k1 + k2) * 2 + 2 * (k1 + k2) * tn * 4 + (k1 + k2) * tn * 2 + 5 * tm * tn * 4 + (6 << 20)
    return pl.pallas_call(
        functools.partial(_mix_out_kernel, gate_row=2),
        grid=(b, l // tm, d // tn),
        in_specs=[
            pl.BlockSpec((1, tm, k1), lambda bi, m, n: (bi, m, 0)),
            pl.BlockSpec((1, tm, k2), lambda bi, m, n: (bi, m, 0)),
            pl.BlockSpec((k1 + k2, tn), lambda bi, m, n: (0, n)),
            pl.BlockSpec((1, tm, tn), lambda bi, m, n: (bi, m, n)),
            pl.BlockSpec((1, 6, tn), mod_idx),
        ],
        out_specs=pl.BlockSpec((1, tm, tn), lambda bi, m, n: (bi, m, n)),
        out_shape=jax.ShapeDtypeStruct((b, l, d), F32),
        compiler_params=_cparams(("parallel", "parallel", "arbitrary"), vmem),
        name="mix_out",
    )(a1, a2, w, x, mod)


def _ffn_up_kernel(xm_ref, xp_ref, xn_ref, mod_ref, g_ref, wa_ref, wg_ref, cwa_ref, cwg_ref, cba_ref, cbg_ref,
                   o_ref, h_scr, *, tm, n_m):
    m = pl.program_id(1)
    n = pl.program_id(2)

    @pl.when(n == 0)
    def _():
        g = g_ref[...]
        shift = mod_ref[0, 3:4, :]
        scale = mod_ref[0, 4:5, :]
        h_scr[HALO:HALO + tm, :] = _norm_mod(xm_ref[0], g, shift, scale).astype(BF16)
        hp = _norm_mod(xp_ref[0], g, shift, scale) * (m > 0).astype(F32)
        hn = _norm_mod(xn_ref[0], g, shift, scale) * (m < n_m - 1).astype(F32)
        h_scr[0:HALO, :] = hp.astype(BF16)
        h_scr[HALO + tm:, :] = hn.astype(BF16)

    h = h_scr[...]
    rows = tm + 2 * HALO

    def conv(w_ref, cw_ref, cb_ref):
        u = _dot(h, w_ref[...].astype(BF16))
        prev = pltpu.roll(u, 1, 0)[HALO:HALO + tm]
        nxt = pltpu.roll(u, rows - 1, 0)[HALO:HALO + tm]
        cw = cw_ref[...]
        return cw[0:1, :] * prev + cw[1:2, :] * u[HALO:HALO + tm] + cw[2:3, :] * nxt + cb_ref[...]

    a = conv(wa_ref, cwa_ref, cba_ref)
    gt = conv(wg_ref, cwg_ref, cbg_ref)
    o_ref[0] = (_silu(gt) * a).astype(BF16)


def _ffn_up(x, mod, mod_row, g, w_up, conv_w, conv_b, layer, tm, tn):
    b, l, d = x.shape
    f = w_up.shape[2] // 2
    n_m, n_n = l // tm, f // tn
    hb = tm // HALO
    mod_idx = (lambda bi, m, n: (bi, 0, 0)) if mod_row is None else (lambda bi, m, n: (mod_row, 0, 0))
    vmem = (2 * tm * d * 4 + 4 * d * tn * 4 + 2 * d * tn * 2 + (tm + 2 * HALO) * d * 2
            + 2 * tm * tn * 2 + 8 * (tm + 2 * HALO) * tn * 4 + (4 << 20))
    return pl.pallas_call(
        functools.partial(_ffn_up_kernel, tm=tm, n_m=n_m),
        grid=(b, n_m, n_n),
        in_specs=[
            pl.BlockSpec((1, tm, d), lambda bi, m, n: (bi, m, 0)),
            pl.BlockSpec((1, HALO, d), lambda bi, m, n: (bi, jnp.maximum(m * hb - 1, 0), 0)),
            pl.BlockSpec((1, HALO, d), lambda bi, m, n: (bi, jnp.minimum((m + 1) * hb, l // HALO - 1), 0)),
            pl.BlockSpec((1, 6, d), mod_idx),
            pl.BlockSpec((1, d), lambda bi, m, n: (0, 0)),
            pl.BlockSpec((None, d, tn), lambda bi, m, n: (layer, 0, n)),
            pl.BlockSpec((None, d, tn), lambda bi, m, n: (layer, 0, n + n_n)),
            pl.BlockSpec((None, 3, tn), lambda bi, m, n: (layer, 0, n)),
            pl.BlockSpec((None, 3, tn), lambda bi, m, n: (layer, 0, n + n_n)),
            pl.BlockSpec((None, 1, tn), lambda bi, m, n: (layer, 0, n)),
            pl.BlockSpec((None, 1, tn), lambda bi, m, n: (layer, 0, n + n_n)),
        ],
        out_specs=pl.BlockSpec((1, tm, tn), lambda bi, m, n: (bi, m, n)),
        out_shape=jax.ShapeDtypeStruct((b, l, f), BF16),
        scratch_shapes=[pltpu.VMEM((tm + 2 * HALO, d), BF16)],
        compiler_params=_cparams(("parallel", "parallel", "arbitrary"), vmem),
        name="ffn_up",
    )(x, x, x, mod, g, w_up, w_up, conv_w, conv_w, conv_b, conv_b)


def _ffn_down_kernel(a_ref, w_ref, x_ref, mod_ref, o_ref, *, gate_row):
    y = _dot(a_ref[0], w_ref[...].astype(BF16))
    o_ref[0] = x_ref[0] + mod_ref[0, gate_row:gate_row + 1, :] * y


def _ffn_down(act, w, x, mod, mod_row, layer, tm, tn):
    b, l, d = x.shape
    f = act.shape[2]
    mod_idx = (lambda bi, m, n: (bi, 0, n)) if mod_row is None else (lambda bi, m, n: (mod_row, 0, n))
    vmem = 2 * tm * f * 2 + 2 * f * tn * 4 + f * tn * 2 + 5 * tm * tn * 4 + (6 << 20)
    return pl.pallas_call(
        functools.partial(_ffn_down_kernel, gate_row=5),
        grid=(b, l // tm, d // tn),
        in_specs=[
            pl.BlockSpec((1, tm, f), lambda bi, m, n: (bi, m, 0)),
            pl.BlockSpec((None, f, tn), lambda bi, m, n: (layer, 0, n)),
            pl.BlockSpec((1, tm, tn), lambda bi, m, n: (bi, m, n)),
            pl.BlockSpec((1, 6, tn), mod_idx),
        ],
        out_specs=pl.BlockSpec((1, tm, tn), lambda bi, m, n: (bi, m, n)),
        out_shape=jax.ShapeDtypeStruct((b, l, d), F32),
        compiler_params=_cparams(("parallel", "parallel", "arbitrary"), vmem),
        name="ffn_down",
    )(act, w, x, mod)


def _pool_kernel(xm_ref, xp_ref, xn_ref, mod_ref, g_ref, w_ref, ps_ref, o_ref, *, tm, n_m, seq_len):
    m = pl.program_id(1)
    g = g_ref[...]
    shift = mod_ref[0, 0:1, :]
    scale = mod_ref[0, 1:2, :]
    xm = xm_ref[0]
    hp = _norm_mod(xp_ref[0], g, shift, scale) * (m > 0).astype(F32)
    hn = _norm_mod(xn_ref[0], g, shift, scale) * (m < n_m - 1).astype(F32)
    hh = jnp.concatenate([hp, _norm_mod(xm, g, shift, scale), hn], axis=0)
    rows = tm + 2 * HALO
    t = (m * tm + lax.broadcasted_iota(jnp.int32, (tm, 1), 0)).astype(F32)
    gw = hh.shape[1] // len(POOL_WINDOWS)
    gate = mod_ref[0, 2:3, :]
    for gi, w in enumerate(POOL_WINDOWS):
        sl = slice(gi * gw, (gi + 1) * gw)
        hg = hh[:, sl]
        acc = hg
        span = 1
        while span < w:
            acc = acc + pltpu.roll(acc, span, 0)
            span *= 2
        ahead = w // 2 - 1
        if ahead:
            acc = pltpu.roll(acc, rows - ahead, 0)
        win = acc[HALO:HALO + tm]
        cnt = jnp.minimum(t + (w // 2 - 1), seq_len - 1.0) - jnp.maximum(t - w // 2, 0.0) + 1.0
        p = (win / cnt - hg[HALO:HALO + tm]).astype(BF16)
        y = _dot(p, w_ref[gi].astype(BF16)) * ps_ref[:, sl]
        o_ref[0, :, sl] = xm[:, sl] + gate[:, sl] * y


def _pool_mix(x, mod, g, pool_w, pool_scale, tm):
    b, l, d = x.shape
    n_m = l // tm
    hb = tm // HALO
    ng, gw, _ = pool_w.shape
    vmem = 4 * tm * d * 4 + 2 * ng * gw * gw * 4 + 8 * (tm + 2 * HALO) * d * 4 + (6 << 20)
    return pl.pallas_call(
        functools.partial(_pool_kernel, tm=tm, n_m=n_m, seq_len=l),
        grid=(b, n_m),
        in_specs=[
            pl.BlockSpec((1, tm, d), lambda bi, m: (bi, m, 0)),
            pl.BlockSpec((1, HALO, d), lambda bi, m: (bi, jnp.maximum(m * hb - 1, 0), 0)),
            pl.BlockSpec((1, HALO, d), lambda bi, m: (bi, jnp.minimum((m + 1) * hb, l // HALO - 1), 0)),
            pl.BlockSpec((1, 6, d), lambda bi, m: (bi, 0, 0)),
            pl.BlockSpec((1, d), lambda bi, m: (0, 0)),
            pl.BlockSpec((ng, gw, gw), lambda bi, m: (0, 0, 0)),
            pl.BlockSpec((1, d), lambda bi, m: (0, 0)),
        ],
        out_specs=pl.BlockSpec((1, tm, d), lambda bi, m: (bi, m, 0)),
        out_shape=jax.ShapeDtypeStruct((b, l, d), F32),
        compiler_params=_cparams(("parallel", "parallel"), vmem),
        name="pool_mix",
    )(x, x, x, mod, g, pool_w, pool_scale)


def _final_norm_kernel(x_ref, g_ref, o_ref):
    x = x_ref[0]
    o_ref[0] = x * lax.rsqrt(jnp.mean(x * x, axis=-1, keepdims=True) + EPS) * g_ref[...]


def _final_norm(x, g, tm):
    b, l, d = x.shape
    return pl.pallas_call(
        _final_norm_kernel,
        grid=(b, l // tm),
        in_specs=[pl.BlockSpec((1, tm, d), lambda bi, m: (bi, m, 0)), pl.BlockSpec((1, d), lambda bi, m: (0, 0))],
        out_specs=pl.BlockSpec((1, tm, d), lambda bi, m: (bi, m, 0)),
        out_shape=jax.ShapeDtypeStruct((b, l, d), F32),
        compiler_params=_cparams(("parallel", "parallel"), 6 * tm * d * 4 + (4 << 20)),
        name="final_norm",
    )(x, g)


def _axial_angles(rows, dim):
    row = jnp.broadcast_to(jnp.arange(rows, dtype=F32)[:, None], (rows, GRID_W)).reshape(-1)
    col = jnp.broadcast_to(jnp.arange(GRID_W, dtype=F32)[None, :], (rows, GRID_W)).reshape(-1)
    n_freq = dim // 4
    inv = ROPE_BASE ** (-jnp.arange(n_freq, dtype=F32) / n_freq)
    return jnp.concatenate([row[:, None] * inv, col[:, None] * inv], axis=-1)


def _rope_tables(rows, dim):
    ang = _axial_angles(rows, dim)
    cos, sin = jnp.cos(ang), jnp.sin(ang)
    cos_t = jnp.concatenate([cos, cos], axis=-1)
    sin_t = jnp.concatenate([-sin, sin], axis=-1)
    pad = LANES - dim
    if pad:
        cos_t = jnp.pad(cos_t, ((0, 0), (0, pad)))
        sin_t = jnp.pad(sin_t, ((0, 0), (0, pad)))
    return cos_t, sin_t


def _row_tile(l, want):
    return min(l, want)


def _conv_ffn(x, mod, mod_row, g, w_up, conv_w, conv_b, w_down, layer):
    l = x.shape[1]
    act = _ffn_up(x, mod, mod_row, g, w_up, conv_w, conv_b[:, None, :], layer, tm=_row_tile(l, 1024), tn=512)
    return _ffn_down(act, w_down, x, mod, mod_row, layer, tm=_row_tile(l, 1024), tn=256)


def kernel(x, c, ctx, c_ctx, ada_w, ada_b, norm1_g, norm2_g, ffn_w_up, ffn_conv_w, ffn_conv_b, ffn_w_down, mix_w_in, mla_q_norm_g, mla_w_uq, mla_kv_norm_g, mla_w_ukv, ret_decay_f, ret_decay_b, mix_w_out, pool_w, pool_scale, final_g):
    b, l, d = x.shape
    depth = ada_w.shape[0]
    heads = ret_decay_f.shape[1]
    ret_w = heads * HEAD_DIM
    q_rank = mla_q_norm_g.shape[1]
    kv_rank = mla_kv_norm_g.shape[1]
    rows = l // GRID_W
    rope_ret = _rope_tables(rows, HEAD_DIM)
    rope_mla = _rope_tables(rows, MLA_DR)

    ctx_row = b
    cc = jnp.concatenate([c, c_ctx[None, :], jnp.zeros((-(b + 1) % 8, d), F32)], axis=0)
    mods = _ada(cc, ada_w, ada_b).reshape(depth, cc.shape[0], 6, d)

    x_lat, x_ctx = x, ctx
    for layer in range(depth):
        j = layer // 2
        with_ctx = layer < depth - 1
        mod = mods[layer]
        g1 = norm1_g[layer][None, :]
        g2 = norm2_g[layer][None, :]
        if layer % 2 == 0:
            w_in = mix_w_in[j]
            w_tail = jnp.pad(w_in[:, 4 * ret_w:], ((0, 0), (0, LANES - MLA_DR))).astype(BF16)
            wq = mla_w_uq[j].reshape(q_rank, heads, HEAD_DIM + MLA_DR)
            wq = jnp.pad(wq, ((0, 0), (0, 0), (0, HEAD_DIM - MLA_DR))).reshape(q_rank, heads * 2 * HEAD_DIM)
            wkv = mla_w_ukv[j].reshape(kv_rank, heads, 2 * HEAD_DIM)
            wkn = wkv[:, :, :HEAD_DIM].reshape(kv_rank, ret_w).astype(BF16)
            wv = wkv[:, :, HEAD_DIM:].reshape(kv_rank, ret_w).astype(BF16)
            mla_w = (w_tail, mla_q_norm_g[j][None, :], wq.astype(BF16), mla_kv_norm_g[j][None, :], wkn, wv)
            dec = jnp.stack([ret_decay_f[j], ret_decay_b[j]]).astype(F32)

            ret_lat = _inproj_ret(x_lat, mod, None, g1, w_in, rope_ret, ret_w, tm=_row_tile(l, 1024))
            q_lat, k_lat, v_lat = _inproj_mla(x_lat, mod, None, g1, *mla_w, rope_mla, heads, tm=_row_tile(l, 512))
            lc = x_ctx.shape[1]
            ret_ctx = _inproj_ret(x_ctx, mod, ctx_row, g1, w_in, None, ret_w, tm=_row_tile(lc, 1024))
            q_ctx, k_ctx, v_ctx = _inproj_mla(x_ctx, mod, ctx_row, g1, *mla_w, None, heads, tm=_row_tile(lc, 512))

            o_ret_lat, o_ret_ctx = _retention(dec, ret_lat, ret_ctx, heads)
            o_mla_lat = _attention(q_lat, k_ctx, v_ctx, k_lat, v_lat, heads, tq=_row_tile(l, 512))
            x_lat = _mix_out(o_ret_lat, o_mla_lat, mix_w_out[j], x_lat, mod, None, tm=_row_tile(l, 1024), tn=512)
            if with_ctx:
                o_mla_ctx = _attention(q_ctx, k_ctx, v_ctx, None, None, heads, tq=lc)
                x_ctx = _mix_out(o_ret_ctx, o_mla_ctx, mix_w_out[j], x_ctx, mod, ctx_row,
                                 tm=_row_tile(lc, 1024), tn=512)
        else:
            x_lat_new = _pool_mix(x_lat, mod, g1, pool_w[j], pool_scale[j][None, :], tm=_row_tile(l, 512))
            if with_ctx:
                mod_c = jnp.broadcast_to(mod[ctx_row][None], (x_ctx.shape[0], 6, d))
                x_ctx = _pool_mix(x_ctx, mod_c, g1, pool_w[j], pool_scale[j][None, :],
                                  tm=_row_tile(x_ctx.shape[1], 512))
            x_lat = x_lat_new
        x_lat = _conv_ffn(x_lat, mod, None, g2, ffn_w_up, ffn_conv_w, ffn_conv_b, ffn_w_down, layer)
        if with_ctx:
            x_ctx = _conv_ffn(x_ctx, mod, ctx_row, g2, ffn_w_up, ffn_conv_w, ffn_conv_b, ffn_w_down, layer)
    return _final_norm(x_lat, final_g[None, :], tm=_row_tile(l, 512))
```

```python
import functools

import jax
import jax.numpy as jnp
from jax import lax
from jax.experimental import pallas as pl
from jax.experimental.pallas import tpu as pltpu

F32 = jnp.float32
BF16 = jnp.bfloat16

GRID_W = 64
HEAD_DIM = 128
RET_CHUNK = 128
MLA_DR = 64
POOL_WINDOWS = (2, 4, 8, 16)
ROPE_BASE = 10000.0
EPS = 1e-6

LANES = 128
HALO = 16
V7X_VMEM_BYTES = 64 * 1024 * 1024
VMEM_CAP = V7X_VMEM_BYTES - 8 * 1024 * 1024

SHIFT1, SCALE1, GATE1, SHIFT2, SCALE2, GATE2 = range(6)

TILES = dict(
    prenorm_tm=512,
    inproj_ret_tm=1024,
    inproj_mla_tm=512,
    attn_tq=512,
    mix_out_tm=512,
    ffn_up_tm=1024, ffn_up_tn=512,
    ffn_down_tm=512, ffn_down_tn=1024,
    pool_tm=512,
    final_tm=512,
    ada_tn=1024,
)


def _tile(name, extent):
    return min(extent, TILES[name])


def _cparams(sem, vmem_bytes):
    return pltpu.CompilerParams(dimension_semantics=sem, vmem_limit_bytes=min(int(vmem_bytes), VMEM_CAP))


def _silu(x):
    return x / (1.0 + jnp.exp(-x))


def _norm_mod(x, g, shift, scale):
    y = x * lax.rsqrt(jnp.mean(x * x, axis=-1, keepdims=True) + EPS) * g
    return y * (1.0 + scale) + shift


def _dot(a, b):
    return jnp.dot(a, b, preferred_element_type=F32)


def _dot_nt(a, b):
    return lax.dot_general(a, b, (((1,), (1,)), ((), ())), preferred_element_type=F32)


def _dot_tn(a, b):
    return lax.dot_general(a, b, (((0,), (0,)), ((), ())), preferred_element_type=F32)


def _mod_index(mod_row, batch_axis, col_axis=None):
    def index(*ids):
        row = ids[batch_axis] if mod_row is None else mod_row
        return (row, 0, 0 if col_axis is None else ids[col_axis])
    return index


def _ada_kernel(c_ref, w_ref, b_ref, o_ref):
    s = _silu(c_ref[...]).astype(BF16)
    o_ref[0] = _dot(s, w_ref[0].astype(BF16)) + b_ref[0]


def _ada(cc, ada_w, ada_b):
    depth, d, n6 = ada_w.shape
    rows = cc.shape[0]
    tn = _tile("ada_tn", n6)
    return pl.pallas_call(
        _ada_kernel,
        grid=(depth, n6 // tn),
        in_specs=[
            pl.BlockSpec((rows, d), lambda l, n: (0, 0)),
            pl.BlockSpec((1, d, tn), lambda l, n: (l, 0, n)),
            pl.BlockSpec((1, 1, tn), lambda l, n: (l, 0, n)),
        ],
        out_specs=pl.BlockSpec((1, rows, tn), lambda l, n: (l, 0, n)),
        out_shape=jax.ShapeDtypeStruct((depth, rows, n6), F32),
        compiler_params=_cparams(("parallel", "parallel"), 2 * d * tn * 4 + d * tn * 2 + (8 << 20)),
        name="ada_mod",
    )(cc, ada_w, ada_b.reshape(depth, 1, n6))


def _prenorm_kernel(x_ref, mod_ref, g_ref, o_ref):
    o_ref[0] = _norm_mod(x_ref[0], g_ref[...], mod_ref[0, SHIFT1:SHIFT1 + 1, :],
                         mod_ref[0, SCALE1:SCALE1 + 1, :]).astype(BF16)


def _prenorm(x, mod, mod_row, g):
    b, l, d = x.shape
    tm = _tile("prenorm_tm", l)
    return pl.pallas_call(
        _prenorm_kernel,
        grid=(b, l // tm),
        in_specs=[pl.BlockSpec((1, tm, d), lambda bi, m: (bi, m, 0)),
                  pl.BlockSpec((1, 6, d), _mod_index(mod_row, 0)),
                  pl.BlockSpec((1, d), lambda bi, m: (0, 0))],
        out_specs=pl.BlockSpec((1, tm, d), lambda bi, m: (bi, m, 0)),
        out_shape=jax.ShapeDtypeStruct((b, l, d), BF16),
        compiler_params=_cparams(("parallel", "parallel"), 8 * tm * d * 4 + (4 << 20)),
        name="prenorm",
    )(x, mod, g)


def _inproj_ret_kernel(*refs, rope, heads_per_tile, k_scale):
    if rope:
        h_ref, cos_ref, sin_ref, w_ref, o_ref, w_scr = refs
    else:
        h_ref, w_ref, o_ref, w_scr = refs
    n = pl.program_id(0)

    @pl.when((pl.program_id(1) == 0) & (pl.program_id(2) == 0))
    def _():
        w_scr[...] = w_ref[...].astype(BF16)

    z = _dot(h_ref[0], w_scr[...])
    scale = jnp.where(n == 1, k_scale, 1.0).astype(F32)
    if rope:
        @pl.when(n < 2)
        def _():
            cos = cos_ref[...]
            sin = sin_ref[...]
            for h in range(heads_per_tile):
                sl = slice(h * HEAD_DIM, (h + 1) * HEAD_DIM)
                zh = z[:, sl]
                r = zh * cos + pltpu.roll(zh, HEAD_DIM // 2, 1) * sin
                o_ref[0, :, sl] = (r * scale).astype(BF16)

        @pl.when(n >= 2)
        def _():
            o_ref[0] = z.astype(BF16)
    else:
        o_ref[0] = (z * scale).astype(BF16)


def _inproj_ret(h, w_in, rope_tabs, ret_w):
    b, l, d = h.shape
    tm = _tile("inproj_ret_tm", l)
    tn = ret_w
    rope = rope_tabs is not None
    in_specs = [pl.BlockSpec((1, tm, d), lambda n, bi, m: (bi, m, 0))]
    args = [h]
    if rope:
        in_specs += [pl.BlockSpec((tm, HEAD_DIM), lambda n, bi, m: (m, 0))] * 2
        args += list(rope_tabs)
    in_specs.append(pl.BlockSpec((d, tn), lambda n, bi, m: (0, n)))
    args.append(w_in)
    vmem = 2 * tm * d * 2 + 2 * d * tn * 4 + d * tn * 2 + 2 * tm * tn * 2 + 4 * tm * tn * 4 + (6 << 20)
    return pl.pallas_call(
        functools.partial(_inproj_ret_kernel, rope=rope, heads_per_tile=tn // HEAD_DIM, k_scale=HEAD_DIM ** -0.5),
        grid=(4, b, l // tm),
        in_specs=in_specs,
        out_specs=pl.BlockSpec((1, tm, tn), lambda n, bi, m: (bi, m, n)),
        out_shape=jax.ShapeDtypeStruct((b, l, 4 * tn), BF16),
        scratch_shapes=[pltpu.VMEM((d, tn), BF16)],
        compiler_params=_cparams(("arbitrary", "arbitrary", "arbitrary"), vmem),
        name="inproj_ret_rope" if rope else "inproj_ret",
    )(*args)


def _rope_half_padded(x, cos, sin):
    rot = pltpu.roll(x, MLA_DR // 2, 1) + pltpu.roll(x, LANES - MLA_DR // 2, 1)
    return x * cos + rot * sin


def _inproj_mla_kernel(*refs, rope, heads, q_rank, kv_rank, q_scale):
    if rope:
        h_ref, cos_ref, sin_ref, wt_ref, gq_ref, wq_ref, gkv_ref, wkn_ref, wv_ref, q_out, k_out, v_out = refs
    else:
        h_ref, wt_ref, gq_ref, wq_ref, gkv_ref, wkn_ref, wv_ref, q_out, k_out, v_out = refs
    zt = _dot(h_ref[0], wt_ref[...])
    cq = zt[:, :q_rank]
    ckv = zt[:, q_rank:q_rank + kv_rank]
    kr = zt[:, q_rank + kv_rank:]
    cqn = (cq * lax.rsqrt(jnp.mean(cq * cq, axis=-1, keepdims=True) + EPS) * gq_ref[...]).astype(BF16)
    ckvn = (ckv * lax.rsqrt(jnp.mean(ckv * ckv, axis=-1, keepdims=True) + EPS) * gkv_ref[...]).astype(BF16)
    q = _dot(cqn, wq_ref[...])
    kn = _dot(ckvn, wkn_ref[...])
    v_out[0] = _dot(ckvn, wv_ref[...]).astype(BF16)
    if rope:
        cos = cos_ref[...]
        sin = sin_ref[...]
        kr = _rope_half_padded(kr, cos, sin)
    kr = kr.astype(BF16)
    for hd in range(heads):
        base = hd * 2 * HEAD_DIM
        qn = q[:, base:base + HEAD_DIM]
        qr = q[:, base + HEAD_DIM:base + 2 * HEAD_DIM]
        if rope:
            qr = _rope_half_padded(qr, cos, sin)
        q_out[0, :, base:base + HEAD_DIM] = (qn * q_scale).astype(BF16)
        q_out[0, :, base + HEAD_DIM:base + 2 * HEAD_DIM] = (qr * q_scale).astype(BF16)
        k_out[0, :, base:base + HEAD_DIM] = kn[:, hd * HEAD_DIM:(hd + 1) * HEAD_DIM].astype(BF16)
        k_out[0, :, base + HEAD_DIM:base + 2 * HEAD_DIM] = kr


def _inproj_mla(h, w_tail, gq, wq, gkv, wkn, wv, rope_tabs, heads):
    b, l, d = h.shape
    tm = _tile("inproj_mla_tm", l)
    q_rank, kv_rank = gq.shape[1], gkv.shape[1]
    rope = rope_tabs is not None
    full = lambda a: pl.BlockSpec(a.shape, lambda bi, m: (0,) * a.ndim)
    in_specs = [pl.BlockSpec((1, tm, d), lambda bi, m: (bi, m, 0))]
    args = [h]
    if rope:
        in_specs += [pl.BlockSpec((tm, LANES), lambda bi, m: (m, 0))] * 2
        args += list(rope_tabs)
    for a in (w_tail, gq, wq, gkv, wkn, wv):
        in_specs.append(full(a))
        args.append(a)
    qw = heads * 2 * HEAD_DIM
    vw = heads * HEAD_DIM
    out_spec = lambda w: pl.BlockSpec((1, tm, w), lambda bi, m: (bi, m, 0))
    wbytes = sum(a.size * a.dtype.itemsize for a in (w_tail, wq, wkn, wv))
    vmem = 2 * tm * d * 2 + 2 * wbytes + 2 * tm * (2 * qw + vw) * 2 + tm * (8 * qw * 4) + (6 << 20)
    return pl.pallas_call(
        functools.partial(_inproj_mla_kernel, rope=rope, heads=heads, q_rank=q_rank, kv_rank=kv_rank,
                          q_scale=(HEAD_DIM + MLA_DR) ** -0.5),
        grid=(b, l // tm),
        in_specs=in_specs,
        out_specs=[out_spec(qw), out_spec(qw), out_spec(vw)],
        out_shape=[jax.ShapeDtypeStruct((b, l, qw), BF16), jax.ShapeDtypeStruct((b, l, qw), BF16),
                   jax.ShapeDtypeStruct((b, l, vw), BF16)],
        compiler_params=_cparams(("parallel", "parallel"), vmem),
        name="inproj_mla_rope" if rope else "inproj_mla",
    )(*args)


def _log_sigmoid(x):
    return jnp.minimum(x, 0.0) - jnp.log1p(jnp.exp(-jnp.abs(x)))


def _retention_kernel(dec_ref, ql_ref, kl_ref, vl_ref, gl_ref, qc_ref, kc_ref, vc_ref, gc_ref, ol_ref, oc_ref,
                      kv_scr, *, n_ctx, n_lat):
    c = RET_CHUNK
    hd = pl.program_id(1)
    lf = _log_sigmoid(jnp.full((c, c), dec_ref[0, hd], F32))
    lb = _log_sigmoid(jnp.full((c, c), dec_ref[1, hd], F32))
    ii = lax.broadcasted_iota(jnp.int32, (c, c), 0).astype(F32)
    jj = lax.broadcasted_iota(jnp.int32, (c, c), 1).astype(F32)
    diff = ii - jj
    fwd = diff >= 0
    decay = jnp.where(fwd, jnp.exp(jnp.where(fwd, diff, 0.0) * lf), jnp.exp(jnp.where(fwd, 0.0, -diff) * lb))
    q_dec_f = jnp.exp((ii + 1.0) * lf)
    k_dec_f = jnp.exp((c - 1.0 - ii) * lf)
    c_dec_f = jnp.exp(c * lf)
    q_dec_b = jnp.exp((c - ii) * lb)
    k_dec_b = jnp.exp(ii * lb)
    c_dec_b = jnp.exp(c * lb)

    def chunk(ref, i):
        return ref[0, pl.ds(pl.multiple_of(i * c, c), c), :]

    def kv_store(slot, k, v):
        kf = k.astype(F32)
        kv_scr[0, slot] = _dot_tn((kf * k_dec_f).astype(BF16), v)
        kv_scr[1, slot] = _dot_tn((kf * k_dec_b).astype(BF16), v)

    for i in range(n_ctx):
        kv_store(i, chunk(kc_ref, i), chunk(vc_ref, i))

    def kv_body(i, carry):
        kv_store(n_ctx + i, chunk(kl_ref, i), chunk(vl_ref, i))
        return carry

    lax.fori_loop(0, n_lat, kv_body, 0)

    s = jnp.zeros((c, c), F32)
    for slot in range(n_ctx + n_lat):
        kv = kv_scr[0, slot]
        kv_scr[0, slot] = s
        s = s * c_dec_f + kv
    s = jnp.zeros((c, c), F32)
    for slot in list(range(n_ctx - 1, -1, -1)) + list(range(n_ctx + n_lat - 1, n_ctx - 1, -1)):
        kv = kv_scr[1, slot]
        kv_scr[1, slot] = s
        s = s * c_dec_b + kv

    def out_chunk(slot, q, k, v, gate):
        p = (_dot_nt(q, k) * decay).astype(BF16)
        qf = q.astype(F32)
        o = _dot(p, v)
        o = o + _dot((qf * q_dec_f).astype(BF16), kv_scr[0, slot].astype(BF16))
        o = o + _dot((qf * q_dec_b).astype(BF16), kv_scr[1, slot].astype(BF16))
        o = o * lax.rsqrt(jnp.mean(o * o, axis=-1, keepdims=True) + EPS)
        return (o * _silu(gate.astype(F32))).astype(BF16)

    for i in range(n_ctx):
        oc_ref[0, i * c:(i + 1) * c, :] = out_chunk(i, chunk(qc_ref, i), chunk(kc_ref, i), chunk(vc_ref, i),
                                                    chunk(gc_ref, i))

    def out_body(i, carry):
        ol_ref[0, pl.ds(pl.multiple_of(i * c, c), c), :] = out_chunk(
            n_ctx + i, chunk(ql_ref, i), chunk(kl_ref, i), chunk(vl_ref, i), chunk(gl_ref, i))
        return carry

    lax.fori_loop(0, n_lat, out_body, 0)


def _retention(dec, ret_lat, ret_ctx, heads):
    b, l, _ = ret_lat.shape
    lc = ret_ctx.shape[1]
    n_lat, n_ctx = l // RET_CHUNK, lc // RET_CHUNK
    col = lambda rows, group: pl.BlockSpec((1, rows, HEAD_DIM), lambda bi, h: (bi, 0, group * heads + h))
    in_specs = [pl.BlockSpec(memory_space=pltpu.SMEM)]
    in_specs += [col(l, grp) for grp in range(4)] + [col(lc, grp) for grp in range(4)]
    out_col = lambda rows: pl.BlockSpec((1, rows, HEAD_DIM), lambda bi, h: (bi, 0, h))
    vmem = 2 * 5 * (l + lc) * HEAD_DIM * 2 + 2 * (n_lat + n_ctx) * RET_CHUNK * HEAD_DIM * 4 + (8 << 20)
    return pl.pallas_call(
        functools.partial(_retention_kernel, n_ctx=n_ctx, n_lat=n_lat),
        grid=(b, heads),
        in_specs=in_specs,
        out_specs=[out_col(l), out_col(lc)],
        out_shape=[jax.ShapeDtypeStruct((b, l, heads * HEAD_DIM), BF16),
                   jax.ShapeDtypeStruct((b, lc, heads * HEAD_DIM), BF16)],
        scratch_shapes=[pltpu.VMEM((2, n_lat + n_ctx, RET_CHUNK, HEAD_DIM), F32)],
        compiler_params=_cparams(("parallel", "parallel"), vmem),
        name="retention",
    )(dec, ret_lat, ret_lat, ret_lat, ret_lat, ret_ctx, ret_ctx, ret_ctx, ret_ctx)


def _attn_kernel(*refs, with_lat):
    if with_lat:
        q_ref, kc_ref, vc_ref, kl_ref, vl_ref, o_ref = refs
    else:
        q_ref, kc_ref, vc_ref, o_ref = refs
    q = q_ref[0]
    sc = _dot_nt(q, kc_ref[0])
    m = jnp.max(sc, axis=-1, keepdims=True)
    if with_lat:
        sl = _dot_nt(q, kl_ref[0])
        m = jnp.maximum(m, jnp.max(sl, axis=-1, keepdims=True))
    pc = jnp.exp(sc - m)
    den = jnp.sum(pc, axis=-1, keepdims=True)
    o = _dot(pc.astype(BF16), vc_ref[0])
    if with_lat:
        p_lat = jnp.exp(sl - m)
        den = den + jnp.sum(p_lat, axis=-1, keepdims=True)
        o = o + _dot(p_lat.astype(BF16), vl_ref[0])
    o_ref[0] = (o / den).astype(BF16)


def _attention(q, k_ctx, v_ctx, k_lat, v_lat, heads):
    b, lq, _ = q.shape
    tq = _tile("attn_tq", lq)
    lc = k_ctx.shape[1]
    with_lat = k_lat is not None
    kw = 2 * HEAD_DIM
    in_specs = [
        pl.BlockSpec((1, tq, kw), lambda bi, h, i: (bi, i, h)),
        pl.BlockSpec((1, lc, kw), lambda bi, h, i: (bi, 0, h)),
        pl.BlockSpec((1, lc, HEAD_DIM), lambda bi, h, i: (bi, 0, h)),
    ]
    args = [q, k_ctx, v_ctx]
    lk = lc
    if with_lat:
        ll = k_lat.shape[1]
        lk += ll
        in_specs += [pl.BlockSpec((1, ll, kw), lambda bi, h, i: (bi, 0, h)),
                     pl.BlockSpec((1, ll, HEAD_DIM), lambda bi, h, i: (bi, 0, h))]
        args += [k_lat, v_lat]
    vmem = 2 * (tq * kw + lk * (kw + HEAD_DIM) + tq * HEAD_DIM) * 2 + 3 * tq * lk * 4 + (8 << 20)
    return pl.pallas_call(
        functools.partial(_attn_kernel, with_lat=with_lat),
        grid=(b, heads, lq // tq),
        in_specs=in_specs,
        out_specs=pl.BlockSpec((1, tq, HEAD_DIM), lambda bi, h, i: (bi, i, h)),
        out_shape=jax.ShapeDtypeStruct((b, lq, heads * HEAD_DIM), BF16),
        compiler_params=_cparams(("parallel", "parallel", "arbitrary"), vmem),
        name="mla_attn" if with_lat else "mla_attn_ctx",
    )(*args)


def _mix_out_kernel(a1_ref, a2_ref, w_ref, x_ref, mod_ref, g2_ref, x_out, h_out):
    k1 = a1_ref.shape[2]
    y = _dot(a1_ref[0], w_ref[:k1, :]) + _dot(a2_ref[0], w_ref[k1:, :])
    x1 = x_ref[0] + mod_ref[0, GATE1:GATE1 + 1, :] * y
    x_out[0] = x1
    h_out[0] = _norm_mod(x1, g2_ref[...], mod_ref[0, SHIFT2:SHIFT2 + 1, :],
                         mod_ref[0, SCALE2:SCALE2 + 1, :]).astype(BF16)


def _mix_out(a1, a2, w_bf16, x, mod, mod_row, g2):
    b, l, d = x.shape
    tm = _tile("mix_out_tm", l)
    k1, k2 = a1.shape[2], a2.shape[2]
    row = lambda w: pl.BlockSpec((1, tm, w), lambda bi, m: (bi, m, 0))
    vmem = (k1 + k2) * d * 2 + 2 * tm * (k1 + k2) * 2 + 4 * tm * d * 4 + 2 * tm * d * 2 + 6 * tm * d * 4 + (4 << 20)
    return pl.pallas_call(
        _mix_out_kernel,
        grid=(b, l // tm),
        in_specs=[row(k1), row(k2),
                  pl.BlockSpec((k1 + k2, d), lambda bi, m: (0, 0)),
                  row(d),
                  pl.BlockSpec((1, 6, d), _mod_index(mod_row, 0)),
                  pl.BlockSpec((1, d), lambda bi, m: (0, 0))],
        out_specs=[row(d), row(d)],
        out_shape=[jax.ShapeDtypeStruct((b, l, d), F32), jax.ShapeDtypeStruct((b, l, d), BF16)],
        compiler_params=_cparams(("parallel", "parallel"), vmem),
        name="mix_out",
    )(a1, a2, w_bf16, x, mod, g2)


def _ffn_up_kernel(hm_ref, hp_ref, hn_ref, wa_ref, wg_ref, cwa_ref, cwg_ref, cba_ref, cbg_ref, o_ref,
                   lhs_scr, wa_scr, wg_scr, *, tm, n_m):
    m = pl.program_id(2)

    @pl.when((pl.program_id(1) == 0) & (m == 0))
    def _():
        wa_scr[...] = wa_ref[...].astype(BF16)
        wg_scr[...] = wg_ref[...].astype(BF16)

    zero = jnp.zeros((HALO, lhs_scr.shape[1]), BF16)
    lhs_scr[0:HALO, :] = jnp.where(m > 0, hp_ref[0], zero)
    lhs_scr[HALO:HALO + tm, :] = hm_ref[0]
    lhs_scr[HALO + tm:, :] = jnp.where(m < n_m - 1, hn_ref[0], zero)
    h = lhs_scr[...]
    rows = tm + 2 * HALO

    def conv(w_scr, cw_ref, cb_ref):
        u = _dot(h, w_scr[...])
        prev = pltpu.roll(u, 1, 0)[HALO:HALO + tm]
        nxt = pltpu.roll(u, rows - 1, 0)[HALO:HALO + tm]
        cw = cw_ref[...]
        return cw[0:1, :] * prev + cw[1:2, :] * u[HALO:HALO + tm] + cw[2:3, :] * nxt + cb_ref[...]

    a = conv(wa_scr, cwa_ref, cba_ref)
    gt = conv(wg_scr, cwg_ref, cbg_ref)
    o_ref[0] = (_silu(gt) * a).astype(BF16)


def _ffn_up(h, w_up, conv_w, conv_b, layer):
    b, l, d = h.shape
    tm = _tile("ffn_up_tm", l)
    tn = TILES["ffn_up_tn"]
    f = w_up.shape[2] // 2
    n_m, n_n = l // tm, f // tn
    hb = tm // HALO
    col = lambda rows, off: pl.BlockSpec((None, rows, tn), lambda n, bi, m: (layer, 0, n + off))
    vmem = (2 * tm * d * 2 + (tm + 2 * HALO) * d * 2 + 4 * d * tn * 4 + 2 * d * tn * 2
            + 2 * tm * tn * 2 + 8 * (tm + 2 * HALO) * tn * 4 + (4 << 20))
    return pl.pallas_call(
        functools.partial(_ffn_up_kernel, tm=tm, n_m=n_m),
        grid=(n_n, b, n_m),
        in_specs=[
            pl.BlockSpec((1, tm, d), lambda n, bi, m: (bi, m, 0)),
            pl.BlockSpec((1, HALO, d), lambda n, bi, m: (bi, jnp.maximum(m * hb - 1, 0), 0)),
            pl.BlockSpec((1, HALO, d), lambda n, bi, m: (bi, jnp.minimum((m + 1) * hb, l // HALO - 1), 0)),
            col(d, 0), col(d, n_n), col(3, 0), col(3, n_n), col(1, 0), col(1, n_n),
        ],
        out_specs=pl.BlockSpec((1, tm, tn), lambda n, bi, m: (bi, m, n)),
        out_shape=jax.ShapeDtypeStruct((b, l, f), BF16),
        scratch_shapes=[pltpu.VMEM((tm + 2 * HALO, d), BF16), pltpu.VMEM((d, tn), BF16), pltpu.VMEM((d, tn), BF16)],
        compiler_params=_cparams(("arbitrary", "arbitrary", "arbitrary"), vmem),
        name="ffn_up",
    )(h, h, h, w_up, w_up, conv_w, conv_w, conv_b, conv_b)


def _ffn_down_kernel(a_ref, w_ref, x_ref, mod_ref, o_ref):
    y = _dot(a_ref[0], w_ref[...])
    o_ref[0] = x_ref[0] + mod_ref[0, GATE2:GATE2 + 1, :] * y


def _ffn_down(act, w_bf16, x, mod, mod_row, layer):
    b, l, d = x.shape
    tm = _tile("ffn_down_tm", l)
    tn = _tile("ffn_down_tn", d)
    f = act.shape[2]
    vmem = 2 * tm * f * 2 + 2 * f * tn * 2 + 5 * tm * tn * 4 + (6 << 20)
    return pl.pallas_call(
        _ffn_down_kernel,
        grid=(d // tn, b, l // tm),
        in_specs=[
            pl.BlockSpec((1, tm, f), lambda n, bi, m: (bi, m, 0)),
            pl.BlockSpec((None, f, tn), lambda n, bi, m: (layer, 0, n)),
            pl.BlockSpec((1, tm, tn), lambda n, bi, m: (bi, m, n)),
            pl.BlockSpec((1, 6, tn), _mod_index(mod_row, 1, col_axis=0)),
        ],
        out_specs=pl.BlockSpec((1, tm, tn), lambda n, bi, m: (bi, m, n)),
        out_shape=jax.ShapeDtypeStruct((b, l, d), F32),
        compiler_params=_cparams(("parallel", "parallel", "parallel"), vmem),
        name="ffn_down",
    )(act, w_bf16, x, mod)


def _pool_kernel(xm_ref, xp_ref, xn_ref, mod_ref, g_ref, w_ref, ps_ref, g2_ref, x_out, h_out, x1_scr,
                 *, tm, n_m, seq_len):
    m = pl.program_id(1)
    g = g_ref[...]
    shift = mod_ref[0, SHIFT1:SHIFT1 + 1, :]
    scale = mod_ref[0, SCALE1:SCALE1 + 1, :]
    xm = xm_ref[0]
    hp = _norm_mod(xp_ref[0], g, shift, scale) * (m > 0).astype(F32)
    hn = _norm_mod(xn_ref[0], g, shift, scale) * (m < n_m - 1).astype(F32)
    hh = jnp.concatenate([hp, _norm_mod(xm, g, shift, scale), hn], axis=0)
    rows = tm + 2 * HALO
    t = (m * tm + lax.broadcasted_iota(jnp.int32, (tm, 1), 0)).astype(F32)
    gw = hh.shape[1] // len(POOL_WINDOWS)
    gate = mod_ref[0, GATE1:GATE1 + 1, :]
    for gi, w in enumerate(POOL_WINDOWS):
        sl = slice(gi * gw, (gi + 1) * gw)
        hg = hh[:, sl]
        acc = hg
        span = 1
        while span < w:
            acc = acc + pltpu.roll(acc, span, 0)
            span *= 2
        ahead = w // 2 - 1
        if ahead:
            acc = pltpu.roll(acc, rows - ahead, 0)
        win = acc[HALO:HALO + tm]
        cnt = jnp.minimum(t + (w // 2 - 1), seq_len - 1.0) - jnp.maximum(t - w // 2, 0.0) + 1.0
        p = (win / cnt - hg[HALO:HALO + tm]).astype(BF16)
        y = _dot(p, w_ref[gi].astype(BF16)) * ps_ref[:, sl]
        x1_scr[:, sl] = xm[:, sl] + gate[:, sl] * y
    x1 = x1_scr[...]
    x_out[0] = x1
    h_out[0] = _norm_mod(x1, g2_ref[...], mod_ref[0, SHIFT2:SHIFT2 + 1, :],
                         mod_ref[0, SCALE2:SCALE2 + 1, :]).astype(BF16)


def _pool_mix(x, mod, mod_row, g, pool_w, pool_scale, g2):
    b, l, d = x.shape
    tm = _tile("pool_tm", l)
    n_m = l // tm
    hb = tm // HALO
    ng, gw, _ = pool_w.shape
    row = pl.BlockSpec((1, tm, d), lambda bi, m: (bi, m, 0))
    vec = pl.BlockSpec((1, d), lambda bi, m: (0, 0))
    vmem = 7 * tm * d * 4 + 2 * ng * gw * gw * 4 + 8 * (tm + 2 * HALO) * d * 4 + (6 << 20)
    return pl.pallas_call(
        functools.partial(_pool_kernel, tm=tm, n_m=n_m, seq_len=l),
        grid=(b, n_m),
        in_specs=[
            row,
            pl.BlockSpec((1, HALO, d), lambda bi, m: (bi, jnp.maximum(m * hb - 1, 0), 0)),
            pl.BlockSpec((1, HALO, d), lambda bi, m: (bi, jnp.minimum((m + 1) * hb, l // HALO - 1), 0)),
            pl.BlockSpec((1, 6, d), _mod_index(mod_row, 0)),
            vec,
            pl.BlockSpec((ng, gw, gw), lambda bi, m: (0, 0, 0)),
            vec, vec,
        ],
        out_specs=[row, row],
        out_shape=[jax.ShapeDtypeStruct((b, l, d), F32), jax.ShapeDtypeStruct((b, l, d), BF16)],
        scratch_shapes=[pltpu.VMEM((tm, d), F32)],
        compiler_params=_cparams(("parallel", "parallel"), vmem),
        name="pool_mix",
    )(x, x, x, mod, g, pool_w, pool_scale, g2)


def _final_norm_kernel(x_ref, g_ref, o_ref):
    x = x_ref[0]
    o_ref[0] = x * lax.rsqrt(jnp.mean(x * x, axis=-1, keepdims=True) + EPS) * g_ref[...]


def _final_norm(x, g):
    b, l, d = x.shape
    tm = _tile("final_tm", l)
    return pl.pallas_call(
        _final_norm_kernel,
        grid=(b, l // tm),
        in_specs=[pl.BlockSpec((1, tm, d), lambda bi, m: (bi, m, 0)), pl.BlockSpec((1, d), lambda bi, m: (0, 0))],
        out_specs=pl.BlockSpec((1, tm, d), lambda bi, m: (bi, m, 0)),
        out_shape=jax.ShapeDtypeStruct((b, l, d), F32),
        compiler_params=_cparams(("parallel", "parallel"), 6 * tm * d * 4 + (4 << 20)),
        name="final_norm",
    )(x, g)


def _axial_angles(rows, dim):
    row = jnp.broadcast_to(jnp.arange(rows, dtype=F32)[:, None], (rows, GRID_W)).reshape(-1)
    col = jnp.broadcast_to(jnp.arange(GRID_W, dtype=F32)[None, :], (rows, GRID_W)).reshape(-1)
    n_freq = dim // 4
    inv = ROPE_BASE ** (-jnp.arange(n_freq, dtype=F32) / n_freq)
    return jnp.concatenate([row[:, None] * inv, col[:, None] * inv], axis=-1)


def _rope_tables(rows, dim):
    ang = _axial_angles(rows, dim)
    cos, sin = jnp.cos(ang), jnp.sin(ang)
    cos_t = jnp.concatenate([cos, cos], axis=-1)
    sin_t = jnp.concatenate([-sin, sin], axis=-1)
    pad = LANES - dim
    if pad:
        cos_t = jnp.pad(cos_t, ((0, 0), (0, pad)))
        sin_t = jnp.pad(sin_t, ((0, 0), (0, pad)))
    return cos_t, sin_t


def _mla_weights(w_in, g_q, w_uq, g_kv, w_ukv, heads, ret_w):
    q_rank, kv_rank = g_q.shape[0], g_kv.shape[0]
    w_tail = jnp.pad(w_in[:, 4 * ret_w:], ((0, 0), (0, LANES - MLA_DR))).astype(BF16)
    wq = w_uq.reshape(q_rank, heads, HEAD_DIM + MLA_DR)
    wq = jnp.pad(wq, ((0, 0), (0, 0), (0, HEAD_DIM - MLA_DR))).reshape(q_rank, heads * 2 * HEAD_DIM).astype(BF16)
    wkv = w_ukv.reshape(kv_rank, heads, 2 * HEAD_DIM)
    wkn = wkv[:, :, :HEAD_DIM].reshape(kv_rank, ret_w).astype(BF16)
    wv = wkv[:, :, HEAD_DIM:].reshape(kv_rank, ret_w).astype(BF16)
    return w_tail, g_q[None, :], wq, g_kv[None, :], wkn, wv


def _conv_ffn(h, x, mod, mod_row, w_up, conv_w, conv_b, w_down_bf16, layer):
    act = _ffn_up(h, w_up, conv_w, conv_b[:, None, :], layer)
    return _ffn_down(act, w_down_bf16, x, mod, mod_row, layer)


def kernel(x, c, ctx, c_ctx, ada_w, ada_b, norm1_g, norm2_g, ffn_w_up, ffn_conv_w, ffn_conv_b, ffn_w_down, mix_w_in, mla_q_norm_g, mla_w_uq, mla_kv_norm_g, mla_w_ukv, ret_decay_f, ret_decay_b, mix_w_out, pool_w, pool_scale, final_g):
    b, l, d = x.shape
    lc = ctx.shape[1]
    depth = ada_w.shape[0]
    heads = ret_decay_f.shape[1]
    ret_w = heads * HEAD_DIM
    rows = l // GRID_W
    rope_ret = _rope_tables(rows, HEAD_DIM)
    rope_mla = _rope_tables(rows, MLA_DR)
    w_down_bf16 = ffn_w_down.astype(BF16)

    ctx_row = b
    cc = jnp.concatenate([c, c_ctx[None, :], jnp.zeros((-(b + 1) % 8, d), F32)], axis=0)
    mods = _ada(cc, ada_w, ada_b).reshape(depth, cc.shape[0], 6, d)

    x_lat, x_ctx = x, ctx
    for layer in range(depth):
        j = layer // 2
        with_ctx = layer < depth - 1
        mod = mods[layer]
        g1 = norm1_g[layer][None, :]
        g2 = norm2_g[layer][None, :]
        h_ctx = None
        if layer % 2 == 0:
            w_in = mix_w_in[j]
            mla_w = _mla_weights(w_in, mla_q_norm_g[j], mla_w_uq[j], mla_kv_norm_g[j], mla_w_ukv[j], heads, ret_w)
            w_out_bf16 = mix_w_out[j].astype(BF16)
            dec = jnp.stack([ret_decay_f[j], ret_decay_b[j]]).astype(F32)

            h1_lat = _prenorm(x_lat, mod, None, g1)
            h1_ctx = _prenorm(x_ctx, mod, ctx_row, g1).reshape(1, b * lc, d)
            ret_lat = _inproj_ret(h1_lat, w_in, rope_ret, ret_w)
            q_lat, k_lat, v_lat = _inproj_mla(h1_lat, *mla_w, rope_mla, heads)
            ret_ctx = _inproj_ret(h1_ctx, w_in, None, ret_w).reshape(b, lc, 4 * ret_w)
            q_ctx, k_ctx, v_ctx = [a.reshape(b, lc, -1) for a in _inproj_mla(h1_ctx, *mla_w, None, heads)]

            o_ret_lat, o_ret_ctx = _retention(dec, ret_lat, ret_ctx, heads)
            o_mla_lat = _attention(q_lat, k_ctx, v_ctx, k_lat, v_lat, heads)
            x_lat, h_lat = _mix_out(o_ret_lat, o_mla_lat, w_out_bf16, x_lat, mod, None, g2)
            if with_ctx:
                o_mla_ctx = _attention(q_ctx, k_ctx, v_ctx, None, None, heads)
                x_ctx, h_ctx = _mix_out(o_ret_ctx, o_mla_ctx, w_out_bf16, x_ctx, mod, ctx_row, g2)
        else:
            ps = pool_scale[j][None, :]
            if with_ctx:
                x_ctx, h_ctx = _pool_mix(x_ctx, mod, ctx_row, g1, pool_w[j], ps, g2)
            x_lat, h_lat = _pool_mix(x_lat, mod, None, g1, pool_w[j], ps, g2)
        x_lat = _conv_ffn(h_lat, x_lat, mod, None, ffn_w_up, ffn_conv_w, ffn_conv_b, w_down_bf16, layer)
        if with_ctx:
            x_ctx = _conv_ffn(h_ctx, x_ctx, mod, ctx_row, ffn_w_up, ffn_conv_w, ffn_conv_b, w_down_bf16, layer)
    return _final_norm(x_lat, final_g[None, :])
```

```python
import functools

import jax
import jax.numpy as jnp
from jax import lax
from jax.experimental import pallas as pl
from jax.experimental.pallas import tpu as pltpu

F32 = jnp.float32
BF16 = jnp.bfloat16

GRID_W = 64
HEAD_DIM = 128
RET_CHUNK = 128
MLA_DR = 64
POOL_WINDOWS = (2, 4, 8, 16)
ROPE_BASE = 10000.0
EPS = 1e-6
LOG2_E = 1.4426950408889634

LANES = 128
MXU_DIM = 256
HALO = 16
V7X_VMEM_BYTES = 64 * 1024 * 1024
VMEM_CAP = V7X_VMEM_BYTES - 8 * 1024 * 1024

SHIFT1, SCALE1, GATE1, SHIFT2, SCALE2, GATE2 = range(6)

TILES = dict(
    prenorm_tm=512,
    inproj_ret_tm=1024,
    inproj_mla_tm=512,
    attn_tq=512,
    mix_out_tm=512,
    ffn_up_tm=1024, ffn_up_tn=512,
    ffn_down_tm=512, ffn_down_tn=1024,
    pool_tm=512,
    final_tm=512,
    ada_tn=1024,
)


def _tile(name, extent):
    return min(extent, TILES[name])


def _cparams(sem, vmem_bytes, flags=None):
    return pltpu.CompilerParams(dimension_semantics=sem, vmem_limit_bytes=min(int(vmem_bytes), VMEM_CAP),
                                flags=flags)


def _silu(x):
    return x / (1.0 + jnp.exp(-x))


def _norm_mod(x, g, shift, scale):
    y = x * lax.rsqrt(jnp.mean(x * x, axis=-1, keepdims=True) + EPS) * g
    return y * (1.0 + scale) + shift


def _dot(a, b):
    return jnp.dot(a, b, preferred_element_type=F32)


def _dot_nt(a, b):
    return lax.dot_general(a, b, (((1,), (1,)), ((), ())), preferred_element_type=F32)


def _dot_tn(a, b):
    return lax.dot_general(a, b, (((0,), (0,)), ((), ())), preferred_element_type=F32)


def _mod_index(mod_row, batch_axis, col_axis=None):
    def index(*ids):
        row = ids[batch_axis] if mod_row is None else mod_row
        return (row, 0, 0 if col_axis is None else ids[col_axis])
    return index


def _ada_kernel(c_ref, w_ref, b_ref, o_ref):
    s = _silu(c_ref[...]).astype(BF16)
    o_ref[0] = _dot(s, w_ref[0].astype(BF16)) + b_ref[0]


def _ada(cc, ada_w, ada_b):
    depth, d, n6 = ada_w.shape
    rows = cc.shape[0]
    tn = _tile("ada_tn", n6)
    return pl.pallas_call(
        _ada_kernel,
        grid=(depth, n6 // tn),
        in_specs=[
            pl.BlockSpec((rows, d), lambda l, n: (0, 0)),
            pl.BlockSpec((1, d, tn), lambda l, n: (l, 0, n)),
            pl.BlockSpec((1, 1, tn), lambda l, n: (l, 0, n)),
        ],
        out_specs=pl.BlockSpec((1, rows, tn), lambda l, n: (l, 0, n)),
        out_shape=jax.ShapeDtypeStruct((depth, rows, n6), F32),
        compiler_params=_cparams(("parallel", "parallel"), 2 * d * tn * 4 + d * tn * 2 + (8 << 20)),
        name="ada_mod",
    )(cc, ada_w, ada_b.reshape(depth, 1, n6))


def _prenorm_kernel(x_ref, mod_ref, g_ref, o_ref):
    o_ref[0] = _norm_mod(x_ref[0], g_ref[...], mod_ref[0, SHIFT1:SHIFT1 + 1, :],
                         mod_ref[0, SCALE1:SCALE1 + 1, :]).astype(BF16)


def _prenorm(x, mod, mod_row, g):
    b, l, d = x.shape
    tm = _tile("prenorm_tm", l)
    return pl.pallas_call(
        _prenorm_kernel,
        grid=(b, l // tm),
        in_specs=[pl.BlockSpec((1, tm, d), lambda bi, m: (bi, m, 0)),
                  pl.BlockSpec((1, 6, d), _mod_index(mod_row, 0)),
                  pl.BlockSpec((1, d), lambda bi, m: (0, 0))],
        out_specs=pl.BlockSpec((1, tm, d), lambda bi, m: (bi, m, 0)),
        out_shape=jax.ShapeDtypeStruct((b, l, d), BF16),
        compiler_params=_cparams(("parallel", "parallel"), 8 * tm * d * 4 + (4 << 20)),
        name="prenorm",
    )(x, mod, g)


def _inproj_ret_kernel(*refs, rope, heads_per_tile, k_scale):
    if rope:
        h_ref, cos_ref, sin_ref, w_ref, o_ref, w_scr = refs
    else:
        h_ref, w_ref, o_ref, w_scr = refs
    n = pl.program_id(0)

    @pl.when((pl.program_id(1) == 0) & (pl.program_id(2) == 0))
    def _():
        w_scr[...] = w_ref[...].astype(BF16)

    scale = jnp.where(n == 1, k_scale, 1.0).astype(F32)
    if rope:
        rotated = n < 2
        cos = jnp.where(rotated, cos_ref[...], 1.0) * scale
        sin = jnp.where(rotated, sin_ref[...], 0.0) * scale
    heads_per_piece = MXU_DIM // HEAD_DIM
    for j in range(heads_per_tile // heads_per_piece):
        cs = slice(j * MXU_DIM, (j + 1) * MXU_DIM)
        z = _dot(h_ref[0], w_scr[:, cs])
        if rope:
            for h in range(heads_per_piece):
                zh = z[:, h * HEAD_DIM:(h + 1) * HEAD_DIM]
                r = zh * cos + pltpu.roll(zh, HEAD_DIM // 2, 1) * sin
                o_ref[0, :, j * MXU_DIM + h * HEAD_DIM:j * MXU_DIM + (h + 1) * HEAD_DIM] = r.astype(BF16)
        else:
            o_ref[0, :, cs] = (z * scale).astype(BF16)


def _inproj_ret(h, w_in, rope_tabs, ret_w):
    b, l, d = h.shape
    tm = _tile("inproj_ret_tm", l)
    tn = ret_w
    rope = rope_tabs is not None
    in_specs = [pl.BlockSpec((1, tm, d), lambda n, bi, m: (bi, m, 0))]
    args = [h]
    if rope:
        in_specs += [pl.BlockSpec((tm, HEAD_DIM), lambda n, bi, m: (m, 0))] * 2
        args += list(rope_tabs)
    in_specs.append(pl.BlockSpec((d, tn), lambda n, bi, m: (0, n)))
    args.append(w_in)
    vmem = 2 * tm * d * 2 + 2 * d * tn * 4 + d * tn * 2 + 2 * tm * tn * 2 + 4 * tm * tn * 4 + (6 << 20)
    return pl.pallas_call(
        functools.partial(_inproj_ret_kernel, rope=rope, heads_per_tile=tn // HEAD_DIM, k_scale=HEAD_DIM ** -0.5),
        grid=(4, b, l // tm),
        in_specs=in_specs,
        out_specs=pl.BlockSpec((1, tm, tn), lambda n, bi, m: (bi, m, n)),
        out_shape=jax.ShapeDtypeStruct((b, l, 4 * tn), BF16),
        scratch_shapes=[pltpu.VMEM((d, tn), BF16)],
        compiler_params=_cparams(("arbitrary", "arbitrary", "arbitrary"), vmem),
        name="inproj_ret_rope" if rope else "inproj_ret",
    )(*args)


def _rope_half_padded(x, cos, sin):
    rot = pltpu.roll(x, MLA_DR // 2, 1) + pltpu.roll(x, LANES - MLA_DR // 2, 1)
    return x * cos + rot * sin


def _inproj_mla_kernel(*refs, rope, heads, q_rank, kv_rank, q_scale):
    if rope:
        h_ref, cos_ref, sin_ref, wt_ref, gq_ref, wq_ref, gkv_ref, wkn_ref, wv_ref, q_out, k_out, v_out = refs
    else:
        h_ref, wt_ref, gq_ref, wq_ref, gkv_ref, wkn_ref, wv_ref, q_out, k_out, v_out = refs
    zt = _dot(h_ref[0], wt_ref[...])
    cq = zt[:, :q_rank]
    ckv = zt[:, q_rank:q_rank + kv_rank]
    kr = zt[:, q_rank + kv_rank:]
    cqn = (cq * lax.rsqrt(jnp.mean(cq * cq, axis=-1, keepdims=True) + EPS) * gq_ref[...]).astype(BF16)
    ckvn = (ckv * lax.rsqrt(jnp.mean(ckv * ckv, axis=-1, keepdims=True) + EPS) * gkv_ref[...]).astype(BF16)
    q = _dot(cqn, wq_ref[...])
    kn = _dot(ckvn, wkn_ref[...])
    v_out[0] = _dot(ckvn, wv_ref[...]).astype(BF16)
    if rope:
        cos = cos_ref[...]
        sin = sin_ref[...]
        kr = _rope_half_padded(kr, cos, sin)
    kr = kr.astype(BF16)
    for hd in range(heads):
        base = hd * 2 * HEAD_DIM
        qn = q[:, base:base + HEAD_DIM]
        qr = q[:, base + HEAD_DIM:base + 2 * HEAD_DIM]
        if rope:
            qr = _rope_half_padded(qr, cos, sin)
        q_out[0, :, base:base + HEAD_DIM] = (qn * q_scale).astype(BF16)
        q_out[0, :, base + HEAD_DIM:base + 2 * HEAD_DIM] = (qr * q_scale).astype(BF16)
        k_out[0, :, base:base + HEAD_DIM] = kn[:, hd * HEAD_DIM:(hd + 1) * HEAD_DIM].astype(BF16)
        k_out[0, :, base + HEAD_DIM:base + 2 * HEAD_DIM] = kr


def _inproj_mla(h, w_tail, gq, wq, gkv, wkn, wv, rope_tabs, heads):
    b, l, d = h.shape
    tm = _tile("inproj_mla_tm", l)
    q_rank, kv_rank = gq.shape[1], gkv.shape[1]
    rope = rope_tabs is not None
    full = lambda a: pl.BlockSpec(a.shape, lambda bi, m: (0,) * a.ndim)
    in_specs = [pl.BlockSpec((1, tm, d), lambda bi, m: (bi, m, 0))]
    args = [h]
    if rope:
        in_specs += [pl.BlockSpec((tm, LANES), lambda bi, m: (m, 0))] * 2
        args += list(rope_tabs)
    for a in (w_tail, gq, wq, gkv, wkn, wv):
        in_specs.append(full(a))
        args.append(a)
    qw = heads * 2 * HEAD_DIM
    vw = heads * HEAD_DIM
    out_spec = lambda w: pl.BlockSpec((1, tm, w), lambda bi, m: (bi, m, 0))
    wbytes = sum(a.size * a.dtype.itemsize for a in (w_tail, wq, wkn, wv))
    vmem = 2 * tm * d * 2 + 2 * wbytes + 2 * tm * (2 * qw + vw) * 2 + tm * (8 * qw * 4) + (6 << 20)
    return pl.pallas_call(
        functools.partial(_inproj_mla_kernel, rope=rope, heads=heads, q_rank=q_rank, kv_rank=kv_rank,
                          q_scale=(HEAD_DIM + MLA_DR) ** -0.5 * LOG2_E),
        grid=(b, l // tm),
        in_specs=in_specs,
        out_specs=[out_spec(qw), out_spec(qw), out_spec(vw)],
        out_shape=[jax.ShapeDtypeStruct((b, l, qw), BF16), jax.ShapeDtypeStruct((b, l, qw), BF16),
                   jax.ShapeDtypeStruct((b, l, vw), BF16)],
        compiler_params=_cparams(("parallel", "parallel"), vmem),
        name="inproj_mla_rope" if rope else "inproj_mla",
    )(*args)


def _log_sigmoid(x):
    return jnp.minimum(x, 0.0) - jnp.log1p(jnp.exp(-jnp.abs(x)))


def _retention_kernel(dec_ref, ql_ref, kl_ref, vl_ref, gl_ref, qc_ref, kc_ref, vc_ref, gc_ref, ol_ref, oc_ref,
                      st_scr, *, n_ctx, n_lat, c):
    dk = HEAD_DIM
    hd = pl.program_id(1)
    lf = _log_sigmoid(jnp.full((c, c), dec_ref[0, hd], F32))
    lb = _log_sigmoid(jnp.full((c, c), dec_ref[1, hd], F32))
    ii = lax.broadcasted_iota(jnp.int32, (c, c), 0).astype(F32)
    jj = lax.broadcasted_iota(jnp.int32, (c, c), 1).astype(F32)
    diff = ii - jj
    fwd = diff >= 0
    decay = jnp.where(fwd, jnp.exp(jnp.where(fwd, diff, 0.0) * lf), jnp.exp(jnp.where(fwd, 0.0, -diff) * lb))
    ri = lax.broadcasted_iota(jnp.int32, (c, dk), 0).astype(F32)
    lfr = _log_sigmoid(jnp.full((c, dk), dec_ref[0, hd], F32))
    lbr = _log_sigmoid(jnp.full((c, dk), dec_ref[1, hd], F32))
    q_dec_f = jnp.exp((ri + 1.0) * lfr)
    k_dec_f = jnp.exp((c - 1.0 - ri) * lfr)
    q_dec_b = jnp.exp((c - ri) * lbr)
    k_dec_b = jnp.exp(ri * lbr)
    c_dec_f = jnp.exp(c * _log_sigmoid(jnp.full((dk, dk), dec_ref[0, hd], F32)))
    c_dec_b = jnp.exp(c * _log_sigmoid(jnp.full((dk, dk), dec_ref[1, hd], F32)))

    blocks = [(qc_ref, kc_ref, vc_ref, gc_ref, oc_ref, i) for i in range(n_ctx)]
    blocks += [(ql_ref, kl_ref, vl_ref, gl_ref, ol_ref, i) for i in range(n_lat)]

    def rows(ref, i):
        return ref[0, i * c:(i + 1) * c, :]

    for g, (_, k_ref, v_ref, _, _, i) in enumerate(blocks):
        kf = rows(k_ref, i).astype(F32)
        kk = jnp.concatenate([(kf * k_dec_f).astype(BF16), (kf * k_dec_b).astype(BF16)], axis=1)
        st_scr[g] = _dot_tn(kk, rows(v_ref, i))

    s = jnp.zeros((dk, dk), F32)
    for g in range(n_ctx + n_lat):
        kv = st_scr[g, :dk, :]
        st_scr[g, :dk, :] = s
        s = s * c_dec_f + kv
    s = jnp.zeros((dk, dk), F32)
    for g in list(range(n_ctx - 1, -1, -1)) + list(range(n_ctx + n_lat - 1, n_ctx - 1, -1)):
        kv = st_scr[g, dk:, :]
        st_scr[g, dk:, :] = s
        s = s * c_dec_b + kv

    for g, (q_ref, k_ref, v_ref, g_ref, o_ref, i) in enumerate(blocks):
        q = rows(q_ref, i)
        p = (_dot_nt(q, rows(k_ref, i)) * decay).astype(BF16)
        qf = q.astype(F32)
        qq = jnp.concatenate([(qf * q_dec_f).astype(BF16), (qf * q_dec_b).astype(BF16)], axis=1)
        o = _dot(p, rows(v_ref, i)) + _dot(qq, st_scr[g].astype(BF16))
        o = o * lax.rsqrt(jnp.mean(o * o, axis=-1, keepdims=True) + EPS)
        o_ref[0, i * c:(i + 1) * c, :] = (o * _silu(rows(g_ref, i).astype(F32))).astype(BF16)


def _retention(dec, ret_lat, ret_ctx, heads):
    b, l, _ = ret_lat.shape
    lc = ret_ctx.shape[1]
    c = MXU_DIM if (l % MXU_DIM == 0 and lc % MXU_DIM == 0) else RET_CHUNK
    n_lat, n_ctx = l // c, lc // c
    col = lambda rows, group: pl.BlockSpec((1, rows, HEAD_DIM), lambda bi, h: (bi, 0, group * heads + h))
    in_specs = [pl.BlockSpec(memory_space=pltpu.SMEM)]
    in_specs += [col(l, grp) for grp in range(4)] + [col(lc, grp) for grp in range(4)]
    out_col = lambda rows: pl.BlockSpec((1, rows, HEAD_DIM), lambda bi, h: (bi, 0, h))
    vmem = 2 * 5 * (l + lc) * HEAD_DIM * 2 + (n_lat + n_ctx) * 2 * HEAD_DIM * HEAD_DIM * 4 + (12 << 20)
    return pl.pallas_call(
        functools.partial(_retention_kernel, n_ctx=n_ctx, n_lat=n_lat, c=c),
        grid=(b, heads),
        in_specs=in_specs,
        out_specs=[out_col(l), out_col(lc)],
        out_shape=[jax.ShapeDtypeStruct((b, l, heads * HEAD_DIM), BF16),
                   jax.ShapeDtypeStruct((b, lc, heads * HEAD_DIM), BF16)],
        scratch_shapes=[pltpu.VMEM((n_lat + n_ctx, 2 * HEAD_DIM, HEAD_DIM), F32)],
        compiler_params=_cparams(("parallel", "parallel"), vmem),
        name="retention",
    )(dec, ret_lat, ret_lat, ret_lat, ret_lat, ret_ctx, ret_ctx, ret_ctx, ret_ctx)


def _attn_kernel(*refs, with_lat, tq):
    if with_lat:
        q_ref, kc_ref, vc_ref, kl_ref, vl_ref, o_ref = refs
    else:
        q_ref, kc_ref, vc_ref, o_ref = refs
    for i in range(q_ref.shape[1] // tq):
        q = q_ref[0, i * tq:(i + 1) * tq, :]
        sc = _dot_nt(q, kc_ref[0])
        m = jnp.max(sc, axis=-1, keepdims=True)
        if with_lat:
            sl = _dot_nt(q, kl_ref[0])
            m = jnp.maximum(m, jnp.max(sl, axis=-1, keepdims=True))
        pc = jnp.exp2(sc - m)
        den = jnp.sum(pc, axis=-1, keepdims=True)
        o = _dot(pc.astype(BF16), vc_ref[0])
        if with_lat:
            p_lat = jnp.exp2(sl - m)
            den = den + jnp.sum(p_lat, axis=-1, keepdims=True)
            o = o + _dot(p_lat.astype(BF16), vl_ref[0])
        o_ref[0, i * tq:(i + 1) * tq, :] = (o / den).astype(BF16)


def _attention(q, k_ctx, v_ctx, k_lat, v_lat, heads):
    b, lq, _ = q.shape
    tq = _tile("attn_tq", lq)
    lc = k_ctx.shape[1]
    with_lat = k_lat is not None
    kw = 2 * HEAD_DIM
    head = lambda rows, w: pl.BlockSpec((1, rows, w), lambda bi, h: (bi, 0, h))
    in_specs = [head(lq, kw), head(lc, kw), head(lc, HEAD_DIM)]
    args = [q, k_ctx, v_ctx]
    lk = lc
    if with_lat:
        ll = k_lat.shape[1]
        lk += ll
        in_specs += [head(ll, kw), head(ll, HEAD_DIM)]
        args += [k_lat, v_lat]
    vmem = 2 * (lq * kw + lk * (kw + HEAD_DIM) + lq * HEAD_DIM) * 2 + 8 * tq * lk * 4 + (8 << 20)
    return pl.pallas_call(
        functools.partial(_attn_kernel, with_lat=with_lat, tq=tq),
        grid=(b, heads),
        in_specs=in_specs,
        out_specs=head(lq, HEAD_DIM),
        out_shape=jax.ShapeDtypeStruct((b, lq, heads * HEAD_DIM), BF16),
        compiler_params=_cparams(("parallel", "parallel"), vmem),
        name="mla_attn" if with_lat else "mla_attn_ctx",
    )(*args)


def _mix_out_kernel(a1_ref, a2_ref, w_ref, x_ref, mod_ref, g2_ref, x_out, h_out):
    k1 = a1_ref.shape[2]
    y = _dot(a1_ref[0], w_ref[:k1, :]) + _dot(a2_ref[0], w_ref[k1:, :])
    x1 = x_ref[0] + mod_ref[0, GATE1:GATE1 + 1, :] * y
    x_out[0] = x1
    h_out[0] = _norm_mod(x1, g2_ref[...], mod_ref[0, SHIFT2:SHIFT2 + 1, :],
                         mod_ref[0, SCALE2:SCALE2 + 1, :]).astype(BF16)


def _mix_out(a1, a2, w_bf16, x, mod, mod_row, g2):
    b, l, d = x.shape
    tm = _tile("mix_out_tm", l)
    k1, k2 = a1.shape[2], a2.shape[2]
    row = lambda w: pl.BlockSpec((1, tm, w), lambda bi, m: (bi, m, 0))
    vmem = (k1 + k2) * d * 2 + 2 * tm * (k1 + k2) * 2 + 4 * tm * d * 4 + 2 * tm * d * 2 + 6 * tm * d * 4 + (4 << 20)
    return pl.pallas_call(
        _mix_out_kernel,
        grid=(b, l // tm),
        in_specs=[row(k1), row(k2),
                  pl.BlockSpec((k1 + k2, d), lambda bi, m: (0, 0)),
                  row(d),
                  pl.BlockSpec((1, 6, d), _mod_index(mod_row, 0)),
                  pl.BlockSpec((1, d), lambda bi, m: (0, 0))],
        out_specs=[row(d), row(d)],
        out_shape=[jax.ShapeDtypeStruct((b, l, d), F32), jax.ShapeDtypeStruct((b, l, d), BF16)],
        compiler_params=_cparams(("parallel", "parallel"), vmem),
        name="mix_out",
    )(a1, a2, w_bf16, x, mod, g2)


def _ffn_up_kernel(hm_ref, hp_ref, hn_ref, wa_ref, wg_ref, cwa_ref, cwg_ref, cba_ref, cbg_ref, o_ref,
                   lhs_scr, wa_scr, wg_scr, *, tm, n_m, row_pieces):
    m = pl.program_id(2)

    @pl.when((pl.program_id(1) == 0) & (m == 0))
    def _():
        wa_scr[...] = wa_ref[...].astype(BF16)
        wg_scr[...] = wg_ref[...].astype(BF16)

    zero = jnp.zeros((HALO, lhs_scr.shape[1]), BF16)
    lhs_scr[0:HALO, :] = jnp.where(m > 0, hp_ref[0], zero)
    lhs_scr[HALO:HALO + tm, :] = hm_ref[0]
    lhs_scr[HALO + tm:, :] = jnp.where(m < n_m - 1, hn_ref[0], zero)
    rows = tm + 2 * HALO
    pr = rows // row_pieces

    def conv(w_scr, cw_ref, cb_ref):
        u = jnp.concatenate([_dot(lhs_scr[i * pr:(i + 1) * pr, :], w_scr[...]) for i in range(row_pieces)], axis=0)
        prev = pltpu.roll(u, 1, 0)[HALO:HALO + tm]
        nxt = pltpu.roll(u, rows - 1, 0)[HALO:HALO + tm]
        cw = cw_ref[...]
        return cw[0:1, :] * prev + cw[1:2, :] * u[HALO:HALO + tm] + cw[2:3, :] * nxt + cb_ref[...]

    a = conv(wa_scr, cwa_ref, cba_ref)
    gt = conv(wg_scr, cwg_ref, cbg_ref)
    o_ref[0] = (_silu(gt) * a).astype(BF16)


def _ffn_up(h, w_up, conv_w, conv_b, layer):
    b, l, d = h.shape
    tm = _tile("ffn_up_tm", l)
    tn = TILES["ffn_up_tn"]
    f = w_up.shape[2] // 2
    n_m, n_n = l // tm, f // tn
    hb = tm // HALO
    rows = tm + 2 * HALO
    row_pieces = next(p for p in (3, 2, 1) if rows % (p * HALO) == 0)
    col = lambda rows, off: pl.BlockSpec((None, rows, tn), lambda n, bi, m: (layer, 0, n + off))
    vmem = (2 * tm * d * 2 + rows * d * 2 + 4 * d * tn * 4 + 2 * d * tn * 2
            + 2 * tm * tn * 2 + 8 * rows * tn * 4 + (4 << 20))
    return pl.pallas_call(
        functools.partial(_ffn_up_kernel, tm=tm, n_m=n_m, row_pieces=row_pieces),
        grid=(n_n, b, n_m),
        in_specs=[
            pl.BlockSpec((1, tm, d), lambda n, bi, m: (bi, m, 0)),
            pl.BlockSpec((1, HALO, d), lambda n, bi, m: (bi, jnp.maximum(m * hb - 1, 0), 0)),
            pl.BlockSpec((1, HALO, d), lambda n, bi, m: (bi, jnp.minimum((m + 1) * hb, l // HALO - 1), 0)),
            col(d, 0), col(d, n_n), col(3, 0), col(3, n_n), col(1, 0), col(1, n_n),
        ],
        out_specs=pl.BlockSpec((1, tm, tn), lambda n, bi, m: (bi, m, n)),
        out_shape=jax.ShapeDtypeStruct((b, l, f), BF16),
        scratch_shapes=[pltpu.VMEM((rows, d), BF16), pltpu.VMEM((d, tn), BF16), pltpu.VMEM((d, tn), BF16)],
        compiler_params=_cparams(("arbitrary", "arbitrary", "arbitrary"), vmem),
        name="ffn_up",
    )(h, h, h, w_up, w_up, conv_w, conv_w, conv_b, conv_b)


def _ffn_down_kernel(a_ref, w_ref, x_ref, mod_ref, o_ref):
    y = _dot(a_ref[0], w_ref[...])
    o_ref[0] = x_ref[0] + mod_ref[0, GATE2:GATE2 + 1, :] * y


def _ffn_down(act, w_bf16, x, mod, mod_row, layer):
    b, l, d = x.shape
    tm = _tile("ffn_down_tm", l)
    tn = _tile("ffn_down_tn", d)
    f = act.shape[2]
    vmem = 2 * tm * f * 2 + 2 * f * tn * 2 + 5 * tm * tn * 4 + (6 << 20)
    return pl.pallas_call(
        _ffn_down_kernel,
        grid=(d // tn, b, l // tm),
        in_specs=[
            pl.BlockSpec((1, tm, f), lambda n, bi, m: (bi, m, 0)),
            pl.BlockSpec((None, f, tn), lambda n, bi, m: (layer, 0, n)),
            pl.BlockSpec((1, tm, tn), lambda n, bi, m: (bi, m, n)),
            pl.BlockSpec((1, 6, tn), _mod_index(mod_row, 1, col_axis=0)),
        ],
        out_specs=pl.BlockSpec((1, tm, tn), lambda n, bi, m: (bi, m, n)),
        out_shape=jax.ShapeDtypeStruct((b, l, d), F32),
        compiler_params=_cparams(("parallel", "parallel", "parallel"), vmem),
        name="ffn_down",
    )(act, w_bf16, x, mod)


def _pool_kernel(xm_ref, xp_ref, xn_ref, mod_ref, g_ref, w_ref, ps_ref, g2_ref, x_out, h_out, x1_scr,
                 *, tm, n_m, seq_len):
    m = pl.program_id(1)
    g = g_ref[...]
    shift = mod_ref[0, SHIFT1:SHIFT1 + 1, :]
    scale = mod_ref[0, SCALE1:SCALE1 + 1, :]
    xm = xm_ref[0]
    hp = _norm_mod(xp_ref[0], g, shift, scale) * (m > 0).astype(F32)
    hn = _norm_mod(xn_ref[0], g, shift, scale) * (m < n_m - 1).astype(F32)
    hh = jnp.concatenate([hp, _norm_mod(xm, g, shift, scale), hn], axis=0)
    rows = tm + 2 * HALO
    t = (m * tm + lax.broadcasted_iota(jnp.int32, (tm, 1), 0)).astype(F32)
    gw = hh.shape[1] // len(POOL_WINDOWS)
    gate = mod_ref[0, GATE1:GATE1 + 1, :]
    for gi, w in enumerate(POOL_WINDOWS):
        sl = slice(gi * gw, (gi + 1) * gw)
        hg = hh[:, sl]
        acc = hg
        span = 1
        while span < w:
            acc = acc + pltpu.roll(acc, span, 0)
            span *= 2
        ahead = w // 2 - 1
        if ahead:
            acc = pltpu.roll(acc, rows - ahead, 0)
        win = acc[HALO:HALO + tm]
        cnt = jnp.minimum(t + (w // 2 - 1), seq_len - 1.0) - jnp.maximum(t - w // 2, 0.0) + 1.0
        p = (win / cnt - hg[HALO:HALO + tm]).astype(BF16)
        y = _dot(p, w_ref[gi].astype(BF16)) * ps_ref[:, sl]
        x1_scr[:, sl] = xm[:, sl] + gate[:, sl] * y
    x1 = x1_scr[...]
    x_out[0] = x1
    h_out[0] = _norm_mod(x1, g2_ref[...], mod_ref[0, SHIFT2:SHIFT2 + 1, :],
                         mod_ref[0, SCALE2:SCALE2 + 1, :]).astype(BF16)


def _pool_mix(x, mod, mod_row, g, pool_w, pool_scale, g2):
    b, l, d = x.shape
    tm = _tile("pool_tm", l)
    n_m = l // tm
    hb = tm // HALO
    ng, gw, _ = pool_w.shape
    row = pl.BlockSpec((1, tm, d), lambda bi, m: (bi, m, 0))
    vec = pl.BlockSpec((1, d), lambda bi, m: (0, 0))
    vmem = 7 * tm * d * 4 + 2 * ng * gw * gw * 4 + 8 * (tm + 2 * HALO) * d * 4 + (6 << 20)
    return pl.pallas_call(
        functools.partial(_pool_kernel, tm=tm, n_m=n_m, seq_len=l),
        grid=(b, n_m),
        in_specs=[
            row,
            pl.BlockSpec((1, HALO, d), lambda bi, m: (bi, jnp.maximum(m * hb - 1, 0), 0)),
            pl.BlockSpec((1, HALO, d), lambda bi, m: (bi, jnp.minimum((m + 1) * hb, l // HALO - 1), 0)),
            pl.BlockSpec((1, 6, d), _mod_index(mod_row, 0)),
            vec,
            pl.BlockSpec((ng, gw, gw), lambda bi, m: (0, 0, 0)),
            vec, vec,
        ],
        out_specs=[row, row],
        out_shape=[jax.ShapeDtypeStruct((b, l, d), F32), jax.ShapeDtypeStruct((b, l, d), BF16)],
        scratch_shapes=[pltpu.VMEM((tm, d), F32)],
        compiler_params=_cparams(("parallel", "parallel"), vmem),
        name="pool_mix",
    )(x, x, x, mod, g, pool_w, pool_scale, g2)


def _final_norm_kernel(x_ref, g_ref, o_ref):
    x = x_ref[0]
    o_ref[0] = x * lax.rsqrt(jnp.mean(x * x, axis=-1, keepdims=True) + EPS) * g_ref[...]


def _final_norm(x, g):
    b, l, d = x.shape
    tm = _tile("final_tm", l)
    return pl.pallas_call(
        _final_norm_kernel,
        grid=(b, l // tm),
        in_specs=[pl.BlockSpec((1, tm, d), lambda bi, m: (bi, m, 0)), pl.BlockSpec((1, d), lambda bi, m: (0, 0))],
        out_specs=pl.BlockSpec((1, tm, d), lambda bi, m: (bi, m, 0)),
        out_shape=jax.ShapeDtypeStruct((b, l, d), F32),
        compiler_params=_cparams(("parallel", "parallel"), 6 * tm * d * 4 + (4 << 20)),
        name="final_norm",
    )(x, g)


def _axial_angles(rows, dim):
    row = jnp.broadcast_to(jnp.arange(rows, dtype=F32)[:, None], (rows, GRID_W)).reshape(-1)
    col = jnp.broadcast_to(jnp.arange(GRID_W, dtype=F32)[None, :], (rows, GRID_W)).reshape(-1)
    n_freq = dim // 4
    inv = ROPE_BASE ** (-jnp.arange(n_freq, dtype=F32) / n_freq)
    return jnp.concatenate([row[:, None] * inv, col[:, None] * inv], axis=-1)


def _rope_tables(rows, dim):
    ang = _axial_angles(rows, dim)
    cos, sin = jnp.cos(ang), jnp.sin(ang)
    cos_t = jnp.concatenate([cos, cos], axis=-1)
    sin_t = jnp.concatenate([-sin, sin], axis=-1)
    pad = LANES - dim
    if pad:
        cos_t = jnp.pad(cos_t, ((0, 0), (0, pad)))
        sin_t = jnp.pad(sin_t, ((0, 0), (0, pad)))
    return cos_t, sin_t


def _mla_weights(w_in, g_q, w_uq, g_kv, w_ukv, heads, ret_w):
    q_rank, kv_rank = g_q.shape[0], g_kv.shape[0]
    w_tail = jnp.pad(w_in[:, 4 * ret_w:], ((0, 0), (0, LANES - MLA_DR))).astype(BF16)
    wq = w_uq.reshape(q_rank, heads, HEAD_DIM + MLA_DR)
    wq = jnp.pad(wq, ((0, 0), (0, 0), (0, HEAD_DIM - MLA_DR))).reshape(q_rank, heads * 2 * HEAD_DIM).astype(BF16)
    wkv = w_ukv.reshape(kv_rank, heads, 2 * HEAD_DIM)
    wkn = wkv[:, :, :HEAD_DIM].reshape(kv_rank, ret_w).astype(BF16)
    wv = wkv[:, :, HEAD_DIM:].reshape(kv_rank, ret_w).astype(BF16)
    return w_tail, g_q[None, :], wq, g_kv[None, :], wkn, wv


def _conv_ffn(h, x, mod, mod_row, w_up, conv_w, conv_b, w_down_bf16, layer):
    act = _ffn_up(h, w_up, conv_w, conv_b[:, None, :], layer)
    return _ffn_down(act, w_down_bf16, x, mod, mod_row, layer)


def kernel(x, c, ctx, c_ctx, ada_w, ada_b, norm1_g, norm2_g, ffn_w_up, ffn_conv_w, ffn_conv_b, ffn_w_down, mix_w_in, mla_q_norm_g, mla_w_uq, mla_kv_norm_g, mla_w_ukv, ret_decay_f, ret_decay_b, mix_w_out, pool_w, pool_scale, final_g):
    b, l, d = x.shape
    lc = ctx.shape[1]
    depth = ada_w.shape[0]
    heads = ret_decay_f.shape[1]
    ret_w = heads * HEAD_DIM
    rows = l // GRID_W
    rope_ret = _rope_tables(rows, HEAD_DIM)
    rope_mla = _rope_tables(rows, MLA_DR)
    w_down_bf16 = ffn_w_down.astype(BF16)

    ctx_row = b
    cc = jnp.concatenate([c, c_ctx[None, :], jnp.zeros((-(b + 1) % 8, d), F32)], axis=0)
    mods = _ada(cc, ada_w, ada_b).reshape(depth, cc.shape[0], 6, d)

    x_lat, x_ctx = x, ctx
    for layer in range(depth):
        j = layer // 2
        with_ctx = layer < depth - 1
        mod = mods[layer]
        g1 = norm1_g[layer][None, :]
        g2 = norm2_g[layer][None, :]
        h_ctx = None
        if layer % 2 == 0:
            w_in = mix_w_in[j]
            mla_w = _mla_weights(w_in, mla_q_norm_g[j], mla_w_uq[j], mla_kv_norm_g[j], mla_w_ukv[j], heads, ret_w)
            w_out_bf16 = mix_w_out[j].astype(BF16)
            dec = jnp.stack([ret_decay_f[j], ret_decay_b[j]]).astype(F32)

            h1_lat = _prenorm(x_lat, mod, None, g1)
            h1_ctx = _prenorm(x_ctx, mod, ctx_row, g1).reshape(1, b * lc, d)
            ret_lat = _inproj_ret(h1_lat, w_in, rope_ret, ret_w)
            q_lat, k_lat, v_lat = _inproj_mla(h1_lat, *mla_w, rope_mla, heads)
            ret_ctx = _inproj_ret(h1_ctx, w_in, None, ret_w).reshape(b, lc, 4 * ret_w)
            q_ctx, k_ctx, v_ctx = [a.reshape(b, lc, -1) for a in _inproj_mla(h1_ctx, *mla_w, None, heads)]

            o_ret_lat, o_ret_ctx = _retention(dec, ret_lat, ret_ctx, heads)
            o_mla_lat = _attention(q_lat, k_ctx, v_ctx, k_lat, v_lat, heads)
            x_lat, h_lat = _mix_out(o_ret_lat, o_mla_lat, w_out_bf16, x_lat, mod, None, g2)
            if with_ctx:
                o_mla_ctx = _attention(q_ctx, k_ctx, v_ctx, None, None, heads)
                x_ctx, h_ctx = _mix_out(o_ret_ctx, o_mla_ctx, w_out_bf16, x_ctx, mod, ctx_row, g2)
        else:
            ps = pool_scale[j][None, :]
            if with_ctx:
                x_ctx, h_ctx = _pool_mix(x_ctx, mod, ctx_row, g1, pool_w[j], ps, g2)
            x_lat, h_lat = _pool_mix(x_lat, mod, None, g1, pool_w[j], ps, g2)
        x_lat = _conv_ffn(h_lat, x_lat, mod, None, ffn_w_up, ffn_conv_w, ffn_conv_b, w_down_bf16, layer)
        if with_ctx:
            x_ctx = _conv_ffn(h_ctx, x_ctx, mod, ctx_row, ffn_w_up, ffn_conv_w, ffn_conv_b, w_down_bf16, layer)
    return _final_norm(x_lat, final_g[None, :])
```

```python
import functools

import jax
import jax.numpy as jnp
from jax import lax
from jax.experimental import pallas as pl
from jax.experimental.pallas import tpu as pltpu

F32 = jnp.float32
BF16 = jnp.bfloat16

GRID_W = 64
HEAD_DIM = 128
RET_CHUNK = 128
MLA_DR = 64
POOL_WINDOWS = (2, 4, 8, 16)
ROPE_BASE = 10000.0
EPS = 1e-6
LOG2_E = 1.4426950408889634

LANES = 128
MXU_DIM = 256
HALO = 16
V7X_VMEM_BYTES = 64 * 1024 * 1024
VMEM_CAP = V7X_VMEM_BYTES - 8 * 1024 * 1024

SHIFT1, SCALE1, GATE1, SHIFT2, SCALE2, GATE2 = range(6)

TILES = dict(
    inproj_ret_tm=1024,
    inproj_mla_tm=512,
    attn_tq=512,
    mix_out_tm=512,
    ffn_up_tm=1024, ffn_up_tn=512,
    ffn_down_tm=512, ffn_down_tn=1024,
    pool_tm=512,
    final_tm=512,
    ada_tn=1024,
)


def _tile(name, extent):
    tile = min(extent, TILES[name])
    assert extent % tile == 0, f"{name}: extent {extent} is not a multiple of its tile {tile}"
    return tile


def _cparams(sem, vmem_bytes, flags=None):
    return pltpu.CompilerParams(dimension_semantics=sem, vmem_limit_bytes=min(int(vmem_bytes), VMEM_CAP),
                                flags=flags)


def _silu(x):
    return x / (1.0 + jnp.exp(-x))


def _norm_mod(x, g, shift, scale):
    y = x * lax.rsqrt(jnp.mean(x * x, axis=-1, keepdims=True) + EPS) * g
    return y * (1.0 + scale) + shift


def _dot(a, b):
    return jnp.dot(a, b, preferred_element_type=F32)


def _dot_nt(a, b):
    return lax.dot_general(a, b, (((1,), (1,)), ((), ())), preferred_element_type=F32)


def _dot_tn(a, b):
    return lax.dot_general(a, b, (((0,), (0,)), ((), ())), preferred_element_type=F32)


def _mod_index(mod_row, batch_axis, col_axis=None):
    def index(*ids):
        row = ids[batch_axis] if mod_row is None else mod_row
        return (row, 0, 0 if col_axis is None else ids[col_axis])
    return index


def _ada_kernel(c_ref, w_ref, b_ref, o_ref):
    s = _silu(c_ref[...]).astype(BF16)
    o_ref[0] = _dot(s, w_ref[0].astype(BF16)) + b_ref[0]


def _ada(cc, ada_w, ada_b):
    depth, d, n6 = ada_w.shape
    rows = cc.shape[0]
    tn = _tile("ada_tn", n6)
    return pl.pallas_call(
        _ada_kernel,
        grid=(depth, n6 // tn),
        in_specs=[
            pl.BlockSpec((rows, d), lambda l, n: (0, 0)),
            pl.BlockSpec((1, d, tn), lambda l, n: (l, 0, n)),
            pl.BlockSpec((1, 1, tn), lambda l, n: (l, 0, n)),
        ],
        out_specs=pl.BlockSpec((1, rows, tn), lambda l, n: (l, 0, n)),
        out_shape=jax.ShapeDtypeStruct((depth, rows, n6), F32),
        compiler_params=_cparams(("parallel", "parallel"), 2 * d * tn * 4 + d * tn * 2 + (8 << 20)),
        name="ada_mod",
    )(cc, ada_w, ada_b.reshape(depth, 1, n6))


def _inproj_ret_kernel(*refs, rope, cast_extra, heads_per_tile, k_scale):
    refs = list(refs)
    h_ref = refs.pop(0)
    if rope:
        cos_ref, sin_ref = refs.pop(0), refs.pop(0)
    w_ref = refs.pop(0)
    if cast_extra:
        extra_ref = refs.pop(0)
    o_ref = refs.pop(0)
    if cast_extra:
        extra_out = refs.pop(0)
    (w_scr,) = refs
    n = pl.program_id(0)

    @pl.when((pl.program_id(1) == 0) & (pl.program_id(2) == 0))
    def _():
        w_scr[...] = w_ref[...].astype(BF16)
        if cast_extra:
            extra_out[...] = extra_ref[...].astype(BF16)

    scale = jnp.where(n == 1, k_scale, 1.0).astype(F32)
    if rope:
        rotated = n < 2
        cos = jnp.where(rotated, cos_ref[...], 1.0) * scale
        sin = jnp.where(rotated, sin_ref[...], 0.0) * scale
    heads_per_piece = MXU_DIM // HEAD_DIM
    for j in range(heads_per_tile // heads_per_piece):
        cs = slice(j * MXU_DIM, (j + 1) * MXU_DIM)
        z = _dot_nt(h_ref[0], w_scr[cs, :])
        if rope:
            for h in range(heads_per_piece):
                zh = z[:, h * HEAD_DIM:(h + 1) * HEAD_DIM]
                r = zh * cos + pltpu.roll(zh, HEAD_DIM // 2, 1) * sin
                o_ref[0, :, j * MXU_DIM + h * HEAD_DIM:j * MXU_DIM + (h + 1) * HEAD_DIM] = r.astype(BF16)
        else:
            o_ref[0, :, cs] = (z * scale).astype(BF16)


def _inproj_ret(h, w_in_t, rope_tabs, ret_w, cast_extra=None):
    b, l, d = h.shape
    tm = _tile("inproj_ret_tm", l)
    tn = ret_w
    n_tiles = 4
    rope = rope_tabs is not None
    in_specs = [pl.BlockSpec((1, tm, d), lambda n, bi, m: (bi, m, 0))]
    args = [h]
    if rope:
        in_specs += [pl.BlockSpec((tm, HEAD_DIM), lambda n, bi, m: (m, 0))] * 2
        args += list(rope_tabs)
    in_specs.append(pl.BlockSpec((tn, d), lambda n, bi, m: (n, 0)))
    args.append(w_in_t)
    out_specs = [pl.BlockSpec((1, tm, tn), lambda n, bi, m: (bi, m, n))]
    out_shape = [jax.ShapeDtypeStruct((b, l, n_tiles * tn), BF16)]
    vmem = 2 * tm * d * 2 + 2 * d * tn * 4 + d * tn * 2 + 2 * tm * tn * 2 + 4 * tm * tn * 4 + (6 << 20)
    if cast_extra is not None:
        er, ec = cast_extra.shape
        extra_spec = pl.BlockSpec((er // n_tiles, ec), lambda n, bi, m: (n, 0))
        in_specs.append(extra_spec)
        args.append(cast_extra)
        out_specs.append(extra_spec)
        out_shape.append(jax.ShapeDtypeStruct((er, ec), BF16))
        vmem += 2 * (er // n_tiles) * ec * 6
    outs = pl.pallas_call(
        functools.partial(_inproj_ret_kernel, rope=rope, cast_extra=cast_extra is not None,
                          heads_per_tile=tn // HEAD_DIM, k_scale=HEAD_DIM ** -0.5),
        grid=(n_tiles, b, l // tm),
        in_specs=in_specs,
        out_specs=out_specs,
        out_shape=out_shape,
        scratch_shapes=[pltpu.VMEM((tn, d), BF16)],
        compiler_params=_cparams(("arbitrary", "arbitrary", "arbitrary"), vmem),
        name="inproj_ret_rope" if rope else "inproj_ret",
    )(*args)
    return outs if cast_extra is not None else outs[0]


def _rope_half_padded(x, cos, sin):
    rot = pltpu.roll(x, MLA_DR // 2, 1) + pltpu.roll(x, LANES - MLA_DR // 2, 1)
    return x * cos + rot * sin


def _inproj_mla_kernel(*refs, rope, heads, q_rank, kv_rank, q_scale):
    refs = list(refs)
    x_ref, mod_ref, g_ref = refs[:3]
    del refs[:3]
    if rope:
        cos_ref, sin_ref = refs[:2]
        del refs[:2]
    (wcq_ref, wckv_ref, wkr_ref, gq_ref, wq_ref, gkv_ref, wkn_ref, wv_ref,
     h_out, q_out, k_out, v_out, wt_scr) = refs

    @pl.when((pl.program_id(0) == 0) & (pl.program_id(1) == 0))
    def _():
        wt_scr[0:q_rank, :] = wcq_ref[...].astype(BF16)
        wt_scr[q_rank:q_rank + kv_rank, :] = wckv_ref[...].astype(BF16)
        wt_scr[q_rank + kv_rank:q_rank + kv_rank + MLA_DR, :] = wkr_ref[...].astype(BF16)
        wt_scr[q_rank + kv_rank + MLA_DR:, :] = jnp.zeros((LANES - MLA_DR, wt_scr.shape[1]), BF16)

    h = _norm_mod(x_ref[0], g_ref[...], mod_ref[0, SHIFT1:SHIFT1 + 1, :], mod_ref[0, SCALE1:SCALE1 + 1, :])
    h = h.astype(BF16)
    h_out[0] = h
    zt = _dot_nt(h, wt_scr[...])
    cq = zt[:, :q_rank]
    ckv = zt[:, q_rank:q_rank + kv_rank]
    kr = zt[:, q_rank + kv_rank:]
    cqn = (cq * lax.rsqrt(jnp.mean(cq * cq, axis=-1, keepdims=True) + EPS) * gq_ref[...]).astype(BF16)
    ckvn = (ckv * lax.rsqrt(jnp.mean(ckv * ckv, axis=-1, keepdims=True) + EPS) * gkv_ref[...]).astype(BF16)
    q = _dot(cqn, wq_ref[...])
    kn = _dot(ckvn, wkn_ref[...])
    v_out[0] = _dot(ckvn, wv_ref[...]).astype(BF16)
    if rope:
        cos = cos_ref[...]
        sin = sin_ref[...]
        kr = _rope_half_padded(kr, cos, sin)
    kr = kr.astype(BF16)
    for hd in range(heads):
        base = hd * 2 * HEAD_DIM
        qn = q[:, base:base + HEAD_DIM]
        qr = q[:, base + HEAD_DIM:base + 2 * HEAD_DIM]
        if rope:
            qr = _rope_half_padded(qr, cos, sin)
        q_out[0, :, base:base + HEAD_DIM] = (qn * q_scale).astype(BF16)
        q_out[0, :, base + HEAD_DIM:base + 2 * HEAD_DIM] = (qr * q_scale).astype(BF16)
        k_out[0, :, base:base + HEAD_DIM] = kn[:, hd * HEAD_DIM:(hd + 1) * HEAD_DIM].astype(BF16)
        k_out[0, :, base + HEAD_DIM:base + 2 * HEAD_DIM] = kr


def _inproj_mla(x, mod, mod_row, g, w_in_t, tail_row, gq, wq, gkv, wkn, wv, rope_tabs, heads):
    b, l, d = x.shape
    tm = _tile("inproj_mla_tm", l)
    q_rank, kv_rank = gq.shape[1], gkv.shape[1]
    rope = rope_tabs is not None
    full = lambda a: pl.BlockSpec(a.shape, lambda bi, m: (0,) * a.ndim)
    in_specs = [pl.BlockSpec((1, tm, d), lambda bi, m: (bi, m, 0)),
                pl.BlockSpec((1, 6, d), _mod_index(mod_row, 0)), full(g)]
    args = [x, mod, g]
    if rope:
        in_specs += [pl.BlockSpec((tm, LANES), lambda bi, m: (m, 0))] * 2
        args += list(rope_tabs)
    for start, rows in ((tail_row, q_rank), (tail_row + q_rank, kv_rank), (tail_row + q_rank + kv_rank, MLA_DR)):
        assert start % rows == 0, (start, rows)
        in_specs.append(pl.BlockSpec((rows, d), functools.partial(lambda blk, bi, m: (blk, 0), start // rows)))
        args.append(w_in_t)
    for a in (gq, wq, gkv, wkn, wv):
        in_specs.append(full(a))
        args.append(a)
    qw = heads * 2 * HEAD_DIM
    vw = heads * HEAD_DIM
    tail = q_rank + kv_rank + LANES
    out_spec = lambda w: pl.BlockSpec((1, tm, w), lambda bi, m: (bi, m, 0))
    wbytes = sum(a.size * a.dtype.itemsize for a in (wq, wkn, wv))
    vmem = (2 * tm * d * 4 + 2 * tail * d * 4 + tail * d * 2 + 2 * wbytes + 2 * tm * (d + 2 * qw + vw) * 2
            + tm * (3 * d * 4 + 6 * qw * 4) + (4 << 20))
    return pl.pallas_call(
        functools.partial(_inproj_mla_kernel, rope=rope, heads=heads, q_rank=q_rank, kv_rank=kv_rank,
                          q_scale=(HEAD_DIM + MLA_DR) ** -0.5 * LOG2_E),
        grid=(b, l // tm),
        in_specs=in_specs,
        out_specs=[out_spec(d), out_spec(qw), out_spec(qw), out_spec(vw)],
        out_shape=[jax.ShapeDtypeStruct((b, l, d), BF16), jax.ShapeDtypeStruct((b, l, qw), BF16),
                   jax.ShapeDtypeStruct((b, l, qw), BF16), jax.ShapeDtypeStruct((b, l, vw), BF16)],
        scratch_shapes=[pltpu.VMEM((tail, d), BF16)],
        compiler_params=_cparams(("arbitrary", "arbitrary"), vmem),
        name="inproj_mla_rope" if rope else "inproj_mla",
    )(*args)


def _log_sigmoid(x):
    return jnp.minimum(x, 0.0) - jnp.log1p(jnp.exp(-jnp.abs(x)))


def _retention_kernel(dec_ref, ql_ref, kl_ref, vl_ref, gl_ref, qc_ref, kc_ref, vc_ref, gc_ref, ol_ref, oc_ref,
                      st_scr, *, n_ctx, n_lat, c):
    dk = HEAD_DIM
    hd = pl.program_id(1)
    lf = _log_sigmoid(jnp.full((c, c), dec_ref[0, hd], F32))
    lb = _log_sigmoid(jnp.full((c, c), dec_ref[1, hd], F32))
    ii = lax.broadcasted_iota(jnp.int32, (c, c), 0).astype(F32)
    jj = lax.broadcasted_iota(jnp.int32, (c, c), 1).astype(F32)
    diff = ii - jj
    fwd = diff >= 0
    decay = jnp.where(fwd, jnp.exp(jnp.where(fwd, diff, 0.0) * lf), jnp.exp(jnp.where(fwd, 0.0, -diff) * lb))
    ri = lax.broadcasted_iota(jnp.int32, (c, dk), 0).astype(F32)
    lfr = _log_sigmoid(jnp.full((c, dk), dec_ref[0, hd], F32))
    lbr = _log_sigmoid(jnp.full((c, dk), dec_ref[1, hd], F32))
    q_dec_f = jnp.exp((ri + 1.0) * lfr)
    k_dec_f = jnp.exp((c - 1.0 - ri) * lfr)
    q_dec_b = jnp.exp((c - ri) * lbr)
    k_dec_b = jnp.exp(ri * lbr)
    c_dec_f = jnp.exp(c * _log_sigmoid(jnp.full((dk, dk), dec_ref[0, hd], F32)))
    c_dec_b = jnp.exp(c * _log_sigmoid(jnp.full((dk, dk), dec_ref[1, hd], F32)))

    blocks = [(qc_ref, kc_ref, vc_ref, gc_ref, oc_ref, i) for i in range(n_ctx)]
    blocks += [(ql_ref, kl_ref, vl_ref, gl_ref, ol_ref, i) for i in range(n_lat)]

    def rows(ref, i):
        return ref[0, i * c:(i + 1) * c, :]

    for g, (_, k_ref, v_ref, _, _, i) in enumerate(blocks):
        kf = rows(k_ref, i).astype(F32)
        kk = jnp.concatenate([(kf * k_dec_f).astype(BF16), (kf * k_dec_b).astype(BF16)], axis=1)
        st_scr[g] = _dot_tn(kk, rows(v_ref, i))

    s = jnp.zeros((dk, dk), F32)
    for g in range(n_ctx + n_lat):
        kv = st_scr[g, :dk, :]
        st_scr[g, :dk, :] = s
        s = s * c_dec_f + kv
    s = jnp.zeros((dk, dk), F32)
    for g in list(range(n_ctx - 1, -1, -1)) + list(range(n_ctx + n_lat - 1, n_ctx - 1, -1)):
        kv = st_scr[g, dk:, :]
        st_scr[g, dk:, :] = s
        s = s * c_dec_b + kv

    for g, (q_ref, k_ref, v_ref, g_ref, o_ref, i) in enumerate(blocks):
        q = rows(q_ref, i)
        p = (_dot_nt(q, rows(k_ref, i)) * decay).astype(BF16)
        qf = q.astype(F32)
        qq = jnp.concatenate([(qf * q_dec_f).astype(BF16), (qf * q_dec_b).astype(BF16)], axis=1)
        o = _dot(p, rows(v_ref, i)) + _dot(qq, st_scr[g].astype(BF16))
        o = o * lax.rsqrt(jnp.mean(o * o, axis=-1, keepdims=True) + EPS)
        o_ref[0, i * c:(i + 1) * c, :] = (o * _silu(rows(g_ref, i).astype(F32))).astype(BF16)


def _retention(dec, ret_lat, ret_ctx, heads):
    b, l, _ = ret_lat.shape
    lc = ret_ctx.shape[1]
    c = MXU_DIM if (l % MXU_DIM == 0 and lc % MXU_DIM == 0) else RET_CHUNK
    n_lat, n_ctx = l // c, lc // c
    col = lambda rows, group: pl.BlockSpec((1, rows, HEAD_DIM), lambda bi, h: (bi, 0, group * heads + h))
    in_specs = [pl.BlockSpec(memory_space=pltpu.SMEM)]
    in_specs += [col(l, grp) for grp in range(4)] + [col(lc, grp) for grp in range(4)]
    out_col = lambda rows: pl.BlockSpec((1, rows, HEAD_DIM), lambda bi, h: (bi, 0, h))
    vmem = 2 * 5 * (l + lc) * HEAD_DIM * 2 + (n_lat + n_ctx) * 2 * HEAD_DIM * HEAD_DIM * 4 + (12 << 20)
    return pl.pallas_call(
        functools.partial(_retention_kernel, n_ctx=n_ctx, n_lat=n_lat, c=c),
        grid=(b, heads),
        in_specs=in_specs,
        out_specs=[out_col(l), out_col(lc)],
        out_shape=[jax.ShapeDtypeStruct((b, l, heads * HEAD_DIM), BF16),
                   jax.ShapeDtypeStruct((b, lc, heads * HEAD_DIM), BF16)],
        scratch_shapes=[pltpu.VMEM((n_lat + n_ctx, 2 * HEAD_DIM, HEAD_DIM), F32)],
        compiler_params=_cparams(("parallel", "parallel"), vmem),
        name="retention",
    )(dec, ret_lat, ret_lat, ret_lat, ret_lat, ret_ctx, ret_ctx, ret_ctx, ret_ctx)


def _attn_kernel(*refs, with_lat, tq):
    if with_lat:
        q_ref, kc_ref, vc_ref, kl_ref, vl_ref, o_ref = refs
    else:
        q_ref, kc_ref, vc_ref, o_ref = refs
    for i in range(q_ref.shape[1] // tq):
        q = q_ref[0, i * tq:(i + 1) * tq, :]
        sc = _dot_nt(q, kc_ref[0])
        m = jnp.max(sc, axis=-1, keepdims=True)
        if with_lat:
            sl = _dot_nt(q, kl_ref[0])
            m = jnp.maximum(m, jnp.max(sl, axis=-1, keepdims=True))
        pc = jnp.exp2(sc - m)
        den = jnp.sum(pc, axis=-1, keepdims=True)
        o = _dot(pc.astype(BF16), vc_ref[0])
        if with_lat:
            p_lat = jnp.exp2(sl - m)
            den = den + jnp.sum(p_lat, axis=-1, keepdims=True)
            o = o + _dot(p_lat.astype(BF16), vl_ref[0])
        o_ref[0, i * tq:(i + 1) * tq, :] = (o / den).astype(BF16)


def _attention(q, k_ctx, v_ctx, k_lat, v_lat, heads):
    b, lq, _ = q.shape
    tq = _tile("attn_tq", lq)
    lc = k_ctx.shape[1]
    with_lat = k_lat is not None
    kw = 2 * HEAD_DIM
    head = lambda rows, w: pl.BlockSpec((1, rows, w), lambda bi, h: (bi, 0, h))
    in_specs = [head(lq, kw), head(lc, kw), head(lc, HEAD_DIM)]
    args = [q, k_ctx, v_ctx]
    lk = lc
    if with_lat:
        ll = k_lat.shape[1]
        lk += ll
        in_specs += [head(ll, kw), head(ll, HEAD_DIM)]
        args += [k_lat, v_lat]
    vmem = 2 * (lq * kw + lk * (kw + HEAD_DIM) + lq * HEAD_DIM) * 2 + 8 * tq * lk * 4 + (8 << 20)
    return pl.pallas_call(
        functools.partial(_attn_kernel, with_lat=with_lat, tq=tq),
        grid=(b, heads),
        in_specs=in_specs,
        out_specs=head(lq, HEAD_DIM),
        out_shape=jax.ShapeDtypeStruct((b, lq, heads * HEAD_DIM), BF16),
        compiler_params=_cparams(("parallel", "parallel"), vmem),
        name="mla_attn" if with_lat else "mla_attn_ctx",
    )(*args)


def _mix_out_kernel(a1_ref, a2_ref, w_ref, x_ref, mod_ref, g2_ref, x_out, h_out):
    k1 = a1_ref.shape[2]
    y = _dot(a1_ref[0], w_ref[:k1, :]) + _dot(a2_ref[0], w_ref[k1:, :])
    x1 = x_ref[0] + mod_ref[0, GATE1:GATE1 + 1, :] * y
    x_out[0] = x1
    h_out[0] = _norm_mod(x1, g2_ref[...], mod_ref[0, SHIFT2:SHIFT2 + 1, :],
                         mod_ref[0, SCALE2:SCALE2 + 1, :]).astype(BF16)


def _mix_out(a1, a2, w_bf16, x, mod, mod_row, g2):
    b, l, d = x.shape
    tm = _tile("mix_out_tm", l)
    k1, k2 = a1.shape[2], a2.shape[2]
    row = lambda w: pl.BlockSpec((1, tm, w), lambda bi, m: (bi, m, 0))
    vmem = (k1 + k2) * d * 2 + 2 * tm * (k1 + k2) * 2 + 4 * tm * d * 4 + 2 * tm * d * 2 + 6 * tm * d * 4 + (4 << 20)
    return pl.pallas_call(
        _mix_out_kernel,
        grid=(b, l // tm),
        in_specs=[row(k1), row(k2),
                  pl.BlockSpec((k1 + k2, d), lambda bi, m: (0, 0)),
                  row(d),
                  pl.BlockSpec((1, 6, d), _mod_index(mod_row, 0)),
                  pl.BlockSpec((1, d), lambda bi, m: (0, 0))],
        out_specs=[row(d), row(d)],
        out_shape=[jax.ShapeDtypeStruct((b, l, d), F32), jax.ShapeDtypeStruct((b, l, d), BF16)],
        compiler_params=_cparams(("parallel", "parallel"), vmem),
        name="mix_out",
    )(a1, a2, w_bf16, x, mod, g2)


def _ffn_up_kernel(hm_ref, hp_ref, hn_ref, wa_ref, wg_ref, cwa_ref, cwg_ref, cba_ref, cbg_ref, wd_ref,
                   o_ref, wd_out, lhs_scr, wa_scr, wg_scr, *, tm, n_m, row_pieces):
    m = pl.program_id(2)

    @pl.when((pl.program_id(1) == 0) & (m == 0))
    def _():
        wa_scr[...] = wa_ref[...].astype(BF16)
        wg_scr[...] = wg_ref[...].astype(BF16)
        wd_out[...] = wd_ref[...].astype(BF16)

    zero = jnp.zeros((HALO, lhs_scr.shape[1]), BF16)
    lhs_scr[0:HALO, :] = jnp.where(m > 0, hp_ref[0], zero)
    lhs_scr[HALO:HALO + tm, :] = hm_ref[0]
    lhs_scr[HALO + tm:, :] = jnp.where(m < n_m - 1, hn_ref[0], zero)
    rows = tm + 2 * HALO
    pr = rows // row_pieces

    def conv(w_scr, cw_ref, cb_ref):
        u = jnp.concatenate([_dot(lhs_scr[i * pr:(i + 1) * pr, :], w_scr[...]) for i in range(row_pieces)], axis=0)
        prev = pltpu.roll(u, 1, 0)[HALO:HALO + tm]
        nxt = pltpu.roll(u, rows - 1, 0)[HALO:HALO + tm]
        cw = cw_ref[...]
        return cw[0:1, :] * prev + cw[1:2, :] * u[HALO:HALO + tm] + cw[2:3, :] * nxt + cb_ref[...]

    a = conv(wa_scr, cwa_ref, cba_ref)
    gt = conv(wg_scr, cwg_ref, cbg_ref)
    o_ref[0] = (_silu(gt) * a).astype(BF16)


def _ffn_up(h, w_up, conv_w, conv_b, w_down, layer):
    b, l, d = h.shape
    d_out = w_down.shape[2]
    tm = _tile("ffn_up_tm", l)
    f = w_up.shape[2] // 2
    tn = _tile("ffn_up_tn", f)
    n_m, n_n = l // tm, f // tn
    hb = tm // HALO
    rows = tm + 2 * HALO
    row_pieces = next(p for p in (3, 2, 1) if rows % (p * HALO) == 0)
    col = lambda rows, off: pl.BlockSpec((None, rows, tn), lambda n, bi, m: (layer, 0, n + off))
    vmem = (2 * tm * d * 2 + rows * d * 2 + 4 * d * tn * 4 + 2 * d * tn * 2 + 2 * tn * d_out * 6
            + 2 * tm * tn * 2 + 8 * rows * tn * 4 + (4 << 20))
    return pl.pallas_call(
        functools.partial(_ffn_up_kernel, tm=tm, n_m=n_m, row_pieces=row_pieces),
        grid=(n_n, b, n_m),
        in_specs=[
            pl.BlockSpec((1, tm, d), lambda n, bi, m: (bi, m, 0)),
            pl.BlockSpec((1, HALO, d), lambda n, bi, m: (bi, jnp.maximum(m * hb - 1, 0), 0)),
            pl.BlockSpec((1, HALO, d), lambda n, bi, m: (bi, jnp.minimum((m + 1) * hb, l // HALO - 1), 0)),
            col(d, 0), col(d, n_n), col(3, 0), col(3, n_n), col(1, 0), col(1, n_n),
            pl.BlockSpec((None, tn, d_out), lambda n, bi, m: (layer, n, 0)),
        ],
        out_specs=[pl.BlockSpec((1, tm, tn), lambda n, bi, m: (bi, m, n)),
                   pl.BlockSpec((tn, d_out), lambda n, bi, m: (n, 0))],
        out_shape=[jax.ShapeDtypeStruct((b, l, f), BF16), jax.ShapeDtypeStruct((f, d_out), BF16)],
        scratch_shapes=[pltpu.VMEM((rows, d), BF16), pltpu.VMEM((d, tn), BF16), pltpu.VMEM((d, tn), BF16)],
        compiler_params=_cparams(("arbitrary", "arbitrary", "arbitrary"), vmem),
        name="ffn_up",
    )(h, h, h, w_up, w_up, conv_w, conv_w, conv_b, conv_b, w_down)


def _ffn_down_kernel(a_ref, w_ref, x_ref, mod_ref, o_ref):
    y = _dot(a_ref[0], w_ref[...])
    o_ref[0] = x_ref[0] + mod_ref[0, GATE2:GATE2 + 1, :] * y


def _ffn_down(act, w_bf16, x, mod, mod_row, n_col_tiles=None):
    b, l, d = x.shape
    tm = _tile("ffn_down_tm", l)
    tn = _tile("ffn_down_tn", d)
    f = act.shape[2]
    if n_col_tiles is None:
        n_col_tiles = d // tn
    vmem = 2 * tm * f * 2 + 2 * f * tn * 2 + 5 * tm * tn * 4 + (6 << 20)
    return pl.pallas_call(
        _ffn_down_kernel,
        grid=(n_col_tiles, b, l // tm),
        in_specs=[
            pl.BlockSpec((1, tm, f), lambda n, bi, m: (bi, m, 0)),
            pl.BlockSpec((f, tn), lambda n, bi, m: (0, n)),
            pl.BlockSpec((1, tm, tn), lambda n, bi, m: (bi, m, n)),
            pl.BlockSpec((1, 6, tn), _mod_index(mod_row, 1, col_axis=0)),
        ],
        out_specs=pl.BlockSpec((1, tm, tn), lambda n, bi, m: (bi, m, n)),
        out_shape=jax.ShapeDtypeStruct((b, l, n_col_tiles * tn), F32),
        compiler_params=_cparams(("parallel", "parallel", "parallel"), vmem),
        name="ffn_down",
    )(act, w_bf16, x, mod)


def _ffn_down_norm_kernel(a_ref, w_ref, x_ref, left_ref, mod_ref, g_ref, o_ref):
    y = _dot(a_ref[0], w_ref[...])
    right = x_ref[0] + mod_ref[0, GATE2:GATE2 + 1, :] * y
    left = left_ref[0]
    wl = left.shape[1]
    d = wl + right.shape[1]
    ms = (jnp.sum(left * left, axis=-1, keepdims=True) + jnp.sum(right * right, axis=-1, keepdims=True)) / d
    r = lax.rsqrt(ms + EPS)
    o_ref[0, :, :wl] = left * r * g_ref[:, :wl]
    o_ref[0, :, wl:] = right * r * g_ref[:, wl:]


def _ffn_down_norm(act, w_bf16, x, left, mod, mod_row, g):
    b, l, d = x.shape
    tm = _tile("ffn_down_tm", l)
    tn = _tile("ffn_down_tn", d)
    f = act.shape[2]
    last = d // tn - 1
    vmem = 2 * tm * f * 2 + 2 * f * tn * 2 + 6 * tm * d * 4 + (6 << 20)
    return pl.pallas_call(
        _ffn_down_norm_kernel,
        grid=(b, l // tm),
        in_specs=[
            pl.BlockSpec((1, tm, f), lambda bi, m: (bi, m, 0)),
            pl.BlockSpec((f, tn), lambda bi, m: (0, last)),
            pl.BlockSpec((1, tm, tn), lambda bi, m: (bi, m, last)),
            pl.BlockSpec((1, tm, d - tn), lambda bi, m: (bi, m, 0)),
            pl.BlockSpec((1, 6, tn), lambda bi, m: (bi if mod_row is None else mod_row, 0, last)),
            pl.BlockSpec((1, d), lambda bi, m: (0, 0)),
        ],
        out_specs=pl.BlockSpec((1, tm, d), lambda bi, m: (bi, m, 0)),
        out_shape=jax.ShapeDtypeStruct((b, l, d), F32),
        compiler_params=_cparams(("parallel", "parallel"), vmem),
        name="ffn_down_norm",
    )(act, w_bf16, x, left, mod, g)


def _pool_kernel(xm_ref, xp_ref, xn_ref, mod_ref, g_ref, w_ref, ps_ref, g2_ref, x_out, h_out, x1_scr,
                 *, tm, n_m, seq_len):
    m = pl.program_id(1)
    g = g_ref[...]
    shift = mod_ref[0, SHIFT1:SHIFT1 + 1, :]
    scale = mod_ref[0, SCALE1:SCALE1 + 1, :]
    xm = xm_ref[0]
    hp = _norm_mod(xp_ref[0], g, shift, scale) * (m > 0).astype(F32)
    hn = _norm_mod(xn_ref[0], g, shift, scale) * (m < n_m - 1).astype(F32)
    hh = jnp.concatenate([hp, _norm_mod(xm, g, shift, scale), hn], axis=0)
    rows = tm + 2 * HALO
    t = (m * tm + lax.broadcasted_iota(jnp.int32, (tm, 1), 0)).astype(F32)
    gw = hh.shape[1] // len(POOL_WINDOWS)
    gate = mod_ref[0, GATE1:GATE1 + 1, :]
    for gi, w in enumerate(POOL_WINDOWS):
        sl = slice(gi * gw, (gi + 1) * gw)
        hg = hh[:, sl]
        acc = hg
        span = 1
        while span < w:
            acc = acc + pltpu.roll(acc, span, 0)
            span *= 2
        ahead = w // 2 - 1
        if ahead:
            acc = pltpu.roll(acc, rows - ahead, 0)
        win = acc[HALO:HALO + tm]
        cnt = jnp.minimum(t + (w // 2 - 1), seq_len - 1.0) - jnp.maximum(t - w // 2, 0.0) + 1.0
        p = (win / cnt - hg[HALO:HALO + tm]).astype(BF16)
        y = _dot(p, w_ref[gi].astype(BF16)) * ps_ref[:, sl]
        x1_scr[:, sl] = xm[:, sl] + gate[:, sl] * y
    x1 = x1_scr[...]
    x_out[0] = x1
    h_out[0] = _norm_mod(x1, g2_ref[...], mod_ref[0, SHIFT2:SHIFT2 + 1, :],
                         mod_ref[0, SCALE2:SCALE2 + 1, :]).astype(BF16)


def _pool_mix(x, mod, mod_row, g, pool_w, pool_scale, g2):
    b, l, d = x.shape
    tm = _tile("pool_tm", l)
    n_m = l // tm
    hb = tm // HALO
    ng, gw, _ = pool_w.shape
    row = pl.BlockSpec((1, tm, d), lambda bi, m: (bi, m, 0))
    vec = pl.BlockSpec((1, d), lambda bi, m: (0, 0))
    vmem = 7 * tm * d * 4 + 2 * ng * gw * gw * 4 + 8 * (tm + 2 * HALO) * d * 4 + (6 << 20)
    return pl.pallas_call(
        functools.partial(_pool_kernel, tm=tm, n_m=n_m, seq_len=l),
        grid=(b, n_m),
        in_specs=[
            row,
            pl.BlockSpec((1, HALO, d), lambda bi, m: (bi, jnp.maximum(m * hb - 1, 0), 0)),
            pl.BlockSpec((1, HALO, d), lambda bi, m: (bi, jnp.minimum((m + 1) * hb, l // HALO - 1), 0)),
            pl.BlockSpec((1, 6, d), _mod_index(mod_row, 0)),
            vec,
            pl.BlockSpec((ng, gw, gw), lambda bi, m: (0, 0, 0)),
            vec, vec,
        ],
        out_specs=[row, row],
        out_shape=[jax.ShapeDtypeStruct((b, l, d), F32), jax.ShapeDtypeStruct((b, l, d), BF16)],
        scratch_shapes=[pltpu.VMEM((tm, d), F32)],
        compiler_params=_cparams(("parallel", "parallel"), vmem),
        name="pool_mix",
    )(x, x, x, mod, g, pool_w, pool_scale, g2)


def _final_norm_kernel(x_ref, g_ref, o_ref):
    x = x_ref[0]
    o_ref[0] = x * lax.rsqrt(jnp.mean(x * x, axis=-1, keepdims=True) + EPS) * g_ref[...]


def _final_norm(x, g):
    b, l, d = x.shape
    tm = _tile("final_tm", l)
    return pl.pallas_call(
        _final_norm_kernel,
        grid=(b, l // tm),
        in_specs=[pl.BlockSpec((1, tm, d), lambda bi, m: (bi, m, 0)), pl.BlockSpec((1, d), lambda bi, m: (0, 0))],
        out_specs=pl.BlockSpec((1, tm, d), lambda bi, m: (bi, m, 0)),
        out_shape=jax.ShapeDtypeStruct((b, l, d), F32),
        compiler_params=_cparams(("parallel", "parallel"), 6 * tm * d * 4 + (4 << 20)),
        name="final_norm",
    )(x, g)


def _axial_angles(rows, dim):
    row = jnp.broadcast_to(jnp.arange(rows, dtype=F32)[:, None], (rows, GRID_W)).reshape(-1)
    col = jnp.broadcast_to(jnp.arange(GRID_W, dtype=F32)[None, :], (rows, GRID_W)).reshape(-1)
    n_freq = dim // 4
    inv = ROPE_BASE ** (-jnp.arange(n_freq, dtype=F32) / n_freq)
    return jnp.concatenate([row[:, None] * inv, col[:, None] * inv], axis=-1)


def _rope_tables(rows, dim):
    ang = _axial_angles(rows, dim)
    cos, sin = jnp.cos(ang), jnp.sin(ang)
    cos_t = jnp.concatenate([cos, cos], axis=-1)
    sin_t = jnp.concatenate([-sin, sin], axis=-1)
    pad = LANES - dim
    if pad:
        cos_t = jnp.pad(cos_t, ((0, 0), (0, pad)))
        sin_t = jnp.pad(sin_t, ((0, 0), (0, pad)))
    return cos_t, sin_t


def _mla_weights(g_q, w_uq, g_kv, w_ukv, heads, ret_w):
    q_rank, kv_rank = g_q.shape[0], g_kv.shape[0]
    wq = w_uq.reshape(q_rank, heads, HEAD_DIM + MLA_DR)
    wq = jnp.pad(wq, ((0, 0), (0, 0), (0, HEAD_DIM - MLA_DR))).reshape(q_rank, heads * 2 * HEAD_DIM).astype(BF16)
    wkv = w_ukv.reshape(kv_rank, heads, 2 * HEAD_DIM)
    wkn = wkv[:, :, :HEAD_DIM].reshape(kv_rank, ret_w).astype(BF16)
    wv = wkv[:, :, HEAD_DIM:].reshape(kv_rank, ret_w).astype(BF16)
    return g_q[None, :], wq, g_kv[None, :], wkn, wv


def _conv_ffn(h, x, mod, mod_row, w_up, conv_w, conv_b, w_down, layer, final_g=None):
    act, w_down_bf16 = _ffn_up(h, w_up, conv_w, conv_b[:, None, :], w_down, layer)
    d = x.shape[2]
    n_col_tiles = d // _tile("ffn_down_tn", d)
    if final_g is None:
        return _ffn_down(act, w_down_bf16, x, mod, mod_row)
    if n_col_tiles == 1:
        return _final_norm(_ffn_down(act, w_down_bf16, x, mod, mod_row), final_g)
    left = _ffn_down(act, w_down_bf16, x, mod, mod_row, n_col_tiles - 1)
    return _ffn_down_norm(act, w_down_bf16, x, left, mod, mod_row, final_g)


def kernel(x, c, ctx, c_ctx, ada_w, ada_b, norm1_g, norm2_g, ffn_w_up, ffn_conv_w, ffn_conv_b, ffn_w_down, mix_w_in, mla_q_norm_g, mla_w_uq, mla_kv_norm_g, mla_w_ukv, ret_decay_f, ret_decay_b, mix_w_out, pool_w, pool_scale, final_g):
    b, l, d = x.shape
    lc = ctx.shape[1]
    depth = ada_w.shape[0]
    heads = ret_decay_f.shape[1]
    ret_w = heads * HEAD_DIM
    rows = l // GRID_W
    rope_ret = _rope_tables(rows, HEAD_DIM)
    rope_mla = _rope_tables(rows, MLA_DR)

    ctx_row = b
    cc = jnp.concatenate([c, c_ctx[None, :], jnp.zeros((-(b + 1) % 8, d), F32)], axis=0)
    mods = _ada(cc, ada_w, ada_b).reshape(depth, cc.shape[0], 6, d)

    x_lat, x_ctx = x, ctx
    for layer in range(depth):
        j = layer // 2
        with_ctx = layer < depth - 1
        mod = mods[layer]
        g1 = norm1_g[layer][None, :]
        g2 = norm2_g[layer][None, :]
        h_ctx = None
        if layer % 2 == 0:
            w_in_t = jnp.swapaxes(mix_w_in[j], 0, 1)
            mla_w = _mla_weights(mla_q_norm_g[j], mla_w_uq[j], mla_kv_norm_g[j], mla_w_ukv[j], heads, ret_w)
            dec = jnp.stack([ret_decay_f[j], ret_decay_b[j]]).astype(F32)

            h1_lat, q_lat, k_lat, v_lat = _inproj_mla(x_lat, mod, None, g1, w_in_t, 4 * ret_w, *mla_w, rope_mla, heads)
            h1_ctx, q_ctx, k_ctx, v_ctx = _inproj_mla(x_ctx.reshape(1, b * lc, d), mod, ctx_row, g1, w_in_t,
                                                      4 * ret_w, *mla_w, None, heads)
            q_ctx, k_ctx, v_ctx = [a.reshape(b, lc, -1) for a in (q_ctx, k_ctx, v_ctx)]
            ret_lat, w_out_bf16 = _inproj_ret(h1_lat, w_in_t, rope_ret, ret_w, cast_extra=mix_w_out[j])
            ret_ctx = _inproj_ret(h1_ctx, w_in_t, None, ret_w).reshape(b, lc, 4 * ret_w)

            o_ret_lat, o_ret_ctx = _retention(dec, ret_lat, ret_ctx, heads)
            o_mla_lat = _attention(q_lat, k_ctx, v_ctx, k_lat, v_lat, heads)
            x_lat, h_lat = _mix_out(o_ret_lat, o_mla_lat, w_out_bf16, x_lat, mod, None, g2)
            if with_ctx:
                o_mla_ctx = _attention(q_ctx, k_ctx, v_ctx, None, None, heads)
                x_ctx, h_ctx = _mix_out(o_ret_ctx, o_mla_ctx, w_out_bf16, x_ctx, mod, ctx_row, g2)
        else:
            ps = pool_scale[j][None, :]
            if with_ctx:
                x_ctx, h_ctx = _pool_mix(x_ctx, mod, ctx_row, g1, pool_w[j], ps, g2)
            x_lat, h_lat = _pool_mix(x_lat, mod, None, g1, pool_w[j], ps, g2)
        last = layer == depth - 1
        x_lat = _conv_ffn(h_lat, x_lat, mod, None, ffn_w_up, ffn_conv_w, ffn_conv_b, ffn_w_down, layer,
                          final_g=final_g[None, :] if last else None)
        if with_ctx:
            x_ctx = _conv_ffn(h_ctx, x_ctx, mod, ctx_row, ffn_w_up, ffn_conv_w, ffn_conv_b, ffn_w_down, layer)
    return x_lat
```

```python
import functools

import jax
import jax.numpy as jnp
from jax import lax
from jax.experimental import pallas as pl
from jax.experimental.pallas import tpu as pltpu

F32 = jnp.float32
BF16 = jnp.bfloat16

GRID_W = 64
HEAD_DIM = 128
RET_CHUNK = 128
MLA_DR = 64
POOL_WINDOWS = (2, 4, 8, 16)
ROPE_BASE = 10000.0
EPS = 1e-6
LOG2_E = 1.4426950408889634

LANES = 128
MXU_DIM = 256
HALO = 16
V7X_VMEM_BYTES = 64 * 1024 * 1024
VMEM_CAP = V7X_VMEM_BYTES - 8 * 1024 * 1024

SHIFT1, SCALE1, GATE1, SHIFT2, SCALE2, GATE2 = range(6)

TILES = dict(
    inproj_ret_tm=1024,
    inproj_mla_tm=512,
    attn_tq=512, attn_heads_per_step=2,
    mix_out_tm=512,
    ffn_up_tm=1024, ffn_up_tn=512,
    ffn_down_tm=512, ffn_down_tn=1024,
    pool_tm=512,
    final_tm=512,
    ada_tn=1024,
)


def _tile(name, extent):
    tile = min(extent, TILES[name])
    assert extent % tile == 0, f"{name}: extent {extent} is not a multiple of its tile {tile}"
    return tile


def _cparams(sem, vmem_bytes, flags=None):
    return pltpu.CompilerParams(dimension_semantics=sem, vmem_limit_bytes=min(int(vmem_bytes), VMEM_CAP),
                                flags=flags)


def _silu(x):
    return x / (1.0 + jnp.exp(-x))


def _norm_mod(x, g, shift, scale):
    y = x * lax.rsqrt(jnp.mean(x * x, axis=-1, keepdims=True) + EPS) * g
    return y * (1.0 + scale) + shift


def _dot(a, b):
    return jnp.dot(a, b, preferred_element_type=F32)


def _dot_nt(a, b):
    return lax.dot_general(a, b, (((1,), (1,)), ((), ())), preferred_element_type=F32)


def _dot_tn(a, b):
    return lax.dot_general(a, b, (((0,), (0,)), ((), ())), preferred_element_type=F32)


def _mod_index(mod_row, batch_axis, col_axis=None):
    def index(*ids):
        row = ids[batch_axis] if mod_row is None else mod_row
        return (row, 0, 0 if col_axis is None else ids[col_axis])
    return index


def _ada_kernel(c_ref, w_ref, b_ref, o_ref):
    s = _silu(c_ref[...]).astype(BF16)
    o_ref[0] = _dot(s, w_ref[0].astype(BF16)) + b_ref[0]


def _ada(cc, ada_w, ada_b):
    depth, d, n6 = ada_w.shape
    rows = cc.shape[0]
    tn = _tile("ada_tn", n6)
    return pl.pallas_call(
        _ada_kernel,
        grid=(depth, n6 // tn),
        in_specs=[
            pl.BlockSpec((rows, d), lambda l, n: (0, 0)),
            pl.BlockSpec((1, d, tn), lambda l, n: (l, 0, n)),
            pl.BlockSpec((1, 1, tn), lambda l, n: (l, 0, n)),
        ],
        out_specs=pl.BlockSpec((1, rows, tn), lambda l, n: (l, 0, n)),
        out_shape=jax.ShapeDtypeStruct((depth, rows, n6), F32),
        compiler_params=_cparams(("parallel", "parallel"), 2 * d * tn * 4 + d * tn * 2 + (8 << 20)),
        name="ada_mod",
    )(cc, ada_w, ada_b.reshape(depth, 1, n6))


def _inproj_ret_kernel(*refs, rope, cast_extra, heads_per_tile, k_scale):
    refs = list(refs)
    h_ref = refs.pop(0)
    if rope:
        cos_ref, sin_ref = refs.pop(0), refs.pop(0)
    w_ref = refs.pop(0)
    if cast_extra:
        extra_ref = refs.pop(0)
    o_ref = refs.pop(0)
    if cast_extra:
        extra_out = refs.pop(0)
    (w_scr,) = refs
    n = pl.program_id(0)

    @pl.when((pl.program_id(1) == 0) & (pl.program_id(2) == 0))
    def _():
        w_scr[...] = w_ref[...].astype(BF16)

    if cast_extra:
        extra_out[...] = extra_ref[...].astype(BF16)
    scale = jnp.where(n == 1, k_scale, 1.0).astype(F32)
    if rope:
        rotated = n < 2
        cos = jnp.where(rotated, cos_ref[...], 1.0) * scale
        sin = jnp.where(rotated, sin_ref[...], 0.0) * scale
    heads_per_piece = MXU_DIM // HEAD_DIM
    for j in range(heads_per_tile // heads_per_piece):
        cs = slice(j * MXU_DIM, (j + 1) * MXU_DIM)
        z = _dot_nt(h_ref[0], w_scr[cs, :])
        if rope:
            for h in range(heads_per_piece):
                zh = z[:, h * HEAD_DIM:(h + 1) * HEAD_DIM]
                r = zh * cos + pltpu.roll(zh, HEAD_DIM // 2, 1) * sin
                o_ref[0, :, j * MXU_DIM + h * HEAD_DIM:j * MXU_DIM + (h + 1) * HEAD_DIM] = r.astype(BF16)
        else:
            o_ref[0, :, cs] = (z * scale).astype(BF16)


def _inproj_ret(h, w_in_t, rope_tabs, ret_w, cast_extra=None):
    b, l, d = h.shape
    tm = _tile("inproj_ret_tm", l)
    tn = ret_w
    n_tiles = 4
    n_m = l // tm
    rope = rope_tabs is not None
    in_specs = [pl.BlockSpec((1, tm, d), lambda n, bi, m: (bi, m, 0))]
    args = [h]
    if rope:
        in_specs += [pl.BlockSpec((tm, HEAD_DIM), lambda n, bi, m: (m, 0))] * 2
        args += list(rope_tabs)
    in_specs.append(pl.BlockSpec((tn, d), lambda n, bi, m: (n, 0)))
    args.append(w_in_t)
    out_specs = [pl.BlockSpec((1, tm, tn), lambda n, bi, m: (bi, m, n))]
    out_shape = [jax.ShapeDtypeStruct((b, l, n_tiles * tn), BF16)]
    vmem = 2 * tm * d * 2 + 2 * d * tn * 4 + d * tn * 2 + 2 * tm * tn * 2 + 4 * tm * tn * 4 + (6 << 20)
    if cast_extra is not None:
        er, ec = cast_extra.shape
        band = er // (n_tiles * b * n_m)
        assert band * n_tiles * b * n_m == er and band % HALO == 0, (er, band)
        extra_spec = pl.BlockSpec((band, ec), lambda n, bi, m: ((n * b + bi) * n_m + m, 0))
        in_specs.append(extra_spec)
        args.append(cast_extra)
        out_specs.append(extra_spec)
        out_shape.append(jax.ShapeDtypeStruct((er, ec), BF16))
        vmem += 2 * band * ec * 6
    outs = pl.pallas_call(
        functools.partial(_inproj_ret_kernel, rope=rope, cast_extra=cast_extra is not None,
                          heads_per_tile=tn // HEAD_DIM, k_scale=HEAD_DIM ** -0.5),
        grid=(n_tiles, b, l // tm),
        in_specs=in_specs,
        out_specs=out_specs,
        out_shape=out_shape,
        scratch_shapes=[pltpu.VMEM((tn, d), BF16)],
        compiler_params=_cparams(("arbitrary", "arbitrary", "arbitrary"), vmem),
        name="inproj_ret_rope" if rope else "inproj_ret",
    )(*args)
    return outs if cast_extra is not None else outs[0]


def _rope_half_padded(x, cos, sin):
    rot = pltpu.roll(x, MLA_DR // 2, 1) + pltpu.roll(x, LANES - MLA_DR // 2, 1)
    return x * cos + rot * sin


def _inproj_mla_kernel(*refs, rope, heads, q_rank, kv_rank, q_scale):
    refs = list(refs)
    x_ref, mod_ref, g_ref = refs[:3]
    del refs[:3]
    if rope:
        cos_ref, sin_ref = refs[:2]
        del refs[:2]
    (wcq_ref, wckv_ref, wkr_ref, gq_ref, wq_ref, gkv_ref, wkn_ref, wv_ref,
     h_out, q_out, k_out, v_out, wt_scr) = refs

    @pl.when((pl.program_id(0) == 0) & (pl.program_id(1) == 0))
    def _():
        wt_scr[0:q_rank, :] = wcq_ref[...].astype(BF16)
        wt_scr[q_rank:q_rank + kv_rank, :] = wckv_ref[...].astype(BF16)
        wt_scr[q_rank + kv_rank:q_rank + kv_rank + MLA_DR, :] = wkr_ref[...].astype(BF16)
        wt_scr[q_rank + kv_rank + MLA_DR:, :] = jnp.zeros((LANES - MLA_DR, wt_scr.shape[1]), BF16)

    h = _norm_mod(x_ref[0], g_ref[...], mod_ref[0, SHIFT1:SHIFT1 + 1, :], mod_ref[0, SCALE1:SCALE1 + 1, :])
    h = h.astype(BF16)
    h_out[0] = h
    zt = _dot_nt(h, wt_scr[...])
    cq = zt[:, :q_rank]
    ckv = zt[:, q_rank:q_rank + kv_rank]
    kr = zt[:, q_rank + kv_rank:]
    cqn = (cq * lax.rsqrt(jnp.mean(cq * cq, axis=-1, keepdims=True) + EPS) * gq_ref[...]).astype(BF16)
    ckvn = (ckv * lax.rsqrt(jnp.mean(ckv * ckv, axis=-1, keepdims=True) + EPS) * gkv_ref[...]).astype(BF16)
    q = _dot(cqn, wq_ref[...])
    kn = _dot(ckvn, wkn_ref[...])
    v_out[0] = _dot(ckvn, wv_ref[...]).astype(BF16)
    if rope:
        cos = cos_ref[...]
        sin = sin_ref[...]
        kr = _rope_half_padded(kr, cos, sin)
    kr = kr.astype(BF16)
    for hd in range(heads):
        base = hd * 2 * HEAD_DIM
        qn = q[:, base:base + HEAD_DIM]
        qr = q[:, base + HEAD_DIM:base + 2 * HEAD_DIM]
        if rope:
            qr = _rope_half_padded(qr, cos, sin)
        q_out[0, :, base:base + HEAD_DIM] = (qn * q_scale).astype(BF16)
        q_out[0, :, base + HEAD_DIM:base + 2 * HEAD_DIM] = (qr * q_scale).astype(BF16)
        k_out[0, :, base:base + HEAD_DIM] = kn[:, hd * HEAD_DIM:(hd + 1) * HEAD_DIM].astype(BF16)
        k_out[0, :, base + HEAD_DIM:base + 2 * HEAD_DIM] = kr


def _inproj_mla(x, mod, mod_row, g, w_in_t, tail_row, gq, wq, gkv, wkn, wv, rope_tabs, heads):
    b, l, d = x.shape
    tm = _tile("inproj_mla_tm", l)
    q_rank, kv_rank = gq.shape[1], gkv.shape[1]
    rope = rope_tabs is not None
    full = lambda a: pl.BlockSpec(a.shape, lambda bi, m: (0,) * a.ndim)
    in_specs = [pl.BlockSpec((1, tm, d), lambda bi, m: (bi, m, 0)),
                pl.BlockSpec((1, 6, d), _mod_index(mod_row, 0)), full(g)]
    args = [x, mod, g]
    if rope:
        in_specs += [pl.BlockSpec((tm, LANES), lambda bi, m: (m, 0))] * 2
        args += list(rope_tabs)
    for start, rows in ((tail_row, q_rank), (tail_row + q_rank, kv_rank), (tail_row + q_rank + kv_rank, MLA_DR)):
        assert start % rows == 0, (start, rows)
        in_specs.append(pl.BlockSpec((rows, d), functools.partial(lambda blk, bi, m: (blk, 0), start // rows)))
        args.append(w_in_t)
    for a in (gq, wq, gkv, wkn, wv):
        in_specs.append(full(a))
        args.append(a)
    qw = heads * 2 * HEAD_DIM
    vw = heads * HEAD_DIM
    tail = q_rank + kv_rank + LANES
    out_spec = lambda w: pl.BlockSpec((1, tm, w), lambda bi, m: (bi, m, 0))
    wbytes = sum(a.size * a.dtype.itemsize for a in (wq, wkn, wv))
    vmem = (2 * tm * d * 4 + 2 * tail * d * 4 + tail * d * 2 + 2 * wbytes + 2 * tm * (d + 2 * qw + vw) * 2
            + tm * (3 * d * 4 + 6 * qw * 4) + (4 << 20))
    return pl.pallas_call(
        functools.partial(_inproj_mla_kernel, rope=rope, heads=heads, q_rank=q_rank, kv_rank=kv_rank,
                          q_scale=(HEAD_DIM + MLA_DR) ** -0.5 * LOG2_E),
        grid=(b, l // tm),
        in_specs=in_specs,
        out_specs=[out_spec(d), out_spec(qw), out_spec(qw), out_spec(vw)],
        out_shape=[jax.ShapeDtypeStruct((b, l, d), BF16), jax.ShapeDtypeStruct((b, l, qw), BF16),
                   jax.ShapeDtypeStruct((b, l, qw), BF16), jax.ShapeDtypeStruct((b, l, vw), BF16)],
        scratch_shapes=[pltpu.VMEM((tail, d), BF16)],
        compiler_params=_cparams(("arbitrary", "arbitrary"), vmem),
        name="inproj_mla_rope" if rope else "inproj_mla",
    )(*args)


def _log_sigmoid(x):
    return jnp.minimum(x, 0.0) - jnp.log1p(jnp.exp(-jnp.abs(x)))


def _retention_kernel(dec_ref, ql_ref, kl_ref, vl_ref, gl_ref, qc_ref, kc_ref, vc_ref, gc_ref, ol_ref, oc_ref,
                      st_scr, *, n_ctx, n_lat, c):
    dk = HEAD_DIM
    hd = pl.program_id(1)
    lf = _log_sigmoid(jnp.full((c, c), dec_ref[0, hd], F32))
    lb = _log_sigmoid(jnp.full((c, c), dec_ref[1, hd], F32))
    ii = lax.broadcasted_iota(jnp.int32, (c, c), 0).astype(F32)
    jj = lax.broadcasted_iota(jnp.int32, (c, c), 1).astype(F32)
    diff = ii - jj
    fwd = diff >= 0
    decay = jnp.where(fwd, jnp.exp(jnp.where(fwd, diff, 0.0) * lf), jnp.exp(jnp.where(fwd, 0.0, -diff) * lb))
    ri = lax.broadcasted_iota(jnp.int32, (c, dk), 0).astype(F32)
    lfr = _log_sigmoid(jnp.full((c, dk), dec_ref[0, hd], F32))
    lbr = _log_sigmoid(jnp.full((c, dk), dec_ref[1, hd], F32))
    q_dec_f = jnp.exp((ri + 1.0) * lfr)
    k_dec_f = jnp.exp((c - 1.0 - ri) * lfr)
    q_dec_b = jnp.exp((c - ri) * lbr)
    k_dec_b = jnp.exp(ri * lbr)
    c_dec_f = jnp.exp(c * _log_sigmoid(jnp.full((dk, dk), dec_ref[0, hd], F32)))
    c_dec_b = jnp.exp(c * _log_sigmoid(jnp.full((dk, dk), dec_ref[1, hd], F32)))

    blocks = [(qc_ref, kc_ref, vc_ref, gc_ref, oc_ref, i) for i in range(n_ctx)]
    blocks += [(ql_ref, kl_ref, vl_ref, gl_ref, ol_ref, i) for i in range(n_lat)]

    def rows(ref, i):
        return ref[0, i * c:(i + 1) * c, :]

    for g, (_, k_ref, v_ref, _, _, i) in enumerate(blocks):
        kf = rows(k_ref, i).astype(F32)
        kk = jnp.concatenate([(kf * k_dec_f).astype(BF16), (kf * k_dec_b).astype(BF16)], axis=1)
        st_scr[g] = _dot_tn(kk, rows(v_ref, i))

    s = jnp.zeros((dk, dk), F32)
    for g in range(n_ctx + n_lat):
        kv = st_scr[g, :dk, :]
        st_scr[g, :dk, :] = s
        s = s * c_dec_f + kv
    s = jnp.zeros((dk, dk), F32)
    for g in list(range(n_ctx - 1, -1, -1)) + list(range(n_ctx + n_lat - 1, n_ctx - 1, -1)):
        kv = st_scr[g, dk:, :]
        st_scr[g, dk:, :] = s
        s = s * c_dec_b + kv

    for g, (q_ref, k_ref, v_ref, g_ref, o_ref, i) in enumerate(blocks):
        q = rows(q_ref, i)
        p = (_dot_nt(q, rows(k_ref, i)) * decay).astype(BF16)
        qf = q.astype(F32)
        qq = jnp.concatenate([(qf * q_dec_f).astype(BF16), (qf * q_dec_b).astype(BF16)], axis=1)
        o = _dot(p, rows(v_ref, i)) + _dot(qq, st_scr[g].astype(BF16))
        o = o * lax.rsqrt(jnp.mean(o * o, axis=-1, keepdims=True) + EPS)
        o_ref[0, i * c:(i + 1) * c, :] = (o * _silu(rows(g_ref, i).astype(F32))).astype(BF16)


def _retention(dec, ret_lat, ret_ctx, heads):
    b, l, _ = ret_lat.shape
    lc = ret_ctx.shape[1]
    c = MXU_DIM if (l % MXU_DIM == 0 and lc % MXU_DIM == 0) else RET_CHUNK
    n_lat, n_ctx = l // c, lc // c
    col = lambda rows, group: pl.BlockSpec((1, rows, HEAD_DIM), lambda bi, h: (bi, 0, group * heads + h))
    in_specs = [pl.BlockSpec(memory_space=pltpu.SMEM)]
    in_specs += [col(l, grp) for grp in range(4)] + [col(lc, grp) for grp in range(4)]
    out_col = lambda rows: pl.BlockSpec((1, rows, HEAD_DIM), lambda bi, h: (bi, 0, h))
    vmem = 2 * 5 * (l + lc) * HEAD_DIM * 2 + (n_lat + n_ctx) * 2 * HEAD_DIM * HEAD_DIM * 4 + (12 << 20)
    return pl.pallas_call(
        functools.partial(_retention_kernel, n_ctx=n_ctx, n_lat=n_lat, c=c),
        grid=(b, heads),
        in_specs=in_specs,
        out_specs=[out_col(l), out_col(lc)],
        out_shape=[jax.ShapeDtypeStruct((b, l, heads * HEAD_DIM), BF16),
                   jax.ShapeDtypeStruct((b, lc, heads * HEAD_DIM), BF16)],
        scratch_shapes=[pltpu.VMEM((n_lat + n_ctx, 2 * HEAD_DIM, HEAD_DIM), F32)],
        compiler_params=_cparams(("parallel", "parallel"), vmem),
        name="retention",
    )(dec, ret_lat, ret_lat, ret_lat, ret_lat, ret_ctx, ret_ctx, ret_ctx, ret_ctx)


def _attn_kernel(*refs, with_lat, tq):
    if with_lat:
        q_ref, kc_ref, vc_ref, kl_ref, vl_ref, o_ref = refs
    else:
        q_ref, kc_ref, vc_ref, o_ref = refs
    kw = 2 * HEAD_DIM
    for hp in range(o_ref.shape[2] // HEAD_DIM):
        kcols = slice(hp * kw, (hp + 1) * kw)
        vcols = slice(hp * HEAD_DIM, (hp + 1) * HEAD_DIM)
        for i in range(q_ref.shape[1] // tq):
            q = q_ref[0, i * tq:(i + 1) * tq, kcols]
            sc = _dot_nt(q, kc_ref[0, :, kcols])
            m = jnp.max(sc, axis=-1, keepdims=True)
            if with_lat:
                sl = _dot_nt(q, kl_ref[0, :, kcols])
                m = jnp.maximum(m, jnp.max(sl, axis=-1, keepdims=True))
            pc = jnp.exp2(sc - m)
            den = jnp.sum(pc, axis=-1, keepdims=True)
            o = _dot(pc.astype(BF16), vc_ref[0, :, vcols])
            if with_lat:
                p_lat = jnp.exp2(sl - m)
                den = den + jnp.sum(p_lat, axis=-1, keepdims=True)
                o = o + _dot(p_lat.astype(BF16), vl_ref[0, :, vcols])
            o_ref[0, i * tq:(i + 1) * tq, vcols] = (o / den).astype(BF16)


def _attention(q, k_ctx, v_ctx, k_lat, v_lat, heads):
    b, lq, _ = q.shape
    tq = _tile("attn_tq", lq)
    hps = _tile("attn_heads_per_step", heads)
    lc = k_ctx.shape[1]
    with_lat = k_lat is not None
    kw = 2 * HEAD_DIM
    head = lambda rows, w: pl.BlockSpec((1, rows, hps * w), lambda bi, h: (bi, 0, h))
    in_specs = [head(lq, kw), head(lc, kw), head(lc, HEAD_DIM)]
    args = [q, k_ctx, v_ctx]
    lk = lc
    if with_lat:
        ll = k_lat.shape[1]
        lk += ll
        in_specs += [head(ll, kw), head(ll, HEAD_DIM)]
        args += [k_lat, v_lat]
    vmem = 2 * hps * (lq * kw + lk * (kw + HEAD_DIM) + lq * HEAD_DIM) * 2 + 8 * tq * lk * 4 + (8 << 20)
    return pl.pallas_call(
        functools.partial(_attn_kernel, with_lat=with_lat, tq=tq),
        grid=(b, heads // hps),
        in_specs=in_specs,
        out_specs=head(lq, HEAD_DIM),
        out_shape=jax.ShapeDtypeStruct((b, lq, heads * HEAD_DIM), BF16),
        compiler_params=_cparams(("parallel", "parallel"), vmem),
        name="mla_attn" if with_lat else "mla_attn_ctx",
    )(*args)


def _mix_out_kernel(a1_ref, a2_ref, w_ref, x_ref, mod_ref, g2_ref, x_out, h_out):
    k1 = a1_ref.shape[2]
    y = _dot(a1_ref[0], w_ref[:k1, :]) + _dot(a2_ref[0], w_ref[k1:, :])
    x1 = x_ref[0] + mod_ref[0, GATE1:GATE1 + 1, :] * y
    x_out[0] = x1
    h_out[0] = _norm_mod(x1, g2_ref[...], mod_ref[0, SHIFT2:SHIFT2 + 1, :],
                         mod_ref[0, SCALE2:SCALE2 + 1, :]).astype(BF16)


def _mix_out(a1, a2, w_bf16, x, mod, mod_row, g2):
    b, l, d = x.shape
    tm = _tile("mix_out_tm", l)
    k1, k2 = a1.shape[2], a2.shape[2]
    row = lambda w: pl.BlockSpec((1, tm, w), lambda bi, m: (bi, m, 0))
    vmem = (k1 + k2) * d * 2 + 2 * tm * (k1 + k2) * 2 + 4 * tm * d * 4 + 2 * tm * d * 2 + 6 * tm * d * 4 + (4 << 20)
    return pl.pallas_call(
        _mix_out_kernel,
        grid=(b, l // tm),
        in_specs=[row(k1), row(k2),
                  pl.BlockSpec((k1 + k2, d), lambda bi, m: (0, 0)),
                  row(d),
                  pl.BlockSpec((1, 6, d), _mod_index(mod_row, 0)),
                  pl.BlockSpec((1, d), lambda bi, m: (0, 0))],
        out_specs=[row(d), row(d)],
        out_shape=[jax.ShapeDtypeStruct((b, l, d), F32), jax.ShapeDtypeStruct((b, l, d), BF16)],
        compiler_params=_cparams(("parallel", "parallel"), vmem),
        name="mix_out",
    )(a1, a2, w_bf16, x, mod, g2)


def _ffn_up_kernel(hm_ref, hp_ref, hn_ref, wa_ref, wg_ref, cwa_ref, cwg_ref, cba_ref, cbg_ref, wd_ref,
                   o_ref, wd_out, lhs_scr, wa_scr, wg_scr, *, tm, n_m, row_pieces):
    m = pl.program_id(2)

    @pl.when((pl.program_id(1) == 0) & (m == 0))
    def _():
        wa_scr[...] = wa_ref[...].astype(BF16)
        wg_scr[...] = wg_ref[...].astype(BF16)

    wd_out[...] = wd_ref[...].astype(BF16)

    zero = jnp.zeros((HALO, lhs_scr.shape[1]), BF16)
    lhs_scr[0:HALO, :] = jnp.where(m > 0, hp_ref[0], zero)
    lhs_scr[HALO:HALO + tm, :] = hm_ref[0]
    lhs_scr[HALO + tm:, :] = jnp.where(m < n_m - 1, hn_ref[0], zero)
    rows = tm + 2 * HALO
    pr = rows // row_pieces

    def conv(w_scr, cw_ref, cb_ref):
        u = jnp.concatenate([_dot(lhs_scr[i * pr:(i + 1) * pr, :], w_scr[...]) for i in range(row_pieces)], axis=0)
        prev = pltpu.roll(u, 1, 0)[HALO:HALO + tm]
        nxt = pltpu.roll(u, rows - 1, 0)[HALO:HALO + tm]
        cw = cw_ref[...]
        return cw[0:1, :] * prev + cw[1:2, :] * u[HALO:HALO + tm] + cw[2:3, :] * nxt + cb_ref[...]

    a = conv(wa_scr, cwa_ref, cba_ref)
    gt = conv(wg_scr, cwg_ref, cbg_ref)
    o_ref[0] = (_silu(gt) * a).astype(BF16)


def _ffn_up(h, w_up, conv_w, conv_b, w_down, layer):
    b, l, d = h.shape
    d_out = w_down.shape[2]
    tm = _tile("ffn_up_tm", l)
    f = w_up.shape[2] // 2
    tn = _tile("ffn_up_tn", f)
    n_m, n_n = l // tm, f // tn
    hb = tm // HALO
    rows = tm + 2 * HALO
    row_pieces = next(p for p in (3, 2, 1) if rows % (p * HALO) == 0)
    col = lambda rows, off: pl.BlockSpec((None, rows, tn), lambda n, bi, m: (layer, 0, n + off))
    band = f // (n_n * b * n_m)
    assert band * n_n * b * n_m == f and band % HALO == 0, (f, band)
    vmem = (2 * tm * d * 2 + rows * d * 2 + 4 * d * tn * 4 + 2 * d * tn * 2 + 2 * band * d_out * 6
            + 2 * tm * tn * 2 + 8 * rows * tn * 4 + (4 << 20))
    return pl.pallas_call(
        functools.partial(_ffn_up_kernel, tm=tm, n_m=n_m, row_pieces=row_pieces),
        grid=(n_n, b, n_m),
        in_specs=[
            pl.BlockSpec((1, tm, d), lambda n, bi, m: (bi, m, 0)),
            pl.BlockSpec((1, HALO, d), lambda n, bi, m: (bi, jnp.maximum(m * hb - 1, 0), 0)),
            pl.BlockSpec((1, HALO, d), lambda n, bi, m: (bi, jnp.minimum((m + 1) * hb, l // HALO - 1), 0)),
            col(d, 0), col(d, n_n), col(3, 0), col(3, n_n), col(1, 0), col(1, n_n),
            pl.BlockSpec((None, band, d_out), lambda n, bi, m: (layer, (n * b + bi) * n_m + m, 0)),
        ],
        out_specs=[pl.BlockSpec((1, tm, tn), lambda n, bi, m: (bi, m, n)),
                   pl.BlockSpec((band, d_out), lambda n, bi, m: ((n * b + bi) * n_m + m, 0))],
        out_shape=[jax.ShapeDtypeStruct((b, l, f), BF16), jax.ShapeDtypeStruct((f, d_out), BF16)],
        scratch_shapes=[pltpu.VMEM((rows, d), BF16), pltpu.VMEM((d, tn), BF16), pltpu.VMEM((d, tn), BF16)],
        compiler_params=_cparams(("arbitrary", "arbitrary", "arbitrary"), vmem),
        name="ffn_up",
    )(h, h, h, w_up, w_up, conv_w, conv_w, conv_b, conv_b, w_down)


def _ffn_down_kernel(a_ref, w_ref, x_ref, mod_ref, o_ref):
    y = _dot(a_ref[0], w_ref[...])
    o_ref[0] = x_ref[0] + mod_ref[0, GATE2:GATE2 + 1, :] * y


def _ffn_down(act, w_bf16, x, mod, mod_row, n_col_tiles=None):
    b, l, d = x.shape
    tm = _tile("ffn_down_tm", l)
    tn = _tile("ffn_down_tn", d)
    f = act.shape[2]
    if n_col_tiles is None:
        n_col_tiles = d // tn
    vmem = 2 * tm * f * 2 + 2 * f * tn * 2 + 5 * tm * tn * 4 + (6 << 20)
    return pl.pallas_call(
        _ffn_down_kernel,
        grid=(n_col_tiles, b, l // tm),
        in_specs=[
            pl.BlockSpec((1, tm, f), lambda n, bi, m: (bi, m, 0)),
            pl.BlockSpec((f, tn), lambda n, bi, m: (0, n)),
            pl.BlockSpec((1, tm, tn), lambda n, bi, m: (bi, m, n)),
            pl.BlockSpec((1, 6, tn), _mod_index(mod_row, 1, col_axis=0)),
        ],
        out_specs=pl.BlockSpec((1, tm, tn), lambda n, bi, m: (bi, m, n)),
        out_shape=jax.ShapeDtypeStruct((b, l, n_col_tiles * tn), F32),
        compiler_params=_cparams(("parallel", "parallel", "parallel"), vmem),
        name="ffn_down",
    )(act, w_bf16, x, mod)


def _ffn_down_norm_kernel(a_ref, w_ref, x_ref, left_ref, mod_ref, g_ref, o_ref):
    y = _dot(a_ref[0], w_ref[...])
    right = x_ref[0] + mod_ref[0, GATE2:GATE2 + 1, :] * y
    left = left_ref[0]
    wl = left.shape[1]
    d = wl + right.shape[1]
    ms = (jnp.sum(left * left, axis=-1, keepdims=True) + jnp.sum(right * right, axis=-1, keepdims=True)) / d
    r = lax.rsqrt(ms + EPS)
    o_ref[0, :, :wl] = left * r * g_ref[:, :wl]
    o_ref[0, :, wl:] = right * r * g_ref[:, wl:]


def _ffn_down_norm(act, w_bf16, x, left, mod, mod_row, g):
    b, l, d = x.shape
    tm = _tile("ffn_down_tm", l)
    tn = _tile("ffn_down_tn", d)
    f = act.shape[2]
    last = d // tn - 1
    vmem = 2 * tm * f * 2 + 2 * f * tn * 2 + 6 * tm * d * 4 + (6 << 20)
    return pl.pallas_call(
        _ffn_down_norm_kernel,
        grid=(b, l // tm),
        in_specs=[
            pl.BlockSpec((1, tm, f), lambda bi, m: (bi, m, 0)),
            pl.BlockSpec((f, tn), lambda bi, m: (0, last)),
            pl.BlockSpec((1, tm, tn), lambda bi, m: (bi, m, last)),
            pl.BlockSpec((1, tm, d - tn), lambda bi, m: (bi, m, 0)),
            pl.BlockSpec((1, 6, tn), lambda bi, m: (bi if mod_row is None else mod_row, 0, last)),
            pl.BlockSpec((1, d), lambda bi, m: (0, 0)),
        ],
        out_specs=pl.BlockSpec((1, tm, d), lambda bi, m: (bi, m, 0)),
        out_shape=jax.ShapeDtypeStruct((b, l, d), F32),
        compiler_params=_cparams(("parallel", "parallel"), vmem),
        name="ffn_down_norm",
    )(act, w_bf16, x, left, mod, g)


def _pool_kernel(xm_ref, xp_ref, xn_ref, mod_ref, g_ref, w_ref, ps_ref, g2_ref, x_out, h_out, x1_scr,
                 *, tm, n_m, seq_len):
    m = pl.program_id(1)
    g = g_ref[...]
    shift = mod_ref[0, SHIFT1:SHIFT1 + 1, :]
    scale = mod_ref[0, SCALE1:SCALE1 + 1, :]
    xm = xm_ref[0]
    hp = _norm_mod(xp_ref[0], g, shift, scale) * (m > 0).astype(F32)
    hn = _norm_mod(xn_ref[0], g, shift, scale) * (m < n_m - 1).astype(F32)
    hh = jnp.concatenate([hp, _norm_mod(xm, g, shift, scale), hn], axis=0)
    rows = tm + 2 * HALO
    t = (m * tm + lax.broadcasted_iota(jnp.int32, (tm, 1), 0)).astype(F32)
    gw = hh.shape[1] // len(POOL_WINDOWS)
    gate = mod_ref[0, GATE1:GATE1 + 1, :]
    for gi, w in enumerate(POOL_WINDOWS):
        sl = slice(gi * gw, (gi + 1) * gw)
        hg = hh[:, sl]
        acc = hg
        span = 1
        while span < w:
            acc = acc + pltpu.roll(acc, span, 0)
            span *= 2
        ahead = w // 2 - 1
        if ahead:
            acc = pltpu.roll(acc, rows - ahead, 0)
        win = acc[HALO:HALO + tm]
        cnt = jnp.minimum(t + (w // 2 - 1), seq_len - 1.0) - jnp.maximum(t - w // 2, 0.0) + 1.0
        p = (win / cnt - hg[HALO:HALO + tm]).astype(BF16)
        y = _dot(p, w_ref[gi].astype(BF16)) * ps_ref[:, sl]
        x1_scr[:, sl] = xm[:, sl] + gate[:, sl] * y
    x1 = x1_scr[...]
    x_out[0] = x1
    h_out[0] = _norm_mod(x1, g2_ref[...], mod_ref[0, SHIFT2:SHIFT2 + 1, :],
                         mod_ref[0, SCALE2:SCALE2 + 1, :]).astype(BF16)


def _pool_mix(x, mod, mod_row, g, pool_w, pool_scale, g2):
    b, l, d = x.shape
    tm = _tile("pool_tm", l)
    n_m = l // tm
    hb = tm // HALO
    ng, gw, _ = pool_w.shape
    row = pl.BlockSpec((1, tm, d), lambda bi, m: (bi, m, 0))
    vec = pl.BlockSpec((1, d), lambda bi, m: (0, 0))
    vmem = 7 * tm * d * 4 + 2 * ng * gw * gw * 4 + 8 * (tm + 2 * HALO) * d * 4 + (6 << 20)
    return pl.pallas_call(
        functools.partial(_pool_kernel, tm=tm, n_m=n_m, seq_len=l),
        grid=(b, n_m),
        in_specs=[
            row,
            pl.BlockSpec((1, HALO, d), lambda bi, m: (bi, jnp.maximum(m * hb - 1, 0), 0)),
            pl.BlockSpec((1, HALO, d), lambda bi, m: (bi, jnp.minimum((m + 1) * hb, l // HALO - 1), 0)),
            pl.BlockSpec((1, 6, d), _mod_index(mod_row, 0)),
            vec,
            pl.BlockSpec((ng, gw, gw), lambda bi, m: (0, 0, 0)),
            vec, vec,
        ],
        out_specs=[row, row],
        out_shape=[jax.ShapeDtypeStruct((b, l, d), F32), jax.ShapeDtypeStruct((b, l, d), BF16)],
        scratch_shapes=[pltpu.VMEM((tm, d), F32)],
        compiler_params=_cparams(("parallel", "parallel"), vmem),
        name="pool_mix",
    )(x, x, x, mod, g, pool_w, pool_scale, g2)


def _final_norm_kernel(x_ref, g_ref, o_ref):
    x = x_ref[0]
    o_ref[0] = x * lax.rsqrt(jnp.mean(x * x, axis=-1, keepdims=True) + EPS) * g_ref[...]


def _final_norm(x, g):
    b, l, d = x.shape
    tm = _tile("final_tm", l)
    return pl.pallas_call(
        _final_norm_kernel,
        grid=(b, l // tm),
        in_specs=[pl.BlockSpec((1, tm, d), lambda bi, m: (bi, m, 0)), pl.BlockSpec((1, d), lambda bi, m: (0, 0))],
        out_specs=pl.BlockSpec((1, tm, d), lambda bi, m: (bi, m, 0)),
        out_shape=jax.ShapeDtypeStruct((b, l, d), F32),
        compiler_params=_cparams(("parallel", "parallel"), 6 * tm * d * 4 + (4 << 20)),
        name="final_norm",
    )(x, g)


def _axial_angles(rows, dim):
    row = jnp.broadcast_to(jnp.arange(rows, dtype=F32)[:, None], (rows, GRID_W)).reshape(-1)
    col = jnp.broadcast_to(jnp.arange(GRID_W, dtype=F32)[None, :], (rows, GRID_W)).reshape(-1)
    n_freq = dim // 4
    inv = ROPE_BASE ** (-jnp.arange(n_freq, dtype=F32) / n_freq)
    return jnp.concatenate([row[:, None] * inv, col[:, None] * inv], axis=-1)


def _rope_tables(rows, dim):
    ang = _axial_angles(rows, dim)
    cos, sin = jnp.cos(ang), jnp.sin(ang)
    cos_t = jnp.concatenate([cos, cos], axis=-1)
    sin_t = jnp.concatenate([-sin, sin], axis=-1)
    pad = LANES - dim
    if pad:
        cos_t = jnp.pad(cos_t, ((0, 0), (0, pad)))
        sin_t = jnp.pad(sin_t, ((0, 0), (0, pad)))
    return cos_t, sin_t


def _mla_weights(g_q, w_uq, g_kv, w_ukv, heads, ret_w):
    q_rank, kv_rank = g_q.shape[0], g_kv.shape[0]
    wq = w_uq.reshape(q_rank, heads, HEAD_DIM + MLA_DR)
    wq = jnp.pad(wq, ((0, 0), (0, 0), (0, HEAD_DIM - MLA_DR))).reshape(q_rank, heads * 2 * HEAD_DIM).astype(BF16)
    wkv = w_ukv.reshape(kv_rank, heads, 2 * HEAD_DIM)
    wkn = wkv[:, :, :HEAD_DIM].reshape(kv_rank, ret_w).astype(BF16)
    wv = wkv[:, :, HEAD_DIM:].reshape(kv_rank, ret_w).astype(BF16)
    return g_q[None, :], wq, g_kv[None, :], wkn, wv


def _conv_ffn(h, x, mod, mod_row, w_up, conv_w, conv_b, w_down, layer, final_g=None):
    act, w_down_bf16 = _ffn_up(h, w_up, conv_w, conv_b[:, None, :], w_down, layer)
    d = x.shape[2]
    n_col_tiles = d // _tile("ffn_down_tn", d)
    if final_g is None:
        return _ffn_down(act, w_down_bf16, x, mod, mod_row)
    if n_col_tiles == 1:
        return _final_norm(_ffn_down(act, w_down_bf16, x, mod, mod_row), final_g)
    left = _ffn_down(act, w_down_bf16, x, mod, mod_row, n_col_tiles - 1)
    return _ffn_down_norm(act, w_down_bf16, x, left, mod, mod_row, final_g)


def kernel(x, c, ctx, c_ctx, ada_w, ada_b, norm1_g, norm2_g, ffn_w_up, ffn_conv_w, ffn_conv_b, ffn_w_down, mix_w_in, mla_q_norm_g, mla_w_uq, mla_kv_norm_g, mla_w_ukv, ret_decay_f, ret_decay_b, mix_w_out, pool_w, pool_scale, final_g):
    b, l, d = x.shape
    lc = ctx.shape[1]
    depth = ada_w.shape[0]
    heads = ret_decay_f.shape[1]
    ret_w = heads * HEAD_DIM
    rows = l // GRID_W
    rope_ret = _rope_tables(rows, HEAD_DIM)
    rope_mla = _rope_tables(rows, MLA_DR)

    ctx_row = b
    cc = jnp.concatenate([c, c_ctx[None, :], jnp.zeros((-(b + 1) % 8, d), F32)], axis=0)
    mods = _ada(cc, ada_w, ada_b).reshape(depth, cc.shape[0], 6, d)

    x_lat, x_ctx = x, ctx
    for layer in range(depth):
        j = layer // 2
        with_ctx = layer < depth - 1
        mod = mods[layer]
        g1 = norm1_g[layer][None, :]
        g2 = norm2_g[layer][None, :]
        h_ctx = None
        if layer % 2 == 0:
            w_in_t = jnp.swapaxes(mix_w_in[j], 0, 1)
            mla_w = _mla_weights(mla_q_norm_g[j], mla_w_uq[j], mla_kv_norm_g[j], mla_w_ukv[j], heads, ret_w)
            dec = jnp.stack([ret_decay_f[j], ret_decay_b[j]]).astype(F32)

            h1_lat, q_lat, k_lat, v_lat = _inproj_mla(x_lat, mod, None, g1, w_in_t, 4 * ret_w, *mla_w, rope_mla, heads)
            h1_ctx, q_ctx, k_ctx, v_ctx = _inproj_mla(x_ctx.reshape(1, b * lc, d), mod, ctx_row, g1, w_in_t,
                                                      4 * ret_w, *mla_w, None, heads)
            q_ctx, k_ctx, v_ctx = [a.reshape(b, lc, -1) for a in (q_ctx, k_ctx, v_ctx)]
            ret_lat, w_out_bf16 = _inproj_ret(h1_lat, w_in_t, rope_ret, ret_w, cast_extra=mix_w_out[j])
            ret_ctx = _inproj_ret(h1_ctx, w_in_t, None, ret_w).reshape(b, lc, 4 * ret_w)

            o_ret_lat, o_ret_ctx = _retention(dec, ret_lat, ret_ctx, heads)
            o_mla_lat = _attention(q_lat, k_ctx, v_ctx, k_lat, v_lat, heads)
            x_lat, h_lat = _mix_out(o_ret_lat, o_mla_lat, w_out_bf16, x_lat, mod, None, g2)
            if with_ctx:
                o_mla_ctx = _attention(q_ctx, k_ctx, v_ctx, None, None, heads)
                x_ctx, h_ctx = _mix_out(o_ret_ctx, o_mla_ctx, w_out_bf16, x_ctx, mod, ctx_row, g2)
        else:
            ps = pool_scale[j][None, :]
            if with_ctx:
                x_ctx, h_ctx = _pool_mix(x_ctx, mod, ctx_row, g1, pool_w[j], ps, g2)
            x_lat, h_lat = _pool_mix(x_lat, mod, None, g1, pool_w[j], ps, g2)
        last = layer == depth - 1
        x_lat = _conv_ffn(h_lat, x_lat, mod, None, ffn_w_up, ffn_conv_w, ffn_conv_b, ffn_w_down, layer,
                          final_g=final_g[None, :] if last else None)
        if with_ctx:
            x_ctx = _conv_ffn(h_ctx, x_ctx, mod, ctx_row, ffn_w_up, ffn_conv_w, ffn_conv_b, ffn_w_down, layer)
    return x_lat
```

```python
import functools

import jax
import jax.numpy as jnp
from jax import lax
from jax.experimental import pallas as pl
from jax.experimental.pallas import tpu as pltpu

F32 = jnp.float32
BF16 = jnp.bfloat16

GRID_W = 64
HEAD_DIM = 128
RET_CHUNK = 128
MLA_DR = 64
POOL_WINDOWS = (2, 4, 8, 16)
ROPE_BASE = 10000.0
EPS = 1e-6
LOG2_E = 1.4426950408889634

LANES = 128
MXU_DIM = 256
SUBLANES = 8
BF16_TILE_ROWS = 16
HALO = 16
V7X_VMEM_BYTES = 64 * 1024 * 1024
VMEM_CAP = V7X_VMEM_BYTES - 8 * 1024 * 1024

SHIFT1, SCALE1, GATE1, SHIFT2, SCALE2, GATE2 = range(6)

TILES = dict(
    inproj_ret_tm=1024, inproj_ret_piece_rows=256,
    inproj_mla_tm=512, inproj_mla_piece_rows=256,
    attn_tq=512, attn_heads_per_step=2,
    mix_out_tm=512, mix_out_piece_rows=256,
    ffn_up_tn=256, ffn_up_piece_rows=128, ffn_up_max_rows=2048,
    ffn_down_tm=512, ffn_down_tn=1024,
    pool_tm=512,
    final_tm=512,
    ada_tn=1024,
)


def _tile(name, extent):
    tile = min(extent, TILES[name])
    assert extent % tile == 0, f"{name}: extent {extent} is not a multiple of its tile {tile}"
    return tile


def _cparams(sem, vmem_bytes, flags=None):
    return pltpu.CompilerParams(dimension_semantics=sem, vmem_limit_bytes=min(int(vmem_bytes), VMEM_CAP),
                                flags=flags)


def _silu(x):
    return x / (1.0 + jnp.exp(-x))


def _norm_mod(x, g, shift, scale):
    y = x * lax.rsqrt(jnp.mean(x * x, axis=-1, keepdims=True) + EPS) * g
    return y * (1.0 + scale) + shift


def _dot(a, b):
    return jnp.dot(a, b, preferred_element_type=F32)


def _dot_nt(a, b):
    return lax.dot_general(a, b, (((1,), (1,)), ((), ())), preferred_element_type=F32)


def _dot_tn(a, b):
    return lax.dot_general(a, b, (((0,), (0,)), ((), ())), preferred_element_type=F32)


def _mod_index(mod_row, batch_axis, col_axis=None):
    def index(*ids):
        row = ids[batch_axis] if mod_row is None else mod_row
        return (row, 0, 0 if col_axis is None else ids[col_axis])
    return index


def _ada_kernel(c_ref, w_ref, b_ref, o_ref):
    s = _silu(c_ref[...]).astype(BF16)
    o_ref[0] = _dot(s, w_ref[0].astype(BF16)) + b_ref[0]


def _ada(cc, ada_w, ada_b):
    depth, d, n6 = ada_w.shape
    rows = cc.shape[0]
    tn = _tile("ada_tn", n6)
    return pl.pallas_call(
        _ada_kernel,
        grid=(depth, n6 // tn),
        in_specs=[
            pl.BlockSpec((rows, d), lambda l, n: (0, 0)),
            pl.BlockSpec((1, d, tn), lambda l, n: (l, 0, n)),
            pl.BlockSpec((1, 1, tn), lambda l, n: (l, 0, n)),
        ],
        out_specs=pl.BlockSpec((1, rows, tn), lambda l, n: (l, 0, n)),
        out_shape=jax.ShapeDtypeStruct((depth, rows, n6), F32),
        compiler_params=_cparams(("parallel", "parallel"), 2 * d * tn * 4 + d * tn * 2 + (8 << 20)),
        name="ada_mod",
    )(cc, ada_w, ada_b.reshape(depth, 1, n6))


def _inproj_ret_kernel(*refs, rope, cast_extra, heads_per_tile, k_scale, row_pieces):
    refs = list(refs)
    h_ref = refs.pop(0)
    if rope:
        cos_ref, sin_ref = refs.pop(0), refs.pop(0)
    w_ref = refs.pop(0)
    if cast_extra:
        extra_ref = refs.pop(0)
    o_ref = refs.pop(0)
    if cast_extra:
        extra_out = refs.pop(0)
    (w_scr,) = refs
    n = pl.program_id(0)

    @pl.when((pl.program_id(1) == 0) & (pl.program_id(2) == 0))
    def _():
        w_scr[...] = w_ref[...].astype(BF16)

    if cast_extra:
        extra_out[...] = extra_ref[...].astype(BF16)
    scale = jnp.where(n == 1, k_scale, 1.0).astype(F32)
    if rope:
        rotated = n < 2
        cos = jnp.where(rotated, cos_ref[...], 1.0) * scale
        sin = jnp.where(rotated, sin_ref[...], 0.0) * scale
    pr = h_ref.shape[1] // row_pieces
    for i in range(row_pieces):
        rs = slice(i * pr, (i + 1) * pr)
        z = _dot_nt(h_ref[0, rs, :], w_scr[...])
        if rope:
            for h in range(heads_per_tile):
                zh = z[:, h * HEAD_DIM:(h + 1) * HEAD_DIM]
                r = zh * cos[rs] + pltpu.roll(zh, HEAD_DIM // 2, 1) * sin[rs]
                o_ref[0, rs, h * HEAD_DIM:(h + 1) * HEAD_DIM] = r.astype(BF16)
        else:
            o_ref[0, rs, :] = (z * scale).astype(BF16)


def _inproj_ret(h, w_in_t, rope_tabs, ret_w, cast_extra=None):
    b, l, d = h.shape
    tm = _tile("inproj_ret_tm", l)
    tn = ret_w
    n_tiles = 4
    n_m = l // tm
    rope = rope_tabs is not None
    in_specs = [pl.BlockSpec((1, tm, d), lambda n, bi, m: (bi, m, 0))]
    args = [h]
    if rope:
        in_specs += [pl.BlockSpec((tm, HEAD_DIM), lambda n, bi, m: (m, 0))] * 2
        args += list(rope_tabs)
    in_specs.append(pl.BlockSpec((tn, d), lambda n, bi, m: (n, 0)))
    args.append(w_in_t)
    out_specs = [pl.BlockSpec((1, tm, tn), lambda n, bi, m: (bi, m, n))]
    out_shape = [jax.ShapeDtypeStruct((b, l, n_tiles * tn), BF16)]
    vmem = 2 * tm * d * 2 + 2 * d * tn * 4 + d * tn * 2 + 2 * tm * tn * 2 + 4 * tm * tn * 4 + (6 << 20)
    if cast_extra is not None:
        er, ec = cast_extra.shape
        band = er // (n_tiles * b * n_m)
        assert band * n_tiles * b * n_m == er and band % BF16_TILE_ROWS == 0, (er, band)
        extra_spec = pl.BlockSpec((band, ec), lambda n, bi, m: ((n * b + bi) * n_m + m, 0))
        in_specs.append(extra_spec)
        args.append(cast_extra)
        out_specs.append(extra_spec)
        out_shape.append(jax.ShapeDtypeStruct((er, ec), BF16))
        vmem += 2 * band * ec * 6
    outs = pl.pallas_call(
        functools.partial(_inproj_ret_kernel, rope=rope, cast_extra=cast_extra is not None,
                          heads_per_tile=tn // HEAD_DIM, k_scale=HEAD_DIM ** -0.5,
                          row_pieces=max(1, tm // TILES["inproj_ret_piece_rows"])),
        grid=(n_tiles, b, l // tm),
        in_specs=in_specs,
        out_specs=out_specs,
        out_shape=out_shape,
        scratch_shapes=[pltpu.VMEM((tn, d), BF16)],
        compiler_params=_cparams(("arbitrary", "arbitrary", "arbitrary"), vmem),
        name="inproj_ret_rope" if rope else "inproj_ret",
    )(*args)
    return outs if cast_extra is not None else outs[0]


def _rope_half_padded(x, cos, sin):
    rot = pltpu.roll(x, MLA_DR // 2, 1) + pltpu.roll(x, LANES - MLA_DR // 2, 1)
    return x * cos + rot * sin


def _inproj_mla_kernel(*refs, rope, heads, q_rank, kv_rank, q_scale, row_pieces):
    refs = list(refs)
    x_ref, mod_ref, g_ref = refs[:3]
    del refs[:3]
    if rope:
        cos_ref, sin_ref = refs[:2]
        del refs[:2]
    (wcq_ref, wckv_ref, wkr_ref, gq_ref, wq_ref, gkv_ref, wkn_ref, wv_ref,
     h_out, q_out, k_out, v_out, wt_scr) = refs

    @pl.when((pl.program_id(0) == 0) & (pl.program_id(1) == 0))
    def _():
        wt_scr[0:q_rank, :] = wcq_ref[...].astype(BF16)
        wt_scr[q_rank:q_rank + kv_rank, :] = wckv_ref[...].astype(BF16)
        wt_scr[q_rank + kv_rank:q_rank + kv_rank + MLA_DR, :] = wkr_ref[...].astype(BF16)
        wt_scr[q_rank + kv_rank + MLA_DR:, :] = jnp.zeros((LANES - MLA_DR, wt_scr.shape[1]), BF16)

    pr = x_ref.shape[1] // row_pieces
    for i in range(row_pieces):
        rs = slice(i * pr, (i + 1) * pr)
        h = _norm_mod(x_ref[0, rs, :], g_ref[...], mod_ref[0, SHIFT1:SHIFT1 + 1, :],
                      mod_ref[0, SCALE1:SCALE1 + 1, :]).astype(BF16)
        h_out[0, rs, :] = h
        zt = _dot_nt(h, wt_scr[...])
        cq = zt[:, :q_rank]
        ckv = zt[:, q_rank:q_rank + kv_rank]
        kr = zt[:, q_rank + kv_rank:]
        cqn = (cq * lax.rsqrt(jnp.mean(cq * cq, axis=-1, keepdims=True) + EPS) * gq_ref[...]).astype(BF16)
        ckvn = (ckv * lax.rsqrt(jnp.mean(ckv * ckv, axis=-1, keepdims=True) + EPS) * gkv_ref[...]).astype(BF16)
        q = _dot(cqn, wq_ref[...])
        kn = _dot(ckvn, wkn_ref[...])
        v_out[0, rs, :] = _dot(ckvn, wv_ref[...]).astype(BF16)
        if rope:
            cos = cos_ref[rs, :]
            sin = sin_ref[rs, :]
            kr = _rope_half_padded(kr, cos, sin)
        kr = kr.astype(BF16)
        for hd in range(heads):
            base = hd * 2 * HEAD_DIM
            qn = q[:, base:base + HEAD_DIM]
            qr = q[:, base + HEAD_DIM:base + 2 * HEAD_DIM]
            if rope:
                qr = _rope_half_padded(qr, cos, sin)
            q_out[0, rs, base:base + HEAD_DIM] = (qn * q_scale).astype(BF16)
            q_out[0, rs, base + HEAD_DIM:base + 2 * HEAD_DIM] = (qr * q_scale).astype(BF16)
            k_out[0, rs, base:base + HEAD_DIM] = kn[:, hd * HEAD_DIM:(hd + 1) * HEAD_DIM].astype(BF16)
            k_out[0, rs, base + HEAD_DIM:base + 2 * HEAD_DIM] = kr


def _inproj_mla(x, mod, mod_row, g, w_in_t, tail_row, gq, wq, gkv, wkn, wv, rope_tabs, heads):
    b, l, d = x.shape
    tm = _tile("inproj_mla_tm", l)
    q_rank, kv_rank = gq.shape[1], gkv.shape[1]
    rope = rope_tabs is not None
    full = lambda a: pl.BlockSpec(a.shape, lambda bi, m: (0,) * a.ndim)
    in_specs = [pl.BlockSpec((1, tm, d), lambda bi, m: (bi, m, 0)),
                pl.BlockSpec((1, 6, d), _mod_index(mod_row, 0)), full(g)]
    args = [x, mod, g]
    if rope:
        in_specs += [pl.BlockSpec((tm, LANES), lambda bi, m: (m, 0))] * 2
        args += list(rope_tabs)
    for start, rows in ((tail_row, q_rank), (tail_row + q_rank, kv_rank), (tail_row + q_rank + kv_rank, MLA_DR)):
        assert start % rows == 0, (start, rows)
        in_specs.append(pl.BlockSpec((rows, d), functools.partial(lambda blk, bi, m: (blk, 0), start // rows)))
        args.append(w_in_t)
    for a in (gq, wq, gkv, wkn, wv):
        in_specs.append(full(a))
        args.append(a)
    qw = heads * 2 * HEAD_DIM
    vw = heads * HEAD_DIM
    tail = q_rank + kv_rank + LANES
    out_spec = lambda w: pl.BlockSpec((1, tm, w), lambda bi, m: (bi, m, 0))
    wbytes = sum(a.size * a.dtype.itemsize for a in (wq, wkn, wv))
    vmem = (2 * tm * d * 4 + 2 * tail * d * 4 + tail * d * 2 + 2 * wbytes + 2 * tm * (d + 2 * qw + vw) * 2
            + tm * (3 * d * 4 + 6 * qw * 4) + (4 << 20))
    return pl.pallas_call(
        functools.partial(_inproj_mla_kernel, rope=rope, heads=heads, q_rank=q_rank, kv_rank=kv_rank,
                          q_scale=(HEAD_DIM + MLA_DR) ** -0.5 * LOG2_E,
                          row_pieces=max(1, tm // TILES["inproj_mla_piece_rows"])),
        grid=(b, l // tm),
        in_specs=in_specs,
        out_specs=[out_spec(d), out_spec(qw), out_spec(qw), out_spec(vw)],
        out_shape=[jax.ShapeDtypeStruct((b, l, d), BF16), jax.ShapeDtypeStruct((b, l, qw), BF16),
                   jax.ShapeDtypeStruct((b, l, qw), BF16), jax.ShapeDtypeStruct((b, l, vw), BF16)],
        scratch_shapes=[pltpu.VMEM((tail, d), BF16)],
        compiler_params=_cparams(("arbitrary", "arbitrary"), vmem),
        name="inproj_mla_rope" if rope else "inproj_mla",
    )(*args)


def _log_sigmoid(x):
    return jnp.minimum(x, 0.0) - jnp.log1p(jnp.exp(-jnp.abs(x)))


def _retention_kernel(dec_ref, ql_ref, kl_ref, vl_ref, gl_ref, qc_ref, kc_ref, vc_ref, gc_ref, ol_ref, oc_ref,
                      st_scr, *, n_ctx, n_lat, c):
    dk = HEAD_DIM
    hd = pl.program_id(1)
    lf = _log_sigmoid(jnp.full((c, c), dec_ref[0, hd], F32))
    lb = _log_sigmoid(jnp.full((c, c), dec_ref[1, hd], F32))
    ii = lax.broadcasted_iota(jnp.int32, (c, c), 0).astype(F32)
    jj = lax.broadcasted_iota(jnp.int32, (c, c), 1).astype(F32)
    diff = ii - jj
    fwd = diff >= 0
    decay = jnp.where(fwd, jnp.exp(jnp.where(fwd, diff, 0.0) * lf), jnp.exp(jnp.where(fwd, 0.0, -diff) * lb))
    ri = lax.broadcasted_iota(jnp.int32, (c, dk), 0).astype(F32)
    lfr = _log_sigmoid(jnp.full((c, dk), dec_ref[0, hd], F32))
    lbr = _log_sigmoid(jnp.full((c, dk), dec_ref[1, hd], F32))
    q_dec_f = jnp.exp((ri + 1.0) * lfr)
    k_dec_f = jnp.exp((c - 1.0 - ri) * lfr)
    q_dec_b = jnp.exp((c - ri) * lbr)
    k_dec_b = jnp.exp(ri * lbr)
    c_dec_f = jnp.exp(c * _log_sigmoid(jnp.full((dk, dk), dec_ref[0, hd], F32)))
    c_dec_b = jnp.exp(c * _log_sigmoid(jnp.full((dk, dk), dec_ref[1, hd], F32)))

    blocks = [(qc_ref, kc_ref, vc_ref, gc_ref, oc_ref, i) for i in range(n_ctx)]
    blocks += [(ql_ref, kl_ref, vl_ref, gl_ref, ol_ref, i) for i in range(n_lat)]

    def rows(ref, i):
        return ref[0, i * c:(i + 1) * c, :]

    for g, (_, k_ref, v_ref, _, _, i) in enumerate(blocks):
        kf = rows(k_ref, i).astype(F32)
        kk = jnp.concatenate([(kf * k_dec_f).astype(BF16), (kf * k_dec_b).astype(BF16)], axis=1)
        st_scr[g] = _dot_tn(kk, rows(v_ref, i))

    s = jnp.zeros((dk, dk), F32)
    for g in range(n_ctx + n_lat):
        kv = st_scr[g, :dk, :]
        st_scr[g, :dk, :] = s
        s = s * c_dec_f + kv
    s = jnp.zeros((dk, dk), F32)
    for g in list(range(n_ctx - 1, -1, -1)) + list(range(n_ctx + n_lat - 1, n_ctx - 1, -1)):
        kv = st_scr[g, dk:, :]
        st_scr[g, dk:, :] = s
        s = s * c_dec_b + kv

    for g, (q_ref, k_ref, v_ref, g_ref, o_ref, i) in enumerate(blocks):
        q = rows(q_ref, i)
        p = (_dot_nt(q, rows(k_ref, i)) * decay).astype(BF16)
        qf = q.astype(F32)
        qq = jnp.concatenate([(qf * q_dec_f).astype(BF16), (qf * q_dec_b).astype(BF16)], axis=1)
        o = _dot(p, rows(v_ref, i)) + _dot(qq, st_scr[g].astype(BF16))
        o = o * lax.rsqrt(jnp.mean(o * o, axis=-1, keepdims=True) + EPS)
        o_ref[0, i * c:(i + 1) * c, :] = (o * _silu(rows(g_ref, i).astype(F32))).astype(BF16)


def _retention(dec, ret_lat, ret_ctx, heads):
    b, l, _ = ret_lat.shape
    lc = ret_ctx.shape[1]
    c = MXU_DIM if (l % MXU_DIM == 0 and lc % MXU_DIM == 0) else RET_CHUNK
    n_lat, n_ctx = l // c, lc // c
    col = lambda rows, group: pl.BlockSpec((1, rows, HEAD_DIM), lambda bi, h: (bi, 0, group * heads + h))
    in_specs = [pl.BlockSpec(memory_space=pltpu.SMEM)]
    in_specs += [col(l, grp) for grp in range(4)] + [col(lc, grp) for grp in range(4)]
    out_col = lambda rows: pl.BlockSpec((1, rows, HEAD_DIM), lambda bi, h: (bi, 0, h))
    vmem = 2 * 5 * (l + lc) * HEAD_DIM * 2 + (n_lat + n_ctx) * 2 * HEAD_DIM * HEAD_DIM * 4 + (12 << 20)
    return pl.pallas_call(
        functools.partial(_retention_kernel, n_ctx=n_ctx, n_lat=n_lat, c=c),
        grid=(b, heads),
        in_specs=in_specs,
        out_specs=[out_col(l), out_col(lc)],
        out_shape=[jax.ShapeDtypeStruct((b, l, heads * HEAD_DIM), BF16),
                   jax.ShapeDtypeStruct((b, lc, heads * HEAD_DIM), BF16)],
        scratch_shapes=[pltpu.VMEM((n_lat + n_ctx, 2 * HEAD_DIM, HEAD_DIM), F32)],
        compiler_params=_cparams(("parallel", "parallel"), vmem),
        name="retention",
    )(dec, ret_lat, ret_lat, ret_lat, ret_lat, ret_ctx, ret_ctx, ret_ctx, ret_ctx)


def _attn_kernel(*refs, with_lat, tq):
    if with_lat:
        q_ref, kc_ref, vc_ref, kl_ref, vl_ref, o_ref = refs
    else:
        q_ref, kc_ref, vc_ref, o_ref = refs
    kw = 2 * HEAD_DIM
    for hp in range(o_ref.shape[2] // HEAD_DIM):
        kcols = slice(hp * kw, (hp + 1) * kw)
        vcols = slice(hp * HEAD_DIM, (hp + 1) * HEAD_DIM)
        for i in range(q_ref.shape[1] // tq):
            q = q_ref[0, i * tq:(i + 1) * tq, kcols]
            sc = _dot_nt(q, kc_ref[0, :, kcols])
            m = jnp.max(sc, axis=-1, keepdims=True)
            if with_lat:
                sl = _dot_nt(q, kl_ref[0, :, kcols])
                m = jnp.maximum(m, jnp.max(sl, axis=-1, keepdims=True))
            pc = jnp.exp2(sc - m)
            den = jnp.sum(pc, axis=-1, keepdims=True)
            o = _dot(pc.astype(BF16), vc_ref[0, :, vcols])
            if with_lat:
                p_lat = jnp.exp2(sl - m)
                den = den + jnp.sum(p_lat, axis=-1, keepdims=True)
                o = o + _dot(p_lat.astype(BF16), vl_ref[0, :, vcols])
            o_ref[0, i * tq:(i + 1) * tq, vcols] = (o / den).astype(BF16)


def _attention(q, k_ctx, v_ctx, k_lat, v_lat, heads):
    b, lq, _ = q.shape
    tq = _tile("attn_tq", lq)
    hps = _tile("attn_heads_per_step", heads)
    lc = k_ctx.shape[1]
    with_lat = k_lat is not None
    kw = 2 * HEAD_DIM
    head = lambda rows, w: pl.BlockSpec((1, rows, hps * w), lambda bi, h: (bi, 0, h))
    in_specs = [head(lq, kw), head(lc, kw), head(lc, HEAD_DIM)]
    args = [q, k_ctx, v_ctx]
    lk = lc
    if with_lat:
        ll = k_lat.shape[1]
        lk += ll
        in_specs += [head(ll, kw), head(ll, HEAD_DIM)]
        args += [k_lat, v_lat]
    vmem = 2 * hps * (lq * kw + lk * (kw + HEAD_DIM) + lq * HEAD_DIM) * 2 + 8 * tq * lk * 4 + (8 << 20)
    return pl.pallas_call(
        functools.partial(_attn_kernel, with_lat=with_lat, tq=tq),
        grid=(b, heads // hps),
        in_specs=in_specs,
        out_specs=head(lq, HEAD_DIM),
        out_shape=jax.ShapeDtypeStruct((b, lq, heads * HEAD_DIM), BF16),
        compiler_params=_cparams(("parallel", "parallel"), vmem),
        name="mla_attn" if with_lat else "mla_attn_ctx",
    )(*args)


def _mix_out_kernel(a1_ref, a2_ref, w_ref, x_ref, mod_ref, g2_ref, x_out, h_out, *, row_pieces):
    k1 = a1_ref.shape[2]
    pr = x_ref.shape[1] // row_pieces
    for i in range(row_pieces):
        rs = slice(i * pr, (i + 1) * pr)
        y = _dot(a1_ref[0, rs, :], w_ref[:k1, :]) + _dot(a2_ref[0, rs, :], w_ref[k1:, :])
        x1 = x_ref[0, rs, :] + mod_ref[0, GATE1:GATE1 + 1, :] * y
        x_out[0, rs, :] = x1
        h_out[0, rs, :] = _norm_mod(x1, g2_ref[...], mod_ref[0, SHIFT2:SHIFT2 + 1, :],
                                    mod_ref[0, SCALE2:SCALE2 + 1, :]).astype(BF16)


def _mix_out(a1, a2, w_bf16, x, mod, mod_row, g2):
    b, l, d = x.shape
    tm = _tile("mix_out_tm", l)
    k1, k2 = a1.shape[2], a2.shape[2]
    row = lambda w: pl.BlockSpec((1, tm, w), lambda bi, m: (bi, m, 0))
    vmem = (k1 + k2) * d * 2 + 2 * tm * (k1 + k2) * 2 + 4 * tm * d * 4 + 2 * tm * d * 2 + 6 * tm * d * 4 + (4 << 20)
    return pl.pallas_call(
        functools.partial(_mix_out_kernel, row_pieces=max(1, tm // TILES["mix_out_piece_rows"])),
        grid=(b, l // tm),
        in_specs=[row(k1), row(k2),
                  pl.BlockSpec((k1 + k2, d), lambda bi, m: (0, 0)),
                  row(d),
                  pl.BlockSpec((1, 6, d), _mod_index(mod_row, 0)),
                  pl.BlockSpec((1, d), lambda bi, m: (0, 0))],
        out_specs=[row(d), row(d)],
        out_shape=[jax.ShapeDtypeStruct((b, l, d), F32), jax.ShapeDtypeStruct((b, l, d), BF16)],
        compiler_params=_cparams(("parallel", "parallel"), vmem),
        name="mix_out",
    )(a1, a2, w_bf16, x, mod, g2)


def _ffn_up_kernel(h_ref, wa_ref, wg_ref, cwa_ref, cwg_ref, cba_ref, cbg_ref, wd_ref, o_ref, wd_out,
                   wa_scr, wg_scr, *, row_pieces):
    @pl.when(pl.program_id(1) == 0)
    def _():
        wa_scr[...] = wa_ref[...].astype(BF16)
        wg_scr[...] = wg_ref[...].astype(BF16)

    wd_out[...] = wd_ref[...].astype(BF16)

    l = h_ref.shape[1]
    pr = l // row_pieces
    sub = lax.broadcasted_iota(jnp.int32, (SUBLANES, 1), 0)

    def conv(w_scr, cw_ref, cb_ref):
        u = jnp.concatenate([_dot(h_ref[0, i * pr:(i + 1) * pr, :], w_scr[...]) for i in range(row_pieces)], axis=0)
        prev = pltpu.roll(u, 1, 0)
        nxt = pltpu.roll(u, l - 1, 0)
        prev = jnp.concatenate([jnp.where(sub == 0, 0.0, prev[:SUBLANES]), prev[SUBLANES:]], axis=0)
        nxt = jnp.concatenate([nxt[:-SUBLANES], jnp.where(sub == SUBLANES - 1, 0.0, nxt[-SUBLANES:])], axis=0)
        cw = cw_ref[...]
        return cw[0:1, :] * prev + cw[1:2, :] * u + cw[2:3, :] * nxt + cb_ref[...]

    a = conv(wa_scr, cwa_ref, cba_ref)
    gt = conv(wg_scr, cwg_ref, cbg_ref)
    o_ref[0] = (_silu(gt) * a).astype(BF16)


def _ffn_up(h, w_up, conv_w, conv_b, w_down, layer):
    b, l, d = h.shape
    d_out = w_down.shape[2]
    f = w_up.shape[2] // 2
    tn = _tile("ffn_up_tn", f)
    n_n = f // tn
    assert l <= TILES["ffn_up_max_rows"], "the row tile is the whole sequence"
    row_pieces = max(1, l // TILES["ffn_up_piece_rows"])
    assert l % (row_pieces * BF16_TILE_ROWS) == 0, (l, row_pieces)
    col = lambda rows, off: pl.BlockSpec((None, rows, tn), lambda n, bi: (layer, 0, n + off))
    band = f // (n_n * b)
    assert band * n_n * b == f and band % BF16_TILE_ROWS == 0, (f, band)
    vmem = (2 * l * d * 2 + 4 * d * tn * 4 + 2 * d * tn * 2 + 2 * band * d_out * 6
            + 2 * l * tn * 2 + 10 * l * tn * 4 + (4 << 20))
    return pl.pallas_call(
        functools.partial(_ffn_up_kernel, row_pieces=row_pieces),
        grid=(n_n, b),
        in_specs=[
            pl.BlockSpec((1, l, d), lambda n, bi: (bi, 0, 0)),
            col(d, 0), col(d, n_n), col(3, 0), col(3, n_n), col(1, 0), col(1, n_n),
            pl.BlockSpec((None, band, d_out), lambda n, bi: (layer, n * b + bi, 0)),
        ],
        out_specs=[pl.BlockSpec((1, l, tn), lambda n, bi: (bi, 0, n)),
                   pl.BlockSpec((band, d_out), lambda n, bi: (n * b + bi, 0))],
        out_shape=[jax.ShapeDtypeStruct((b, l, f), BF16), jax.ShapeDtypeStruct((f, d_out), BF16)],
        scratch_shapes=[pltpu.VMEM((d, tn), BF16), pltpu.VMEM((d, tn), BF16)],
        compiler_params=_cparams(("arbitrary", "arbitrary"), vmem),
        name="ffn_up",
    )(h, w_up, w_up, conv_w, conv_w, conv_b, conv_b, w_down)


def _ffn_down_kernel(a_ref, w_ref, x_ref, mod_ref, o_ref):
    y = _dot(a_ref[0], w_ref[...])
    o_ref[0] = x_ref[0] + mod_ref[0, GATE2:GATE2 + 1, :] * y


def _ffn_down(act, w_bf16, x, mod, mod_row, n_col_tiles=None):
    b, l, d = x.shape
    tm = _tile("ffn_down_tm", l)
    tn = _tile("ffn_down_tn", d)
    f = act.shape[2]
    if n_col_tiles is None:
        n_col_tiles = d // tn
    vmem = 2 * tm * f * 2 + 2 * f * tn * 2 + 5 * tm * tn * 4 + (6 << 20)
    return pl.pallas_call(
        _ffn_down_kernel,
        grid=(n_col_tiles, b, l // tm),
        in_specs=[
            pl.BlockSpec((1, tm, f), lambda n, bi, m: (bi, m, 0)),
            pl.BlockSpec((f, tn), lambda n, bi, m: (0, n)),
            pl.BlockSpec((1, tm, tn), lambda n, bi, m: (bi, m, n)),
            pl.BlockSpec((1, 6, tn), _mod_index(mod_row, 1, col_axis=0)),
        ],
        out_specs=pl.BlockSpec((1, tm, tn), lambda n, bi, m: (bi, m, n)),
        out_shape=jax.ShapeDtypeStruct((b, l, n_col_tiles * tn), F32),
        compiler_params=_cparams(("parallel", "parallel", "parallel"), vmem),
        name="ffn_down",
    )(act, w_bf16, x, mod)


def _ffn_down_norm_kernel(a_ref, w_ref, x_ref, left_ref, mod_ref, g_ref, o_ref):
    y = _dot(a_ref[0], w_ref[...])
    right = x_ref[0] + mod_ref[0, GATE2:GATE2 + 1, :] * y
    left = left_ref[0]
    wl = left.shape[1]
    d = wl + right.shape[1]
    ms = (jnp.sum(left * left, axis=-1, keepdims=True) + jnp.sum(right * right, axis=-1, keepdims=True)) / d
    r = lax.rsqrt(ms + EPS)
    o_ref[0, :, :wl] = left * r * g_ref[:, :wl]
    o_ref[0, :, wl:] = right * r * g_ref[:, wl:]


def _ffn_down_norm(act, w_bf16, x, left, mod, mod_row, g):
    b, l, d = x.shape
    tm = _tile("ffn_down_tm", l)
    tn = _tile("ffn_down_tn", d)
    f = act.shape[2]
    last = d // tn - 1
    vmem = 2 * tm * f * 2 + 2 * f * tn * 2 + 6 * tm * d * 4 + (6 << 20)
    return pl.pallas_call(
        _ffn_down_norm_kernel,
        grid=(b, l // tm),
        in_specs=[
            pl.BlockSpec((1, tm, f), lambda bi, m: (bi, m, 0)),
            pl.BlockSpec((f, tn), lambda bi, m: (0, last)),
            pl.BlockSpec((1, tm, tn), lambda bi, m: (bi, m, last)),
            pl.BlockSpec((1, tm, d - tn), lambda bi, m: (bi, m, 0)),
            pl.BlockSpec((1, 6, tn), lambda bi, m: (bi if mod_row is None else mod_row, 0, last)),
            pl.BlockSpec((1, d), lambda bi, m: (0, 0)),
        ],
        out_specs=pl.BlockSpec((1, tm, d), lambda bi, m: (bi, m, 0)),
        out_shape=jax.ShapeDtypeStruct((b, l, d), F32),
        compiler_params=_cparams(("parallel", "parallel"), vmem),
        name="ffn_down_norm",
    )(act, w_bf16, x, left, mod, g)


def _pool_kernel(xm_ref, xp_ref, xn_ref, mod_ref, g_ref, w_ref, ps_ref, g2_ref, x_out, h_out, x1_scr,
                 *, tm, n_m, seq_len):
    m = pl.program_id(1)
    g = g_ref[...]
    shift = mod_ref[0, SHIFT1:SHIFT1 + 1, :]
    scale = mod_ref[0, SCALE1:SCALE1 + 1, :]
    xm = xm_ref[0]
    hp = _norm_mod(xp_ref[0], g, shift, scale) * (m > 0).astype(F32)
    hn = _norm_mod(xn_ref[0], g, shift, scale) * (m < n_m - 1).astype(F32)
    hh = jnp.concatenate([hp, _norm_mod(xm, g, shift, scale), hn], axis=0)
    rows = tm + 2 * HALO
    t = (m * tm + lax.broadcasted_iota(jnp.int32, (tm, 1), 0)).astype(F32)
    gw = hh.shape[1] // len(POOL_WINDOWS)
    gate = mod_ref[0, GATE1:GATE1 + 1, :]
    for gi, w in enumerate(POOL_WINDOWS):
        sl = slice(gi * gw, (gi + 1) * gw)
        hg = hh[:, sl]
        acc = hg
        span = 1
        while span < w:
            acc = acc + pltpu.roll(acc, span, 0)
            span *= 2
        ahead = w // 2 - 1
        if ahead:
            acc = pltpu.roll(acc, rows - ahead, 0)
        win = acc[HALO:HALO + tm]
        cnt = jnp.minimum(t + (w // 2 - 1), seq_len - 1.0) - jnp.maximum(t - w // 2, 0.0) + 1.0
        p = (win / cnt - hg[HALO:HALO + tm]).astype(BF16)
        y = _dot(p, w_ref[gi].astype(BF16)) * ps_ref[:, sl]
        x1_scr[:, sl] = xm[:, sl] + gate[:, sl] * y
    x1 = x1_scr[...]
    x_out[0] = x1
    h_out[0] = _norm_mod(x1, g2_ref[...], mod_ref[0, SHIFT2:SHIFT2 + 1, :],
                         mod_ref[0, SCALE2:SCALE2 + 1, :]).astype(BF16)


def _pool_mix(x, mod, mod_row, g, pool_w, pool_scale, g2):
    b, l, d = x.shape
    tm = _tile("pool_tm", l)
    n_m = l // tm
    hb = tm // HALO
    ng, gw, _ = pool_w.shape
    row = pl.BlockSpec((1, tm, d), lambda bi, m: (bi, m, 0))
    vec = pl.BlockSpec((1, d), lambda bi, m: (0, 0))
    vmem = 7 * tm * d * 4 + 2 * ng * gw * gw * 4 + 8 * (tm + 2 * HALO) * d * 4 + (6 << 20)
    return pl.pallas_call(
        functools.partial(_pool_kernel, tm=tm, n_m=n_m, seq_len=l),
        grid=(b, n_m),
        in_specs=[
            row,
            pl.BlockSpec((1, HALO, d), lambda bi, m: (bi, jnp.maximum(m * hb - 1, 0), 0)),
            pl.BlockSpec((1, HALO, d), lambda bi, m: (bi, jnp.minimum((m + 1) * hb, l // HALO - 1), 0)),
            pl.BlockSpec((1, 6, d), _mod_index(mod_row, 0)),
            vec,
            pl.BlockSpec((ng, gw, gw), lambda bi, m: (0, 0, 0)),
            vec, vec,
        ],
        out_specs=[row, row],
        out_shape=[jax.ShapeDtypeStruct((b, l, d), F32), jax.ShapeDtypeStruct((b, l, d), BF16)],
        scratch_shapes=[pltpu.VMEM((tm, d), F32)],
        compiler_params=_cparams(("parallel", "parallel"), vmem),
        name="pool_mix",
    )(x, x, x, mod, g, pool_w, pool_scale, g2)


def _final_norm_kernel(x_ref, g_ref, o_ref):
    x = x_ref[0]
    o_ref[0] = x * lax.rsqrt(jnp.mean(x * x, axis=-1, keepdims=True) + EPS) * g_ref[...]


def _final_norm(x, g):
    b, l, d = x.shape
    tm = _tile("final_tm", l)
    return pl.pallas_call(
        _final_norm_kernel,
        grid=(b, l // tm),
        in_specs=[pl.BlockSpec((1, tm, d), lambda bi, m: (bi, m, 0)), pl.BlockSpec((1, d), lambda bi, m: (0, 0))],
        out_specs=pl.BlockSpec((1, tm, d), lambda bi, m: (bi, m, 0)),
        out_shape=jax.ShapeDtypeStruct((b, l, d), F32),
        compiler_params=_cparams(("parallel", "parallel"), 6 * tm * d * 4 + (4 << 20)),
        name="final_norm",
    )(x, g)


def _axial_angles(rows, dim):
    row = jnp.broadcast_to(jnp.arange(rows, dtype=F32)[:, None], (rows, GRID_W)).reshape(-1)
    col = jnp.broadcast_to(jnp.arange(GRID_W, dtype=F32)[None, :], (rows, GRID_W)).reshape(-1)
    n_freq = dim // 4
    inv = ROPE_BASE ** (-jnp.arange(n_freq, dtype=F32) / n_freq)
    return jnp.concatenate([row[:, None] * inv, col[:, None] * inv], axis=-1)


def _rope_tables(rows, dim):
    ang = _axial_angles(rows, dim)
    cos, sin = jnp.cos(ang), jnp.sin(ang)
    cos_t = jnp.concatenate([cos, cos], axis=-1)
    sin_t = jnp.concatenate([-sin, sin], axis=-1)
    pad = LANES - dim
    if pad:
        cos_t = jnp.pad(cos_t, ((0, 0), (0, pad)))
        sin_t = jnp.pad(sin_t, ((0, 0), (0, pad)))
    return cos_t, sin_t


def _mla_weights(g_q, w_uq, g_kv, w_ukv, heads, ret_w):
    q_rank, kv_rank = g_q.shape[0], g_kv.shape[0]
    wq = w_uq.reshape(q_rank, heads, HEAD_DIM + MLA_DR)
    wq = jnp.pad(wq, ((0, 0), (0, 0), (0, HEAD_DIM - MLA_DR))).reshape(q_rank, heads * 2 * HEAD_DIM).astype(BF16)
    wkv = w_ukv.reshape(kv_rank, heads, 2 * HEAD_DIM)
    wkn = wkv[:, :, :HEAD_DIM].reshape(kv_rank, ret_w).astype(BF16)
    wv = wkv[:, :, HEAD_DIM:].reshape(kv_rank, ret_w).astype(BF16)
    return g_q[None, :], wq, g_kv[None, :], wkn, wv


def _conv_ffn(h, x, mod, mod_row, w_up, conv_w, conv_b, w_down, layer, final_g=None):
    act, w_down_bf16 = _ffn_up(h, w_up, conv_w, conv_b[:, None, :], w_down, layer)
    d = x.shape[2]
    n_col_tiles = d // _tile("ffn_down_tn", d)
    if final_g is None:
        return _ffn_down(act, w_down_bf16, x, mod, mod_row)
    if n_col_tiles == 1:
        return _final_norm(_ffn_down(act, w_down_bf16, x, mod, mod_row), final_g)
    left = _ffn_down(act, w_down_bf16, x, mod, mod_row, n_col_tiles - 1)
    return _ffn_down_norm(act, w_down_bf16, x, left, mod, mod_row, final_g)


def kernel(x, c, ctx, c_ctx, ada_w, ada_b, norm1_g, norm2_g, ffn_w_up, ffn_conv_w, ffn_conv_b, ffn_w_down, mix_w_in, mla_q_norm_g, mla_w_uq, mla_kv_norm_g, mla_w_ukv, ret_decay_f, ret_decay_b, mix_w_out, pool_w, pool_scale, final_g):
    b, l, d = x.shape
    lc = ctx.shape[1]
    depth = ada_w.shape[0]
    heads = ret_decay_f.shape[1]
    ret_w = heads * HEAD_DIM
    rows = l // GRID_W
    rope_ret = _rope_tables(rows, HEAD_DIM)
    rope_mla = _rope_tables(rows, MLA_DR)

    ctx_row = b
    cc = jnp.concatenate([c, c_ctx[None, :], jnp.zeros((-(b + 1) % 8, d), F32)], axis=0)
    mods = _ada(cc, ada_w, ada_b).reshape(depth, cc.shape[0], 6, d)

    x_lat, x_ctx = x, ctx
    for layer in range(depth):
        j = layer // 2
        with_ctx = layer < depth - 1
        mod = mods[layer]
        g1 = norm1_g[layer][None, :]
        g2 = norm2_g[layer][None, :]
        h_ctx = None
        if layer % 2 == 0:
            w_in_t = jnp.swapaxes(mix_w_in[j], 0, 1)
            mla_w = _mla_weights(mla_q_norm_g[j], mla_w_uq[j], mla_kv_norm_g[j], mla_w_ukv[j], heads, ret_w)
            dec = jnp.stack([ret_decay_f[j], ret_decay_b[j]]).astype(F32)

            h1_lat, q_lat, k_lat, v_lat = _inproj_mla(x_lat, mod, None, g1, w_in_t, 4 * ret_w, *mla_w, rope_mla, heads)
            h1_ctx, q_ctx, k_ctx, v_ctx = _inproj_mla(x_ctx.reshape(1, b * lc, d), mod, ctx_row, g1, w_in_t,
                                                      4 * ret_w, *mla_w, None, heads)
            q_ctx, k_ctx, v_ctx = [a.reshape(b, lc, -1) for a in (q_ctx, k_ctx, v_ctx)]
            ret_lat, w_out_bf16 = _inproj_ret(h1_lat, w_in_t, rope_ret, ret_w, cast_extra=mix_w_out[j])
            ret_ctx = _inproj_ret(h1_ctx, w_in_t, None, ret_w).reshape(b, lc, 4 * ret_w)

            o_ret_lat, o_ret_ctx = _retention(dec, ret_lat, ret_ctx, heads)
            o_mla_lat = _attention(q_lat, k_ctx, v_ctx, k_lat, v_lat, heads)
            x_lat, h_lat = _mix_out(o_ret_lat, o_mla_lat, w_out_bf16, x_lat, mod, None, g2)
            if with_ctx:
                o_mla_ctx = _attention(q_ctx, k_ctx, v_ctx, None, None, heads)
                x_ctx, h_ctx = _mix_out(o_ret_ctx, o_mla_ctx, w_out_bf16, x_ctx, mod, ctx_row, g2)
        else:
            ps = pool_scale[j][None, :]
            if with_ctx:
                x_ctx, h_ctx = _pool_mix(x_ctx, mod, ctx_row, g1, pool_w[j], ps, g2)
            x_lat, h_lat = _pool_mix(x_lat, mod, None, g1, pool_w[j], ps, g2)
        last = layer == depth - 1
        x_lat = _conv_ffn(h_lat, x_lat, mod, None, ffn_w_up, ffn_conv_w, ffn_conv_b, ffn_w_down, layer,
                          final_g=final_g[None, :] if last else None)
        if with_ctx:
            x_ctx = _conv_ffn(h_ctx, x_ctx, mod, ctx_row, ffn_w_up, ffn_conv_w, ffn_conv_b, ffn_w_down, layer)
    return x_lat
```

```python
import functools

import jax
import jax.numpy as jnp
from jax import lax
from jax.experimental import pallas as pl
from jax.experimental.pallas import tpu as pltpu

F32 = jnp.float32
BF16 = jnp.bfloat16

GRID_W = 64
HEAD_DIM = 128
RET_CHUNK = 128
MLA_DR = 64
POOL_WINDOWS = (2, 4, 8, 16)
ROPE_BASE = 10000.0
EPS = 1e-6
LOG2_E = 1.4426950408889634

LANES = 128
MXU_DIM = 256
SUBLANES = 8
BF16_TILE_ROWS = 16
HALO = 16
V7X_VMEM_BYTES = 64 * 1024 * 1024
VMEM_CAP = V7X_VMEM_BYTES - 8 * 1024 * 1024

SHIFT1, SCALE1, GATE1, SHIFT2, SCALE2, GATE2 = range(6)

TILES = dict(
    inproj_ret_tm=1024, inproj_ret_piece_rows=256,
    inproj_mla_tm=512, inproj_mla_piece_rows=256,
    attn_tq=512, attn_heads_per_step=2,
    mix_out_tm=512, mix_out_piece_rows=256,
    ffn_up_tn=256, ffn_up_piece_rows=128, ffn_up_max_rows=2048,
    ffn_down_tm=512, ffn_down_tn=1024,
    pool_tm=512,
    final_tm=512,
    ada_tn=1024,
)


def _tile(name, extent):
    tile = min(extent, TILES[name])
    assert extent % tile == 0, f"{name}: extent {extent} is not a multiple of its tile {tile}"
    return tile


def _cparams(sem, vmem_bytes, flags=None):
    return pltpu.CompilerParams(dimension_semantics=sem, vmem_limit_bytes=min(int(vmem_bytes), VMEM_CAP),
                                flags=flags)


def _silu(x):
    return x / (1.0 + jnp.exp(-x))


def _norm_mod(x, g, shift, scale):
    gain = g * (1.0 + scale)
    return x * lax.rsqrt(jnp.mean(x * x, axis=-1, keepdims=True) + EPS) * gain + shift


def _dot(a, b):
    return jnp.dot(a, b, preferred_element_type=F32)


def _dot_nt(a, b):
    return lax.dot_general(a, b, (((1,), (1,)), ((), ())), preferred_element_type=F32)


def _dot_tn(a, b):
    return lax.dot_general(a, b, (((0,), (0,)), ((), ())), preferred_element_type=F32)


def _mod_index(mod_row, batch_axis, col_axis=None):
    def index(*ids):
        row = ids[batch_axis] if mod_row is None else mod_row
        return (row, 0, 0 if col_axis is None else ids[col_axis])
    return index


def _ada_kernel(c_ref, w_ref, b_ref, o_ref):
    s = _silu(c_ref[...]).astype(BF16)
    o_ref[0] = _dot(s, w_ref[0].astype(BF16)) + b_ref[0]


def _ada(cc, ada_w, ada_b):
    depth, d, n6 = ada_w.shape
    rows = cc.shape[0]
    tn = _tile("ada_tn", n6)
    return pl.pallas_call(
        _ada_kernel,
        grid=(depth, n6 // tn),
        in_specs=[
            pl.BlockSpec((rows, d), lambda l, n: (0, 0)),
            pl.BlockSpec((1, d, tn), lambda l, n: (l, 0, n)),
            pl.BlockSpec((1, 1, tn), lambda l, n: (l, 0, n)),
        ],
        out_specs=pl.BlockSpec((1, rows, tn), lambda l, n: (l, 0, n)),
        out_shape=jax.ShapeDtypeStruct((depth, rows, n6), F32),
        compiler_params=_cparams(("parallel", "parallel"), 2 * d * tn * 4 + d * tn * 2 + (8 << 20)),
        name="ada_mod",
    )(cc, ada_w, ada_b.reshape(depth, 1, n6))


def _inproj_ret_kernel(*refs, rope, cast_extra, heads_per_tile, k_scale, row_pieces):
    refs = list(refs)
    h_ref = refs.pop(0)
    if rope:
        cos_ref, sin_ref = refs.pop(0), refs.pop(0)
    w_ref = refs.pop(0)
    if cast_extra:
        extra_ref = refs.pop(0)
    o_ref = refs.pop(0)
    if cast_extra:
        extra_out = refs.pop(0)
    (w_scr,) = refs
    n = pl.program_id(0)

    @pl.when((pl.program_id(1) == 0) & (pl.program_id(2) == 0))
    def _():
        w_scr[...] = w_ref[...].astype(BF16)

    if cast_extra:
        extra_out[...] = extra_ref[...].astype(BF16)
    scale = jnp.where(n == 1, k_scale, 1.0).astype(F32)
    if rope:
        rotated = n < 2
        cos = jnp.where(rotated, cos_ref[...], 1.0) * scale
        sin = jnp.where(rotated, sin_ref[...], 0.0) * scale
    pr = h_ref.shape[1] // row_pieces
    for i in range(row_pieces):
        rs = slice(i * pr, (i + 1) * pr)
        z = _dot_nt(h_ref[0, rs, :], w_scr[...])
        if rope:
            for h in range(heads_per_tile):
                zh = z[:, h * HEAD_DIM:(h + 1) * HEAD_DIM]
                r = zh * cos[rs] + pltpu.roll(zh, HEAD_DIM // 2, 1) * sin[rs]
                o_ref[0, rs, h * HEAD_DIM:(h + 1) * HEAD_DIM] = r.astype(BF16)
        else:
            o_ref[0, rs, :] = (z * scale).astype(BF16)


def _inproj_ret(h, w_in_t, rope_tabs, ret_w, cast_extra=None):
    b, l, d = h.shape
    tm = _tile("inproj_ret_tm", l)
    tn = ret_w
    n_tiles = 4
    n_m = l // tm
    rope = rope_tabs is not None
    in_specs = [pl.BlockSpec((1, tm, d), lambda n, bi, m: (bi, m, 0))]
    args = [h]
    if rope:
        in_specs += [pl.BlockSpec((tm, HEAD_DIM), lambda n, bi, m: (m, 0))] * 2
        args += list(rope_tabs)
    in_specs.append(pl.BlockSpec((tn, d), lambda n, bi, m: (n, 0)))
    args.append(w_in_t)
    out_specs = [pl.BlockSpec((1, tm, tn), lambda n, bi, m: (bi, m, n))]
    out_shape = [jax.ShapeDtypeStruct((b, l, n_tiles * tn), BF16)]
    vmem = 2 * tm * d * 2 + 2 * d * tn * 4 + d * tn * 2 + 2 * tm * tn * 2 + 4 * tm * tn * 4 + (6 << 20)
    if cast_extra is not None:
        er, ec = cast_extra.shape
        band = er // (n_tiles * b * n_m)
        assert band * n_tiles * b * n_m == er and band % BF16_TILE_ROWS == 0, (er, band)
        extra_spec = pl.BlockSpec((band, ec), lambda n, bi, m: ((n * b + bi) * n_m + m, 0))
        in_specs.append(extra_spec)
        args.append(cast_extra)
        out_specs.append(extra_spec)
        out_shape.append(jax.ShapeDtypeStruct((er, ec), BF16))
        vmem += 2 * band * ec * 6
    outs = pl.pallas_call(
        functools.partial(_inproj_ret_kernel, rope=rope, cast_extra=cast_extra is not None,
                          heads_per_tile=tn // HEAD_DIM, k_scale=HEAD_DIM ** -0.5,
                          row_pieces=max(1, tm // TILES["inproj_ret_piece_rows"])),
        grid=(n_tiles, b, l // tm),
        in_specs=in_specs,
        out_specs=out_specs,
        out_shape=out_shape,
        scratch_shapes=[pltpu.VMEM((tn, d), BF16)],
        compiler_params=_cparams(("arbitrary", "arbitrary", "arbitrary"), vmem),
        name="inproj_ret_rope" if rope else "inproj_ret",
    )(*args)
    return outs if cast_extra is not None else outs[0]


def _rope_half_padded(x, cos, sin):
    rot = pltpu.roll(x, MLA_DR // 2, 1) + pltpu.roll(x, LANES - MLA_DR // 2, 1)
    return x * cos + rot * sin


def _inproj_mla_kernel(*refs, rope, heads, q_rank, kv_rank, q_scale, row_pieces):
    refs = list(refs)
    x_ref, mod_ref, g_ref = refs[:3]
    del refs[:3]
    if rope:
        cos_ref, sin_ref = refs[:2]
        del refs[:2]
    (wcq_ref, wckv_ref, wkr_ref, gq_ref, wq_ref, gkv_ref, wkn_ref, wv_ref,
     h_out, q_out, k_out, v_out, wt_scr) = refs

    @pl.when((pl.program_id(0) == 0) & (pl.program_id(1) == 0))
    def _():
        wt_scr[0:q_rank, :] = wcq_ref[...].astype(BF16)
        wt_scr[q_rank:q_rank + kv_rank, :] = wckv_ref[...].astype(BF16)
        wt_scr[q_rank + kv_rank:q_rank + kv_rank + MLA_DR, :] = wkr_ref[...].astype(BF16)
        wt_scr[q_rank + kv_rank + MLA_DR:, :] = jnp.zeros((LANES - MLA_DR, wt_scr.shape[1]), BF16)

    pr = x_ref.shape[1] // row_pieces
    for i in range(row_pieces):
        rs = slice(i * pr, (i + 1) * pr)
        h = _norm_mod(x_ref[0, rs, :], g_ref[...], mod_ref[0, SHIFT1:SHIFT1 + 1, :],
                      mod_ref[0, SCALE1:SCALE1 + 1, :]).astype(BF16)
        h_out[0, rs, :] = h
        zt = _dot_nt(h, wt_scr[...])
        cq = zt[:, :q_rank]
        ckv = zt[:, q_rank:q_rank + kv_rank]
        kr = zt[:, q_rank + kv_rank:]
        cqn = (cq * lax.rsqrt(jnp.mean(cq * cq, axis=-1, keepdims=True) + EPS) * gq_ref[...]).astype(BF16)
        ckvn = (ckv * lax.rsqrt(jnp.mean(ckv * ckv, axis=-1, keepdims=True) + EPS) * gkv_ref[...]).astype(BF16)
        q = _dot(cqn, wq_ref[...])
        kn = _dot(ckvn, wkn_ref[...])
        v_out[0, rs, :] = _dot(ckvn, wv_ref[...]).astype(BF16)
        if rope:
            cos = cos_ref[rs, :]
            sin = sin_ref[rs, :]
            kr = _rope_half_padded(kr, cos, sin)
        kr = kr.astype(BF16)
        for hd in range(heads):
            base = hd * 2 * HEAD_DIM
            qn = q[:, base:base + HEAD_DIM]
            qr = q[:, base + HEAD_DIM:base + 2 * HEAD_DIM]
            if rope:
                qr = _rope_half_padded(qr, cos, sin)
            q_out[0, rs, base:base + HEAD_DIM] = (qn * q_scale).astype(BF16)
            q_out[0, rs, base + HEAD_DIM:base + 2 * HEAD_DIM] = (qr * q_scale).astype(BF16)
            k_out[0, rs, base:base + HEAD_DIM] = kn[:, hd * HEAD_DIM:(hd + 1) * HEAD_DIM].astype(BF16)
            k_out[0, rs, base + HEAD_DIM:base + 2 * HEAD_DIM] = kr


def _inproj_mla(x, mod, mod_row, g, w_in_t, tail_row, gq, wq, gkv, wkn, wv, rope_tabs, heads):
    b, l, d = x.shape
    tm = _tile("inproj_mla_tm", l)
    q_rank, kv_rank = gq.shape[1], gkv.shape[1]
    rope = rope_tabs is not None
    full = lambda a: pl.BlockSpec(a.shape, lambda bi, m: (0,) * a.ndim)
    in_specs = [pl.BlockSpec((1, tm, d), lambda bi, m: (bi, m, 0)),
                pl.BlockSpec((1, 6, d), _mod_index(mod_row, 0)), full(g)]
    args = [x, mod, g]
    if rope:
        in_specs += [pl.BlockSpec((tm, LANES), lambda bi, m: (m, 0))] * 2
        args += list(rope_tabs)
    for start, rows in ((tail_row, q_rank), (tail_row + q_rank, kv_rank), (tail_row + q_rank + kv_rank, MLA_DR)):
        assert start % rows == 0, (start, rows)
        in_specs.append(pl.BlockSpec((rows, d), functools.partial(lambda blk, bi, m: (blk, 0), start // rows)))
        args.append(w_in_t)
    for a in (gq, wq, gkv, wkn, wv):
        in_specs.append(full(a))
        args.append(a)
    qw = heads * 2 * HEAD_DIM
    vw = heads * HEAD_DIM
    tail = q_rank + kv_rank + LANES
    out_spec = lambda w: pl.BlockSpec((1, tm, w), lambda bi, m: (bi, m, 0))
    wbytes = sum(a.size * a.dtype.itemsize for a in (wq, wkn, wv))
    vmem = (2 * tm * d * 4 + 2 * tail * d * 4 + tail * d * 2 + 2 * wbytes + 2 * tm * (d + 2 * qw + vw) * 2
            + tm * (3 * d * 4 + 6 * qw * 4) + (4 << 20))
    return pl.pallas_call(
        functools.partial(_inproj_mla_kernel, rope=rope, heads=heads, q_rank=q_rank, kv_rank=kv_rank,
                          q_scale=(HEAD_DIM + MLA_DR) ** -0.5 * LOG2_E,
                          row_pieces=max(1, tm // TILES["inproj_mla_piece_rows"])),
        grid=(b, l // tm),
        in_specs=in_specs,
        out_specs=[out_spec(d), out_spec(qw), out_spec(qw), out_spec(vw)],
        out_shape=[jax.ShapeDtypeStruct((b, l, d), BF16), jax.ShapeDtypeStruct((b, l, qw), BF16),
                   jax.ShapeDtypeStruct((b, l, qw), BF16), jax.ShapeDtypeStruct((b, l, vw), BF16)],
        scratch_shapes=[pltpu.VMEM((tail, d), BF16)],
        compiler_params=_cparams(("arbitrary", "arbitrary"), vmem),
        name="inproj_mla_rope" if rope else "inproj_mla",
    )(*args)


def _log_sigmoid(x):
    return jnp.minimum(x, 0.0) - jnp.log1p(jnp.exp(-jnp.abs(x)))


def _retention_kernel(dec_ref, ql_ref, kl_ref, vl_ref, gl_ref, qc_ref, kc_ref, vc_ref, gc_ref, ol_ref, oc_ref,
                      st_scr, *, n_ctx, n_lat, c):
    dk = HEAD_DIM
    hd = pl.program_id(1)
    lf = _log_sigmoid(jnp.full((c, c), dec_ref[0, hd], F32))
    lb = _log_sigmoid(jnp.full((c, c), dec_ref[1, hd], F32))
    ii = lax.broadcasted_iota(jnp.int32, (c, c), 0).astype(F32)
    jj = lax.broadcasted_iota(jnp.int32, (c, c), 1).astype(F32)
    diff = ii - jj
    fwd = diff >= 0
    decay = jnp.where(fwd, jnp.exp(jnp.where(fwd, diff, 0.0) * lf), jnp.exp(jnp.where(fwd, 0.0, -diff) * lb))
    ri = lax.broadcasted_iota(jnp.int32, (c, dk), 0).astype(F32)
    lfr = _log_sigmoid(jnp.full((c, dk), dec_ref[0, hd], F32))
    lbr = _log_sigmoid(jnp.full((c, dk), dec_ref[1, hd], F32))
    q_dec_f = jnp.exp((ri + 1.0) * lfr)
    k_dec_f = jnp.exp((c - 1.0 - ri) * lfr)
    q_dec_b = jnp.exp((c - ri) * lbr)
    k_dec_b = jnp.exp(ri * lbr)
    c_dec_f = jnp.exp(c * _log_sigmoid(jnp.full((dk, dk), dec_ref[0, hd], F32)))
    c_dec_b = jnp.exp(c * _log_sigmoid(jnp.full((dk, dk), dec_ref[1, hd], F32)))

    blocks = [(qc_ref, kc_ref, vc_ref, gc_ref, oc_ref, i) for i in range(n_ctx)]
    blocks += [(ql_ref, kl_ref, vl_ref, gl_ref, ol_ref, i) for i in range(n_lat)]

    def rows(ref, i):
        return ref[0, i * c:(i + 1) * c, :]

    for g, (_, k_ref, v_ref, _, _, i) in enumerate(blocks):
        kf = rows(k_ref, i).astype(F32)
        kk = jnp.concatenate([(kf * k_dec_f).astype(BF16), (kf * k_dec_b).astype(BF16)], axis=1)
        st_scr[g] = _dot_tn(kk, rows(v_ref, i))

    s = jnp.zeros((dk, dk), F32)
    for g in range(n_ctx + n_lat):
        kv = st_scr[g, :dk, :]
        st_scr[g, :dk, :] = s
        s = s * c_dec_f + kv
    s = jnp.zeros((dk, dk), F32)
    for g in list(range(n_ctx - 1, -1, -1)) + list(range(n_ctx + n_lat - 1, n_ctx - 1, -1)):
        kv = st_scr[g, dk:, :]
        st_scr[g, dk:, :] = s
        s = s * c_dec_b + kv

    for g, (q_ref, k_ref, v_ref, g_ref, o_ref, i) in enumerate(blocks):
        q = rows(q_ref, i)
        p = (_dot_nt(q, rows(k_ref, i)) * decay).astype(BF16)
        qf = q.astype(F32)
        qq = jnp.concatenate([(qf * q_dec_f).astype(BF16), (qf * q_dec_b).astype(BF16)], axis=1)
        o = _dot(p, rows(v_ref, i)) + _dot(qq, st_scr[g].astype(BF16))
        o = o * lax.rsqrt(jnp.mean(o * o, axis=-1, keepdims=True) + EPS)
        o_ref[0, i * c:(i + 1) * c, :] = (o * _silu(rows(g_ref, i).astype(F32))).astype(BF16)


def _retention(dec, ret_lat, ret_ctx, heads):
    b, l, _ = ret_lat.shape
    lc = ret_ctx.shape[1]
    c = MXU_DIM if (l % MXU_DIM == 0 and lc % MXU_DIM == 0) else RET_CHUNK
    n_lat, n_ctx = l // c, lc // c
    col = lambda rows, group: pl.BlockSpec((1, rows, HEAD_DIM), lambda bi, h: (bi, 0, group * heads + h))
    in_specs = [pl.BlockSpec(memory_space=pltpu.SMEM)]
    in_specs += [col(l, grp) for grp in range(4)] + [col(lc, grp) for grp in range(4)]
    out_col = lambda rows: pl.BlockSpec((1, rows, HEAD_DIM), lambda bi, h: (bi, 0, h))
    vmem = 2 * 5 * (l + lc) * HEAD_DIM * 2 + (n_lat + n_ctx) * 2 * HEAD_DIM * HEAD_DIM * 4 + (12 << 20)
    return pl.pallas_call(
        functools.partial(_retention_kernel, n_ctx=n_ctx, n_lat=n_lat, c=c),
        grid=(b, heads),
        in_specs=in_specs,
        out_specs=[out_col(l), out_col(lc)],
        out_shape=[jax.ShapeDtypeStruct((b, l, heads * HEAD_DIM), BF16),
                   jax.ShapeDtypeStruct((b, lc, heads * HEAD_DIM), BF16)],
        scratch_shapes=[pltpu.VMEM((n_lat + n_ctx, 2 * HEAD_DIM, HEAD_DIM), F32)],
        compiler_params=_cparams(("parallel", "parallel"), vmem),
        name="retention",
    )(dec, ret_lat, ret_lat, ret_lat, ret_lat, ret_ctx, ret_ctx, ret_ctx, ret_ctx)


def _attn_kernel(*refs, with_lat, tq):
    if with_lat:
        q_ref, kc_ref, vc_ref, kl_ref, vl_ref, o_ref = refs
    else:
        q_ref, kc_ref, vc_ref, o_ref = refs
    kw = 2 * HEAD_DIM
    for hp in range(o_ref.shape[2] // HEAD_DIM):
        kcols = slice(hp * kw, (hp + 1) * kw)
        vcols = slice(hp * HEAD_DIM, (hp + 1) * HEAD_DIM)
        for i in range(q_ref.shape[1] // tq):
            q = q_ref[0, i * tq:(i + 1) * tq, kcols]
            sc = _dot_nt(q, kc_ref[0, :, kcols])
            m = jnp.max(sc, axis=-1, keepdims=True)
            if with_lat:
                sl = _dot_nt(q, kl_ref[0, :, kcols])
                m = jnp.maximum(m, jnp.max(sl, axis=-1, keepdims=True))
            pc = jnp.exp2(sc - m)
            den = jnp.sum(pc, axis=-1, keepdims=True)
            o = _dot(pc.astype(BF16), vc_ref[0, :, vcols])
            if with_lat:
                p_lat = jnp.exp2(sl - m)
                den = den + jnp.sum(p_lat, axis=-1, keepdims=True)
                o = o + _dot(p_lat.astype(BF16), vl_ref[0, :, vcols])
            o_ref[0, i * tq:(i + 1) * tq, vcols] = (o / den).astype(BF16)


def _attention(q, k_ctx, v_ctx, k_lat, v_lat, heads):
    b, lq, _ = q.shape
    tq = _tile("attn_tq", lq)
    hps = _tile("attn_heads_per_step", heads)
    lc = k_ctx.shape[1]
    with_lat = k_lat is not None
    kw = 2 * HEAD_DIM
    head = lambda rows, w: pl.BlockSpec((1, rows, hps * w), lambda bi, h: (bi, 0, h))
    in_specs = [head(lq, kw), head(lc, kw), head(lc, HEAD_DIM)]
    args = [q, k_ctx, v_ctx]
    lk = lc
    if with_lat:
        ll = k_lat.shape[1]
        lk += ll
        in_specs += [head(ll, kw), head(ll, HEAD_DIM)]
        args += [k_lat, v_lat]
    vmem = 2 * hps * (lq * kw + lk * (kw + HEAD_DIM) + lq * HEAD_DIM) * 2 + 8 * tq * lk * 4 + (8 << 20)
    return pl.pallas_call(
        functools.partial(_attn_kernel, with_lat=with_lat, tq=tq),
        grid=(b, heads // hps),
        in_specs=in_specs,
        out_specs=head(lq, HEAD_DIM),
        out_shape=jax.ShapeDtypeStruct((b, lq, heads * HEAD_DIM), BF16),
        compiler_params=_cparams(("parallel", "parallel"), vmem),
        name="mla_attn" if with_lat else "mla_attn_ctx",
    )(*args)


def _mix_out_kernel(a1_ref, a2_ref, w_ref, x_ref, mod_ref, g2_ref, x_out, h_out, *, row_pieces):
    k1 = a1_ref.shape[2]
    pr = x_ref.shape[1] // row_pieces
    for i in range(row_pieces):
        rs = slice(i * pr, (i + 1) * pr)
        y = _dot(a1_ref[0, rs, :], w_ref[:k1, :]) + _dot(a2_ref[0, rs, :], w_ref[k1:, :])
        x1 = x_ref[0, rs, :] + mod_ref[0, GATE1:GATE1 + 1, :] * y
        x_out[0, rs, :] = x1
        h_out[0, rs, :] = _norm_mod(x1, g2_ref[...], mod_ref[0, SHIFT2:SHIFT2 + 1, :],
                                    mod_ref[0, SCALE2:SCALE2 + 1, :]).astype(BF16)


def _mix_out(a1, a2, w_bf16, x, mod, mod_row, g2):
    b, l, d = x.shape
    tm = _tile("mix_out_tm", l)
    k1, k2 = a1.shape[2], a2.shape[2]
    row = lambda w: pl.BlockSpec((1, tm, w), lambda bi, m: (bi, m, 0))
    vmem = (k1 + k2) * d * 2 + 2 * tm * (k1 + k2) * 2 + 4 * tm * d * 4 + 2 * tm * d * 2 + 6 * tm * d * 4 + (4 << 20)
    return pl.pallas_call(
        functools.partial(_mix_out_kernel, row_pieces=max(1, tm // TILES["mix_out_piece_rows"])),
        grid=(b, l // tm),
        in_specs=[row(k1), row(k2),
                  pl.BlockSpec((k1 + k2, d), lambda bi, m: (0, 0)),
                  row(d),
                  pl.BlockSpec((1, 6, d), _mod_index(mod_row, 0)),
                  pl.BlockSpec((1, d), lambda bi, m: (0, 0))],
        out_specs=[row(d), row(d)],
        out_shape=[jax.ShapeDtypeStruct((b, l, d), F32), jax.ShapeDtypeStruct((b, l, d), BF16)],
        compiler_params=_cparams(("parallel", "parallel"), vmem),
        name="mix_out",
    )(a1, a2, w_bf16, x, mod, g2)


def _ffn_up_kernel(h_ref, wa_ref, wg_ref, cwa_ref, cwg_ref, cba_ref, cbg_ref, wd_ref, o_ref, wd_out,
                   wa_scr, wg_scr, *, row_pieces):
    @pl.when(pl.program_id(1) == 0)
    def _():
        wa_scr[...] = wa_ref[...].astype(BF16)
        wg_scr[...] = wg_ref[...].astype(BF16)

    wd_out[...] = wd_ref[...].astype(BF16)

    l = h_ref.shape[1]
    pr = l // row_pieces
    sub = lax.broadcasted_iota(jnp.int32, (SUBLANES, 1), 0)

    def conv(w_scr, cw_ref, cb_ref):
        u = jnp.concatenate([_dot(h_ref[0, i * pr:(i + 1) * pr, :], w_scr[...]) for i in range(row_pieces)], axis=0)
        prev = pltpu.roll(u, 1, 0)
        nxt = pltpu.roll(u, l - 1, 0)
        prev = jnp.concatenate([jnp.where(sub == 0, 0.0, prev[:SUBLANES]), prev[SUBLANES:]], axis=0)
        nxt = jnp.concatenate([nxt[:-SUBLANES], jnp.where(sub == SUBLANES - 1, 0.0, nxt[-SUBLANES:])], axis=0)
        cw = cw_ref[...]
        return cw[0:1, :] * prev + cw[1:2, :] * u + cw[2:3, :] * nxt + cb_ref[...]

    a = conv(wa_scr, cwa_ref, cba_ref)
    gt = conv(wg_scr, cwg_ref, cbg_ref)
    o_ref[0] = (_silu(gt) * a).astype(BF16)


def _ffn_up(h, w_up, conv_w, conv_b, w_down, layer):
    b, l, d = h.shape
    d_out = w_down.shape[2]
    f = w_up.shape[2] // 2
    tn = _tile("ffn_up_tn", f)
    n_n = f // tn
    assert l <= TILES["ffn_up_max_rows"], "the row tile is the whole sequence"
    row_pieces = max(1, l // TILES["ffn_up_piece_rows"])
    assert l % (row_pieces * BF16_TILE_ROWS) == 0, (l, row_pieces)
    col = lambda rows, off: pl.BlockSpec((None, rows, tn), lambda n, bi: (layer, 0, n + off))
    band = f // (n_n * b)
    assert band * n_n * b == f and band % BF16_TILE_ROWS == 0, (f, band)
    vmem = (2 * l * d * 2 + 4 * d * tn * 4 + 2 * d * tn * 2 + 2 * band * d_out * 6
            + 2 * l * tn * 2 + 10 * l * tn * 4 + (4 << 20))
    return pl.pallas_call(
        functools.partial(_ffn_up_kernel, row_pieces=row_pieces),
        grid=(n_n, b),
        in_specs=[
            pl.BlockSpec((1, l, d), lambda n, bi: (bi, 0, 0)),
            col(d, 0), col(d, n_n), col(3, 0), col(3, n_n), col(1, 0), col(1, n_n),
            pl.BlockSpec((None, band, d_out), lambda n, bi: (layer, n * b + bi, 0)),
        ],
        out_specs=[pl.BlockSpec((1, l, tn), lambda n, bi: (bi, 0, n)),
                   pl.BlockSpec((band, d_out), lambda n, bi: (n * b + bi, 0))],
        out_shape=[jax.ShapeDtypeStruct((b, l, f), BF16), jax.ShapeDtypeStruct((f, d_out), BF16)],
        scratch_shapes=[pltpu.VMEM((d, tn), BF16), pltpu.VMEM((d, tn), BF16)],
        compiler_params=_cparams(("arbitrary", "arbitrary"), vmem),
        name="ffn_up",
    )(h, w_up, w_up, conv_w, conv_w, conv_b, conv_b, w_down)


def _ffn_down_kernel(a_ref, w_ref, x_ref, mod_ref, o_ref):
    y = _dot(a_ref[0], w_ref[...])
    o_ref[0] = x_ref[0] + mod_ref[0, GATE2:GATE2 + 1, :] * y


def _ffn_down(act, w_bf16, x, mod, mod_row, n_col_tiles=None):
    b, l, d = x.shape
    tm = _tile("ffn_down_tm", l)
    tn = _tile("ffn_down_tn", d)
    f = act.shape[2]
    if n_col_tiles is None:
        n_col_tiles = d // tn
    vmem = 2 * tm * f * 2 + 2 * f * tn * 2 + 5 * tm * tn * 4 + (6 << 20)
    return pl.pallas_call(
        _ffn_down_kernel,
        grid=(n_col_tiles, b, l // tm),
        in_specs=[
            pl.BlockSpec((1, tm, f), lambda n, bi, m: (bi, m, 0)),
            pl.BlockSpec((f, tn), lambda n, bi, m: (0, n)),
            pl.BlockSpec((1, tm, tn), lambda n, bi, m: (bi, m, n)),
            pl.BlockSpec((1, 6, tn), _mod_index(mod_row, 1, col_axis=0)),
        ],
        out_specs=pl.BlockSpec((1, tm, tn), lambda n, bi, m: (bi, m, n)),
        out_shape=jax.ShapeDtypeStruct((b, l, n_col_tiles * tn), F32),
        compiler_params=_cparams(("parallel", "parallel", "parallel"), vmem),
        name="ffn_down",
    )(act, w_bf16, x, mod)


def _ffn_down_norm_kernel(a_ref, w_ref, x_ref, left_ref, mod_ref, g_ref, o_ref):
    y = _dot(a_ref[0], w_ref[...])
    right = x_ref[0] + mod_ref[0, GATE2:GATE2 + 1, :] * y
    left = left_ref[0]
    wl = left.shape[1]
    d = wl + right.shape[1]
    ms = (jnp.sum(left * left, axis=-1, keepdims=True) + jnp.sum(right * right, axis=-1, keepdims=True)) / d
    r = lax.rsqrt(ms + EPS)
    o_ref[0, :, :wl] = left * r * g_ref[:, :wl]
    o_ref[0, :, wl:] = right * r * g_ref[:, wl:]


def _ffn_down_norm(act, w_bf16, x, left, mod, mod_row, g):
    b, l, d = x.shape
    tm = _tile("ffn_down_tm", l)
    tn = _tile("ffn_down_tn", d)
    f = act.shape[2]
    last = d // tn - 1
    vmem = 2 * tm * f * 2 + 2 * f * tn * 2 + 6 * tm * d * 4 + (6 << 20)
    return pl.pallas_call(
        _ffn_down_norm_kernel,
        grid=(b, l // tm),
        in_specs=[
            pl.BlockSpec((1, tm, f), lambda bi, m: (bi, m, 0)),
            pl.BlockSpec((f, tn), lambda bi, m: (0, last)),
            pl.BlockSpec((1, tm, tn), lambda bi, m: (bi, m, last)),
            pl.BlockSpec((1, tm, d - tn), lambda bi, m: (bi, m, 0)),
            pl.BlockSpec((1, 6, tn), lambda bi, m: (bi if mod_row is None else mod_row, 0, last)),
            pl.BlockSpec((1, d), lambda bi, m: (0, 0)),
        ],
        out_specs=pl.BlockSpec((1, tm, d), lambda bi, m: (bi, m, 0)),
        out_shape=jax.ShapeDtypeStruct((b, l, d), F32),
        compiler_params=_cparams(("parallel", "parallel"), vmem),
        name="ffn_down_norm",
    )(act, w_bf16, x, left, mod, g)


def _pool_kernel(xm_ref, xp_ref, xn_ref, mod_ref, g_ref, w_ref, ps_ref, g2_ref, x_out, h_out, x1_scr, w_scr,
                 *, tm, n_m, seq_len):
    m = pl.program_id(1)

    @pl.when((pl.program_id(0) == 0) & (m == 0))
    def _():
        w_scr[...] = w_ref[...].astype(BF16)

    g = g_ref[...]
    shift = mod_ref[0, SHIFT1:SHIFT1 + 1, :]
    scale = mod_ref[0, SCALE1:SCALE1 + 1, :]
    xm = xm_ref[0]
    hp = _norm_mod(xp_ref[0], g, shift, scale) * (m > 0).astype(F32)
    hn = _norm_mod(xn_ref[0], g, shift, scale) * (m < n_m - 1).astype(F32)
    hh = jnp.concatenate([hp, _norm_mod(xm, g, shift, scale), hn], axis=0)
    rows = tm + 2 * HALO
    t = (m * tm + lax.broadcasted_iota(jnp.int32, (tm, 1), 0)).astype(F32)
    gw = hh.shape[1] // len(POOL_WINDOWS)
    out_gain = mod_ref[0, GATE1:GATE1 + 1, :] * ps_ref[...]
    for gi, w in enumerate(POOL_WINDOWS):
        sl = slice(gi * gw, (gi + 1) * gw)
        hg = hh[:, sl]
        acc = hg
        span = 1
        while span < w // 2:
            acc = acc + pltpu.roll(acc, rows - span, 0)
            span *= 2
        acc = acc + pltpu.roll(acc, w // 2, 0)
        win = acc[HALO:HALO + tm]
        cnt = jnp.minimum(t + (w // 2 - 1), seq_len - 1.0) - jnp.maximum(t - w // 2, 0.0) + 1.0
        p = (win / cnt - hg[HALO:HALO + tm]).astype(BF16)
        x1_scr[:, sl] = xm[:, sl] + out_gain[:, sl] * _dot(p, w_scr[gi])
    x1 = x1_scr[...]
    x_out[0] = x1
    h_out[0] = _norm_mod(x1, g2_ref[...], mod_ref[0, SHIFT2:SHIFT2 + 1, :],
                         mod_ref[0, SCALE2:SCALE2 + 1, :]).astype(BF16)


def _pool_mix(x, mod, mod_row, g, pool_w, pool_scale, g2):
    b, l, d = x.shape
    tm = _tile("pool_tm", l)
    n_m = l // tm
    hb = tm // HALO
    ng, gw, _ = pool_w.shape
    row = pl.BlockSpec((1, tm, d), lambda bi, m: (bi, m, 0))
    vec = pl.BlockSpec((1, d), lambda bi, m: (0, 0))
    vmem = 7 * tm * d * 4 + 2 * ng * gw * gw * 4 + 8 * (tm + 2 * HALO) * d * 4 + (6 << 20)
    return pl.pallas_call(
        functools.partial(_pool_kernel, tm=tm, n_m=n_m, seq_len=l),
        grid=(b, n_m),
        in_specs=[
            row,
            pl.BlockSpec((1, HALO, d), lambda bi, m: (bi, jnp.maximum(m * hb - 1, 0), 0)),
            pl.BlockSpec((1, HALO, d), lambda bi, m: (bi, jnp.minimum((m + 1) * hb, l // HALO - 1), 0)),
            pl.BlockSpec((1, 6, d), _mod_index(mod_row, 0)),
            vec,
            pl.BlockSpec((ng, gw, gw), lambda bi, m: (0, 0, 0)),
            vec, vec,
        ],
        out_specs=[row, row],
        out_shape=[jax.ShapeDtypeStruct((b, l, d), F32), jax.ShapeDtypeStruct((b, l, d), BF16)],
        scratch_shapes=[pltpu.VMEM((tm, d), F32), pltpu.VMEM((ng, gw, gw), BF16)],
        compiler_params=_cparams(("arbitrary", "arbitrary"), vmem),
        name="pool_mix",
    )(x, x, x, mod, g, pool_w, pool_scale, g2)


def _final_norm_kernel(x_ref, g_ref, o_ref):
    x = x_ref[0]
    o_ref[0] = x * lax.rsqrt(jnp.mean(x * x, axis=-1, keepdims=True) + EPS) * g_ref[...]


def _final_norm(x, g):
    b, l, d = x.shape
    tm = _tile("final_tm", l)
    return pl.pallas_call(
        _final_norm_kernel,
        grid=(b, l // tm),
        in_specs=[pl.BlockSpec((1, tm, d), lambda bi, m: (bi, m, 0)), pl.BlockSpec((1, d), lambda bi, m: (0, 0))],
        out_specs=pl.BlockSpec((1, tm, d), lambda bi, m: (bi, m, 0)),
        out_shape=jax.ShapeDtypeStruct((b, l, d), F32),
        compiler_params=_cparams(("parallel", "parallel"), 6 * tm * d * 4 + (4 << 20)),
        name="final_norm",
    )(x, g)


def _axial_angles(rows, dim):
    row = jnp.broadcast_to(jnp.arange(rows, dtype=F32)[:, None], (rows, GRID_W)).reshape(-1)
    col = jnp.broadcast_to(jnp.arange(GRID_W, dtype=F32)[None, :], (rows, GRID_W)).reshape(-1)
    n_freq = dim // 4
    inv = ROPE_BASE ** (-jnp.arange(n_freq, dtype=F32) / n_freq)
    return jnp.concatenate([row[:, None] * inv, col[:, None] * inv], axis=-1)


def _rope_tables(rows, dim):
    ang = _axial_angles(rows, dim)
    cos, sin = jnp.cos(ang), jnp.sin(ang)
    cos_t = jnp.concatenate([cos, cos], axis=-1)
    sin_t = jnp.concatenate([-sin, sin], axis=-1)
    pad = LANES - dim
    if pad:
        cos_t = jnp.pad(cos_t, ((0, 0), (0, pad)))
        sin_t = jnp.pad(sin_t, ((0, 0), (0, pad)))
    return cos_t, sin_t


def _mla_weights(g_q, w_uq, g_kv, w_ukv, heads, ret_w):
    q_rank, kv_rank = g_q.shape[0], g_kv.shape[0]
    wq = w_uq.reshape(q_rank, heads, HEAD_DIM + MLA_DR)
    wq = jnp.pad(wq, ((0, 0), (0, 0), (0, HEAD_DIM - MLA_DR))).reshape(q_rank, heads * 2 * HEAD_DIM).astype(BF16)
    wkv = w_ukv.reshape(kv_rank, heads, 2 * HEAD_DIM)
    wkn = wkv[:, :, :HEAD_DIM].reshape(kv_rank, ret_w).astype(BF16)
    wv = wkv[:, :, HEAD_DIM:].reshape(kv_rank, ret_w).astype(BF16)
    return g_q[None, :], wq, g_kv[None, :], wkn, wv


def _conv_ffn(h, x, mod, mod_row, w_up, conv_w, conv_b, w_down, layer, final_g=None):
    act, w_down_bf16 = _ffn_up(h, w_up, conv_w, conv_b[:, None, :], w_down, layer)
    d = x.shape[2]
    n_col_tiles = d // _tile("ffn_down_tn", d)
    if final_g is None:
        return _ffn_down(act, w_down_bf16, x, mod, mod_row)
    if n_col_tiles == 1:
        return _final_norm(_ffn_down(act, w_down_bf16, x, mod, mod_row), final_g)
    left = _ffn_down(act, w_down_bf16, x, mod, mod_row, n_col_tiles - 1)
    return _ffn_down_norm(act, w_down_bf16, x, left, mod, mod_row, final_g)


def kernel(x, c, ctx, c_ctx, ada_w, ada_b, norm1_g, norm2_g, ffn_w_up, ffn_conv_w, ffn_conv_b, ffn_w_down, mix_w_in, mla_q_norm_g, mla_w_uq, mla_kv_norm_g, mla_w_ukv, ret_decay_f, ret_decay_b, mix_w_out, pool_w, pool_scale, final_g):
    b, l, d = x.shape
    lc = ctx.shape[1]
    depth = ada_w.shape[0]
    heads = ret_decay_f.shape[1]
    ret_w = heads * HEAD_DIM
    rows = l // GRID_W
    rope_ret = _rope_tables(rows, HEAD_DIM)
    rope_mla = _rope_tables(rows, MLA_DR)

    ctx_row = b
    cc = jnp.concatenate([c, c_ctx[None, :], jnp.zeros((-(b + 1) % 8, d), F32)], axis=0)
    mods = _ada(cc, ada_w, ada_b).reshape(depth, cc.shape[0], 6, d)

    x_lat, x_ctx = x, ctx
    for layer in range(depth):
        j = layer // 2
        with_ctx = layer < depth - 1
        mod = mods[layer]
        g1 = norm1_g[layer][None, :]
        g2 = norm2_g[layer][None, :]
        h_ctx = None
        if layer % 2 == 0:
            w_in_t = jnp.swapaxes(mix_w_in[j], 0, 1)
            mla_w = _mla_weights(mla_q_norm_g[j], mla_w_uq[j], mla_kv_norm_g[j], mla_w_ukv[j], heads, ret_w)
            dec = jnp.stack([ret_decay_f[j], ret_decay_b[j]]).astype(F32)

            h1_lat, q_lat, k_lat, v_lat = _inproj_mla(x_lat, mod, None, g1, w_in_t, 4 * ret_w, *mla_w, rope_mla, heads)
            h1_ctx, q_ctx, k_ctx, v_ctx = _inproj_mla(x_ctx.reshape(1, b * lc, d), mod, ctx_row, g1, w_in_t,
                                                      4 * ret_w, *mla_w, None, heads)
            q_ctx, k_ctx, v_ctx = [a.reshape(b, lc, -1) for a in (q_ctx, k_ctx, v_ctx)]
            ret_lat, w_out_bf16 = _inproj_ret(h1_lat, w_in_t, rope_ret, ret_w, cast_extra=mix_w_out[j])
            ret_ctx = _inproj_ret(h1_ctx, w_in_t, None, ret_w).reshape(b, lc, 4 * ret_w)

            o_ret_lat, o_ret_ctx = _retention(dec, ret_lat, ret_ctx, heads)
            o_mla_lat = _attention(q_lat, k_ctx, v_ctx, k_lat, v_lat, heads)
            x_lat, h_lat = _mix_out(o_ret_lat, o_mla_lat, w_out_bf16, x_lat, mod, None, g2)
            if with_ctx:
                o_mla_ctx = _attention(q_ctx, k_ctx, v_ctx, None, None, heads)
                x_ctx, h_ctx = _mix_out(o_ret_ctx, o_mla_ctx, w_out_bf16, x_ctx, mod, ctx_row, g2)
        else:
            ps = pool_scale[j][None, :]
            if with_ctx:
                x_ctx, h_ctx = _pool_mix(x_ctx, mod, ctx_row, g1, pool_w[j], ps, g2)
            x_lat, h_lat = _pool_mix(x_lat, mod, None, g1, pool_w[j], ps, g2)
        last = layer == depth - 1
        x_lat = _conv_ffn(h_lat, x_lat, mod, None, ffn_w_up, ffn_conv_w, ffn_conv_b, ffn_w_down, layer,
                          final_g=final_g[None, :] if last else None)
        if with_ctx:
            x_ctx = _conv_ffn(h_ctx, x_ctx, mod, ctx_row, ffn_w_up, ffn_conv_w, ffn_conv_b, ffn_w_down, layer)
    return x_lat
```

```python
import functools

import jax
import jax.numpy as jnp
from jax import lax
from jax.experimental import pallas as pl
from jax.experimental.pallas import tpu as pltpu

F32 = jnp.float32
BF16 = jnp.bfloat16

GRID_W = 64
HEAD_DIM = 128
RET_CHUNK = 128
MLA_DR = 64
POOL_WINDOWS = (2, 4, 8, 16)
ROPE_BASE = 10000.0
EPS = 1e-6
LOG2_E = 1.4426950408889634

LANES = 128
MXU_DIM = 256
SUBLANES = 8
BF16_TILE_ROWS = 16
HALO = 16
V7X_VMEM_BYTES = 64 * 1024 * 1024
VMEM_CAP = V7X_VMEM_BYTES - 8 * 1024 * 1024

SHIFT1, SCALE1, GATE1, SHIFT2, SCALE2, GATE2 = range(6)

TILES = dict(
    inproj_ret_tm=2048, inproj_ret_piece_rows=256,
    inproj_mla_tm=512, inproj_mla_piece_rows=256,
    attn_tq=512, attn_heads_per_step=2,
    mix_out_tm=512, mix_out_piece_rows=256,
    ffn_up_tn=256, ffn_up_piece_rows=128, ffn_up_max_rows=2048,
    ffn_down_tm=512, ffn_down_tn=1024,
    pool_tm=512,
    final_tm=512,
    ada_tn=1024,
)


def _tile(name, extent):
    tile = min(extent, TILES[name])
    assert extent % tile == 0, f"{name}: extent {extent} is not a multiple of its tile {tile}"
    return tile


def _cparams(sem, vmem_bytes):
    return pltpu.CompilerParams(dimension_semantics=sem, vmem_limit_bytes=min(int(vmem_bytes), VMEM_CAP))


def _silu(x):
    return x / (1.0 + jnp.exp(-x))


def _norm_mod(x, g, shift, scale):
    gain = g * (1.0 + scale)
    return x * lax.rsqrt(jnp.mean(x * x, axis=-1, keepdims=True) + EPS) * gain + shift


def _dot(a, b):
    return jnp.dot(a, b, preferred_element_type=F32)


def _dot_nt(a, b):
    return lax.dot_general(a, b, (((1,), (1,)), ((), ())), preferred_element_type=F32)


def _dot_tn(a, b):
    return lax.dot_general(a, b, (((0,), (0,)), ((), ())), preferred_element_type=F32)


def _mod_index(mod_row, batch_axis, col_axis=None):
    def index(*ids):
        row = ids[batch_axis] if mod_row is None else mod_row
        return (row, 0, 0 if col_axis is None else ids[col_axis])
    return index


def _ada_kernel(c_ref, w_ref, b_ref, o_ref):
    s = _silu(c_ref[...]).astype(BF16)
    o_ref[0] = _dot(s, w_ref[0].astype(BF16)) + b_ref[0]


def _ada(cc, ada_w, ada_b):
    depth, d, n6 = ada_w.shape
    rows = cc.shape[0]
    tn = _tile("ada_tn", n6)
    return pl.pallas_call(
        _ada_kernel,
        grid=(depth, n6 // tn),
        in_specs=[
            pl.BlockSpec((rows, d), lambda l, n: (0, 0)),
            pl.BlockSpec((1, d, tn), lambda l, n: (l, 0, n)),
            pl.BlockSpec((1, 1, tn), lambda l, n: (l, 0, n)),
        ],
        out_specs=pl.BlockSpec((1, rows, tn), lambda l, n: (l, 0, n)),
        out_shape=jax.ShapeDtypeStruct((depth, rows, n6), F32),
        compiler_params=_cparams(("parallel", "parallel"), 2 * d * tn * 4 + d * tn * 2 + (8 << 20)),
        name="ada_mod",
    )(cc, ada_w, ada_b.reshape(depth, 1, n6))


def _inproj_ret_kernel(*refs, rope, cast_extra, heads_per_tile, k_scale, row_pieces):
    refs = list(refs)
    h_ref = refs.pop(0)
    if rope:
        cos_ref, sin_ref = refs.pop(0), refs.pop(0)
    w_ref = refs.pop(0)
    if cast_extra:
        extra_ref = refs.pop(0)
    o_ref = refs.pop(0)
    if cast_extra:
        extra_out = refs.pop(0)
    (w_scr,) = refs
    n = pl.program_id(0)

    @pl.when((pl.program_id(1) == 0) & (pl.program_id(2) == 0))
    def _():
        w_scr[...] = w_ref[...].astype(BF16)

    if cast_extra:
        extra_out[...] = extra_ref[...].astype(BF16)
    scale = jnp.where(n == 1, k_scale, 1.0).astype(F32)
    if rope:
        rotated = n < 2
        cos = jnp.where(rotated, cos_ref[...], 1.0) * scale
        sin = jnp.where(rotated, sin_ref[...], 0.0) * scale
    pr = h_ref.shape[1] // row_pieces
    for i in range(row_pieces):
        rs = slice(i * pr, (i + 1) * pr)
        z = _dot_nt(h_ref[0, rs, :], w_scr[...])
        if rope:
            for h in range(heads_per_tile):
                zh = z[:, h * HEAD_DIM:(h + 1) * HEAD_DIM]
                r = zh * cos[rs] + pltpu.roll(zh, HEAD_DIM // 2, 1) * sin[rs]
                o_ref[0, rs, h * HEAD_DIM:(h + 1) * HEAD_DIM] = r.astype(BF16)
        else:
            o_ref[0, rs, :] = (z * scale).astype(BF16)


def _inproj_ret(h, w_in_t, rope_tabs, ret_w, cast_extra=None):
    b, l, d = h.shape
    tm = _tile("inproj_ret_tm", l)
    tn = ret_w
    n_tiles = 4
    n_m = l // tm
    rope = rope_tabs is not None
    in_specs = [pl.BlockSpec((1, tm, d), lambda n, bi, m: (bi, m, 0))]
    args = [h]
    if rope:
        in_specs += [pl.BlockSpec((tm, HEAD_DIM), lambda n, bi, m: (m, 0))] * 2
        args += list(rope_tabs)
    in_specs.append(pl.BlockSpec((tn, d), lambda n, bi, m: (n, 0)))
    args.append(w_in_t)
    out_specs = [pl.BlockSpec((1, tm, tn), lambda n, bi, m: (bi, m, n))]
    out_shape = [jax.ShapeDtypeStruct((b, l, n_tiles * tn), BF16)]
    vmem = 2 * tm * d * 2 + 2 * d * tn * 4 + d * tn * 2 + 2 * tm * tn * 2 + 4 * tm * tn * 4 + (6 << 20)
    if cast_extra is not None:
        er, ec = cast_extra.shape
        band = er // (n_tiles * b * n_m)
        assert band * n_tiles * b * n_m == er and band % BF16_TILE_ROWS == 0, (er, band)
        extra_spec = pl.BlockSpec((band, ec), lambda n, bi, m: ((n * b + bi) * n_m + m, 0))
        in_specs.append(extra_spec)
        args.append(cast_extra)
        out_specs.append(extra_spec)
        out_shape.append(jax.ShapeDtypeStruct((er, ec), BF16))
        vmem += 2 * band * ec * 6
    outs = pl.pallas_call(
        functools.partial(_inproj_ret_kernel, rope=rope, cast_extra=cast_extra is not None,
                          heads_per_tile=tn // HEAD_DIM, k_scale=HEAD_DIM ** -0.5,
                          row_pieces=max(1, tm // TILES["inproj_ret_piece_rows"])),
        grid=(n_tiles, b, l // tm),
        in_specs=in_specs,
        out_specs=out_specs,
        out_shape=out_shape,
        scratch_shapes=[pltpu.VMEM((tn, d), BF16)],
        compiler_params=_cparams(("arbitrary", "arbitrary", "arbitrary"), vmem),
        name="inproj_ret_rope" if rope else "inproj_ret",
    )(*args)
    return outs if cast_extra is not None else outs[0]


def _rope_half_padded(x, cos, sin):
    rot = pltpu.roll(x, MLA_DR // 2, 1) + pltpu.roll(x, LANES - MLA_DR // 2, 1)
    return x * cos + rot * sin


def _inproj_mla_kernel(*refs, rope, heads, q_rank, kv_rank, q_scale, row_pieces):
    refs = list(refs)
    x_ref, mod_ref, g_ref = refs[:3]
    del refs[:3]
    if rope:
        cos_ref, sin_ref = refs[:2]
        del refs[:2]
    (wcq_ref, wckv_ref, wkr_ref, gq_ref, wq_ref, gkv_ref, wkn_ref, wv_ref,
     h_out, q_out, k_out, v_out, wt_scr) = refs

    @pl.when((pl.program_id(0) == 0) & (pl.program_id(1) == 0))
    def _():
        wt_scr[0:q_rank, :] = wcq_ref[...].astype(BF16)
        wt_scr[q_rank:q_rank + kv_rank, :] = wckv_ref[...].astype(BF16)
        wt_scr[q_rank + kv_rank:q_rank + kv_rank + MLA_DR, :] = wkr_ref[...].astype(BF16)
        wt_scr[q_rank + kv_rank + MLA_DR:, :] = jnp.zeros((LANES - MLA_DR, wt_scr.shape[1]), BF16)

    pr = x_ref.shape[1] // row_pieces
    for i in range(row_pieces):
        rs = slice(i * pr, (i + 1) * pr)
        h = _norm_mod(x_ref[0, rs, :], g_ref[...], mod_ref[0, SHIFT1:SHIFT1 + 1, :],
                      mod_ref[0, SCALE1:SCALE1 + 1, :]).astype(BF16)
        h_out[0, rs, :] = h
        zt = _dot_nt(h, wt_scr[...])
        cq = zt[:, :q_rank]
        ckv = zt[:, q_rank:q_rank + kv_rank]
        kr = zt[:, q_rank + kv_rank:]
        cqn = (cq * lax.rsqrt(jnp.mean(cq * cq, axis=-1, keepdims=True) + EPS) * gq_ref[...]).astype(BF16)
        ckvn = (ckv * lax.rsqrt(jnp.mean(ckv * ckv, axis=-1, keepdims=True) + EPS) * gkv_ref[...]).astype(BF16)
        q = _dot(cqn, wq_ref[...])
        kn = _dot(ckvn, wkn_ref[...])
        v_out[0, rs, :] = _dot(ckvn, wv_ref[...]).astype(BF16)
        if rope:
            cos = cos_ref[rs, :]
            sin = sin_ref[rs, :]
            kr = _rope_half_padded(kr, cos, sin)
        kr = kr.astype(BF16)
        for hd in range(heads):
            base = hd * 2 * HEAD_DIM
            qn = q[:, base:base + HEAD_DIM]
            qr = q[:, base + HEAD_DIM:base + 2 * HEAD_DIM]
            if rope:
                qr = _rope_half_padded(qr, cos, sin)
            q_out[0, rs, base:base + HEAD_DIM] = (qn * q_scale).astype(BF16)
            q_out[0, rs, base + HEAD_DIM:base + 2 * HEAD_DIM] = (qr * q_scale).astype(BF16)
            k_out[0, rs, base:base + HEAD_DIM] = kn[:, hd * HEAD_DIM:(hd + 1) * HEAD_DIM].astype(BF16)
            k_out[0, rs, base + HEAD_DIM:base + 2 * HEAD_DIM] = kr


def _inproj_mla(x, mod, mod_row, g, w_in_t, tail_row, gq, wq, gkv, wkn, wv, rope_tabs, heads):
    b, l, d = x.shape
    tm = _tile("inproj_mla_tm", l)
    q_rank, kv_rank = gq.shape[1], gkv.shape[1]
    rope = rope_tabs is not None
    full = lambda a: pl.BlockSpec(a.shape, lambda bi, m: (0,) * a.ndim)
    in_specs = [pl.BlockSpec((1, tm, d), lambda bi, m: (bi, m, 0)),
                pl.BlockSpec((1, 6, d), _mod_index(mod_row, 0)), full(g)]
    args = [x, mod, g]
    if rope:
        in_specs += [pl.BlockSpec((tm, LANES), lambda bi, m: (m, 0))] * 2
        args += list(rope_tabs)
    for start, rows in ((tail_row, q_rank), (tail_row + q_rank, kv_rank), (tail_row + q_rank + kv_rank, MLA_DR)):
        assert start % rows == 0, (start, rows)
        in_specs.append(pl.BlockSpec((rows, d), functools.partial(lambda blk, bi, m: (blk, 0), start // rows)))
        args.append(w_in_t)
    for a in (gq, wq, gkv, wkn, wv):
        in_specs.append(full(a))
        args.append(a)
    qw = heads * 2 * HEAD_DIM
    vw = heads * HEAD_DIM
    tail = q_rank + kv_rank + LANES
    out_spec = lambda w: pl.BlockSpec((1, tm, w), lambda bi, m: (bi, m, 0))
    wbytes = sum(a.size * a.dtype.itemsize for a in (wq, wkn, wv))
    vmem = (2 * tm * d * 4 + 2 * tail * d * 4 + tail * d * 2 + 2 * wbytes + 2 * tm * (d + 2 * qw + vw) * 2
            + tm * (3 * d * 4 + 6 * qw * 4) + (4 << 20))
    return pl.pallas_call(
        functools.partial(_inproj_mla_kernel, rope=rope, heads=heads, q_rank=q_rank, kv_rank=kv_rank,
                          q_scale=(HEAD_DIM + MLA_DR) ** -0.5 * LOG2_E,
                          row_pieces=max(1, tm // TILES["inproj_mla_piece_rows"])),
        grid=(b, l // tm),
        in_specs=in_specs,
        out_specs=[out_spec(d), out_spec(qw), out_spec(qw), out_spec(vw)],
        out_shape=[jax.ShapeDtypeStruct((b, l, d), BF16), jax.ShapeDtypeStruct((b, l, qw), BF16),
                   jax.ShapeDtypeStruct((b, l, qw), BF16), jax.ShapeDtypeStruct((b, l, vw), BF16)],
        scratch_shapes=[pltpu.VMEM((tail, d), BF16)],
        compiler_params=_cparams(("arbitrary", "arbitrary"), vmem),
        name="inproj_mla_rope" if rope else "inproj_mla",
    )(*args)


def _log_sigmoid(x):
    return jnp.minimum(x, 0.0) - jnp.log1p(jnp.exp(-jnp.abs(x)))


def _retention_kernel(dec_ref, ql_ref, kl_ref, vl_ref, gl_ref, qc_ref, kc_ref, vc_ref, gc_ref, ol_ref, oc_ref,
                      st_scr, *, n_ctx, n_lat, c):
    dk = HEAD_DIM
    hd = pl.program_id(1)
    lf = _log_sigmoid(jnp.full((c, c), dec_ref[0, hd], F32))
    lb = _log_sigmoid(jnp.full((c, c), dec_ref[1, hd], F32))
    ii = lax.broadcasted_iota(jnp.int32, (c, c), 0).astype(F32)
    jj = lax.broadcasted_iota(jnp.int32, (c, c), 1).astype(F32)
    diff = ii - jj
    fwd = diff >= 0
    decay = jnp.where(fwd, jnp.exp(jnp.where(fwd, diff, 0.0) * lf), jnp.exp(jnp.where(fwd, 0.0, -diff) * lb))
    ri = lax.broadcasted_iota(jnp.int32, (c, dk), 0).astype(F32)
    lfr = _log_sigmoid(jnp.full((c, dk), dec_ref[0, hd], F32))
    lbr = _log_sigmoid(jnp.full((c, dk), dec_ref[1, hd], F32))
    q_dec_f = jnp.exp((ri + 1.0) * lfr)
    k_dec_f = jnp.exp((c - 1.0 - ri) * lfr)
    q_dec_b = jnp.exp((c - ri) * lbr)
    k_dec_b = jnp.exp(ri * lbr)
    c_dec_f = jnp.exp(c * _log_sigmoid(jnp.full((dk, dk), dec_ref[0, hd], F32)))
    c_dec_b = jnp.exp(c * _log_sigmoid(jnp.full((dk, dk), dec_ref[1, hd], F32)))

    blocks = [(qc_ref, kc_ref, vc_ref, gc_ref, oc_ref, i) for i in range(n_ctx)]
    blocks += [(ql_ref, kl_ref, vl_ref, gl_ref, ol_ref, i) for i in range(n_lat)]

    def rows(ref, i):
        return ref[0, i * c:(i + 1) * c, :]

    for g, (_, k_ref, v_ref, _, _, i) in enumerate(blocks):
        kf = rows(k_ref, i).astype(F32)
        kk = jnp.concatenate([(kf * k_dec_f).astype(BF16), (kf * k_dec_b).astype(BF16)], axis=1)
        st_scr[g] = _dot_tn(kk, rows(v_ref, i))

    s = jnp.zeros((dk, dk), F32)
    for g in range(n_ctx + n_lat):
        kv = st_scr[g, :dk, :]
        st_scr[g, :dk, :] = s
        s = s * c_dec_f + kv
    s = jnp.zeros((dk, dk), F32)
    for g in list(range(n_ctx - 1, -1, -1)) + list(range(n_ctx + n_lat - 1, n_ctx - 1, -1)):
        kv = st_scr[g, dk:, :]
        st_scr[g, dk:, :] = s
        s = s * c_dec_b + kv

    for g, (q_ref, k_ref, v_ref, g_ref, o_ref, i) in enumerate(blocks):
        q = rows(q_ref, i)
        p = (_dot_nt(q, rows(k_ref, i)) * decay).astype(BF16)
        qf = q.astype(F32)
        qq = jnp.concatenate([(qf * q_dec_f).astype(BF16), (qf * q_dec_b).astype(BF16)], axis=1)
        o = _dot(p, rows(v_ref, i)) + _dot(qq, st_scr[g].astype(BF16))
        o = o * lax.rsqrt(jnp.mean(o * o, axis=-1, keepdims=True) + EPS)
        o_ref[0, i * c:(i + 1) * c, :] = (o * _silu(rows(g_ref, i).astype(F32))).astype(BF16)


def _retention(dec, ret_lat, ret_ctx, heads):
    b, l, _ = ret_lat.shape
    lc = ret_ctx.shape[1]
    c = MXU_DIM if (l % MXU_DIM == 0 and lc % MXU_DIM == 0) else RET_CHUNK
    n_lat, n_ctx = l // c, lc // c
    col = lambda rows, group: pl.BlockSpec((1, rows, HEAD_DIM), lambda bi, h: (bi, 0, group * heads + h))
    in_specs = [pl.BlockSpec(memory_space=pltpu.SMEM)]
    in_specs += [col(l, grp) for grp in range(4)] + [col(lc, grp) for grp in range(4)]
    out_col = lambda rows: pl.BlockSpec((1, rows, HEAD_DIM), lambda bi, h: (bi, 0, h))
    vmem = 2 * 5 * (l + lc) * HEAD_DIM * 2 + (n_lat + n_ctx) * 2 * HEAD_DIM * HEAD_DIM * 4 + (12 << 20)
    return pl.pallas_call(
        functools.partial(_retention_kernel, n_ctx=n_ctx, n_lat=n_lat, c=c),
        grid=(b, heads),
        in_specs=in_specs,
        out_specs=[out_col(l), out_col(lc)],
        out_shape=[jax.ShapeDtypeStruct((b, l, heads * HEAD_DIM), BF16),
                   jax.ShapeDtypeStruct((b, lc, heads * HEAD_DIM), BF16)],
        scratch_shapes=[pltpu.VMEM((n_lat + n_ctx, 2 * HEAD_DIM, HEAD_DIM), F32)],
        compiler_params=_cparams(("parallel", "parallel"), vmem),
        name="retention",
    )(dec, ret_lat, ret_lat, ret_lat, ret_lat, ret_ctx, ret_ctx, ret_ctx, ret_ctx)


def _attn_kernel(*refs, with_lat, tq):
    if with_lat:
        q_ref, kc_ref, vc_ref, kl_ref, vl_ref, o_ref = refs
    else:
        q_ref, kc_ref, vc_ref, o_ref = refs
    kw = 2 * HEAD_DIM
    for hp in range(o_ref.shape[2] // HEAD_DIM):
        kcols = slice(hp * kw, (hp + 1) * kw)
        vcols = slice(hp * HEAD_DIM, (hp + 1) * HEAD_DIM)
        for i in range(q_ref.shape[1] // tq):
            q = q_ref[0, i * tq:(i + 1) * tq, kcols]
            sc = _dot_nt(q, kc_ref[0, :, kcols])
            m = jnp.max(sc, axis=-1, keepdims=True)
            if with_lat:
                sl = _dot_nt(q, kl_ref[0, :, kcols])
                m = jnp.maximum(m, jnp.max(sl, axis=-1, keepdims=True))
            pc = jnp.exp2(sc - m)
            den = jnp.sum(pc, axis=-1, keepdims=True)
            o = _dot(pc.astype(BF16), vc_ref[0, :, vcols])
            if with_lat:
                p_lat = jnp.exp2(sl - m)
                den = den + jnp.sum(p_lat, axis=-1, keepdims=True)
                o = o + _dot(p_lat.astype(BF16), vl_ref[0, :, vcols])
            o_ref[0, i * tq:(i + 1) * tq, vcols] = (o / den).astype(BF16)


def _attention(q, k_ctx, v_ctx, k_lat, v_lat, heads):
    b, lq, _ = q.shape
    tq = _tile("attn_tq", lq)
    hps = _tile("attn_heads_per_step", heads)
    lc = k_ctx.shape[1]
    with_lat = k_lat is not None
    kw = 2 * HEAD_DIM
    head = lambda rows, w: pl.BlockSpec((1, rows, hps * w), lambda bi, h: (bi, 0, h))
    in_specs = [head(lq, kw), head(lc, kw), head(lc, HEAD_DIM)]
    args = [q, k_ctx, v_ctx]
    lk = lc
    if with_lat:
        ll = k_lat.shape[1]
        lk += ll
        in_specs += [head(ll, kw), head(ll, HEAD_DIM)]
        args += [k_lat, v_lat]
    vmem = 2 * hps * (lq * kw + lk * (kw + HEAD_DIM) + lq * HEAD_DIM) * 2 + 8 * tq * lk * 4 + (8 << 20)
    return pl.pallas_call(
        functools.partial(_attn_kernel, with_lat=with_lat, tq=tq),
        grid=(b, heads // hps),
        in_specs=in_specs,
        out_specs=head(lq, HEAD_DIM),
        out_shape=jax.ShapeDtypeStruct((b, lq, heads * HEAD_DIM), BF16),
        compiler_params=_cparams(("parallel", "parallel"), vmem),
        name="mla_attn" if with_lat else "mla_attn_ctx",
    )(*args)


def _mix_out_kernel(a1_ref, a2_ref, w_ref, x_ref, mod_ref, g2_ref, x_out, h_out, *, row_pieces):
    k1 = a1_ref.shape[2]
    pr = x_ref.shape[1] // row_pieces
    for i in range(row_pieces):
        rs = slice(i * pr, (i + 1) * pr)
        y = _dot(a1_ref[0, rs, :], w_ref[:k1, :]) + _dot(a2_ref[0, rs, :], w_ref[k1:, :])
        x1 = x_ref[0, rs, :] + mod_ref[0, GATE1:GATE1 + 1, :] * y
        x_out[0, rs, :] = x1
        h_out[0, rs, :] = _norm_mod(x1, g2_ref[...], mod_ref[0, SHIFT2:SHIFT2 + 1, :],
                                    mod_ref[0, SCALE2:SCALE2 + 1, :]).astype(BF16)


def _mix_out(a1, a2, w_bf16, x, mod, mod_row, g2):
    b, l, d = x.shape
    tm = _tile("mix_out_tm", l)
    k1, k2 = a1.shape[2], a2.shape[2]
    row = lambda w: pl.BlockSpec((1, tm, w), lambda bi, m: (bi, m, 0))
    vmem = (k1 + k2) * d * 2 + 2 * tm * (k1 + k2) * 2 + 4 * tm * d * 4 + 2 * tm * d * 2 + 6 * tm * d * 4 + (4 << 20)
    return pl.pallas_call(
        functools.partial(_mix_out_kernel, row_pieces=max(1, tm // TILES["mix_out_piece_rows"])),
        grid=(b, l // tm),
        in_specs=[row(k1), row(k2),
                  pl.BlockSpec((k1 + k2, d), lambda bi, m: (0, 0)),
                  row(d),
                  pl.BlockSpec((1, 6, d), _mod_index(mod_row, 0)),
                  pl.BlockSpec((1, d), lambda bi, m: (0, 0))],
        out_specs=[row(d), row(d)],
        out_shape=[jax.ShapeDtypeStruct((b, l, d), F32), jax.ShapeDtypeStruct((b, l, d), BF16)],
        compiler_params=_cparams(("parallel", "parallel"), vmem),
        name="mix_out",
    )(a1, a2, w_bf16, x, mod, g2)


def _ffn_up_kernel(h_ref, wa_ref, wg_ref, cwa_ref, cwg_ref, cba_ref, cbg_ref, wd_ref, o_ref, wd_out,
                   wa_scr, wg_scr, *, row_pieces):
    @pl.when(pl.program_id(1) == 0)
    def _():
        wa_scr[...] = wa_ref[...].astype(BF16)
        wg_scr[...] = wg_ref[...].astype(BF16)

    wd_out[...] = wd_ref[...].astype(BF16)

    l = h_ref.shape[1]
    pr = l // row_pieces
    sub = lax.broadcasted_iota(jnp.int32, (SUBLANES, 1), 0)

    def conv(w_scr, cw_ref, cb_ref):
        u = jnp.concatenate([_dot(h_ref[0, i * pr:(i + 1) * pr, :], w_scr[...]) for i in range(row_pieces)], axis=0)
        prev = pltpu.roll(u, 1, 0)
        nxt = pltpu.roll(u, l - 1, 0)
        prev = jnp.concatenate([jnp.where(sub == 0, 0.0, prev[:SUBLANES]), prev[SUBLANES:]], axis=0)
        nxt = jnp.concatenate([nxt[:-SUBLANES], jnp.where(sub == SUBLANES - 1, 0.0, nxt[-SUBLANES:])], axis=0)
        cw = cw_ref[...]
        return cw[0:1, :] * prev + cw[1:2, :] * u + cw[2:3, :] * nxt + cb_ref[...]

    a = conv(wa_scr, cwa_ref, cba_ref)
    gt = conv(wg_scr, cwg_ref, cbg_ref)
    o_ref[0] = (_silu(gt) * a).astype(BF16)


def _ffn_up(h, w_up, conv_w, conv_b, w_down, layer):
    b, l, d = h.shape
    d_out = w_down.shape[2]
    f = w_up.shape[2] // 2
    tn = _tile("ffn_up_tn", f)
    n_n = f // tn
    assert l <= TILES["ffn_up_max_rows"], "the row tile is the whole sequence"
    row_pieces = max(1, l // TILES["ffn_up_piece_rows"])
    assert l % (row_pieces * BF16_TILE_ROWS) == 0, (l, row_pieces)
    col = lambda rows, off: pl.BlockSpec((None, rows, tn), lambda n, bi: (layer, 0, n + off))
    band = f // (n_n * b)
    assert band * n_n * b == f and band % BF16_TILE_ROWS == 0, (f, band)
    vmem = (2 * l * d * 2 + 4 * d * tn * 4 + 2 * d * tn * 2 + 2 * band * d_out * 6
            + 2 * l * tn * 2 + 10 * l * tn * 4 + (4 << 20))
    return pl.pallas_call(
        functools.partial(_ffn_up_kernel, row_pieces=row_pieces),
        grid=(n_n, b),
        in_specs=[
            pl.BlockSpec((1, l, d), lambda n, bi: (bi, 0, 0)),
            col(d, 0), col(d, n_n), col(3, 0), col(3, n_n), col(1, 0), col(1, n_n),
            pl.BlockSpec((None, band, d_out), lambda n, bi: (layer, n * b + bi, 0)),
        ],
        out_specs=[pl.BlockSpec((1, l, tn), lambda n, bi: (bi, 0, n)),
                   pl.BlockSpec((band, d_out), lambda n, bi: (n * b + bi, 0))],
        out_shape=[jax.ShapeDtypeStruct((b, l, f), BF16), jax.ShapeDtypeStruct((f, d_out), BF16)],
        scratch_shapes=[pltpu.VMEM((d, tn), BF16), pltpu.VMEM((d, tn), BF16)],
        compiler_params=_cparams(("arbitrary", "arbitrary"), vmem),
        name="ffn_up",
    )(h, w_up, w_up, conv_w, conv_w, conv_b, conv_b, w_down)


def _ffn_down_kernel(a_ref, w_ref, x_ref, mod_ref, o_ref):
    y = _dot(a_ref[0], w_ref[...])
    o_ref[0] = x_ref[0] + mod_ref[0, GATE2:GATE2 + 1, :] * y


def _ffn_down(act, w_bf16, x, mod, mod_row, n_col_tiles=None):
    b, l, d = x.shape
    tm = _tile("ffn_down_tm", l)
    tn = _tile("ffn_down_tn", d)
    f = act.shape[2]
    if n_col_tiles is None:
        n_col_tiles = d // tn
    vmem = 2 * tm * f * 2 + 2 * f * tn * 2 + 5 * tm * tn * 4 + (6 << 20)
    return pl.pallas_call(
        _ffn_down_kernel,
        grid=(n_col_tiles, b, l // tm),
        in_specs=[
            pl.BlockSpec((1, tm, f), lambda n, bi, m: (bi, m, 0)),
            pl.BlockSpec((f, tn), lambda n, bi, m: (0, n)),
            pl.BlockSpec((1, tm, tn), lambda n, bi, m: (bi, m, n)),
            pl.BlockSpec((1, 6, tn), _mod_index(mod_row, 1, col_axis=0)),
        ],
        out_specs=pl.BlockSpec((1, tm, tn), lambda n, bi, m: (bi, m, n)),
        out_shape=jax.ShapeDtypeStruct((b, l, n_col_tiles * tn), F32),
        compiler_params=_cparams(("parallel", "parallel", "parallel"), vmem),
        name="ffn_down",
    )(act, w_bf16, x, mod)


def _ffn_down_norm_kernel(a_ref, w_ref, x_ref, left_ref, mod_ref, g_ref, o_ref):
    y = _dot(a_ref[0], w_ref[...])
    right = x_ref[0] + mod_ref[0, GATE2:GATE2 + 1, :] * y
    left = left_ref[0]
    wl = left.shape[1]
    d = wl + right.shape[1]
    ms = (jnp.sum(left * left, axis=-1, keepdims=True) + jnp.sum(right * right, axis=-1, keepdims=True)) / d
    r = lax.rsqrt(ms + EPS)
    o_ref[0, :, :wl] = left * r * g_ref[:, :wl]
    o_ref[0, :, wl:] = right * r * g_ref[:, wl:]


def _ffn_down_norm(act, w_bf16, x, left, mod, mod_row, g):
    b, l, d = x.shape
    tm = _tile("ffn_down_tm", l)
    tn = _tile("ffn_down_tn", d)
    f = act.shape[2]
    last = d // tn - 1
    vmem = 2 * tm * f * 2 + 2 * f * tn * 2 + 6 * tm * d * 4 + (6 << 20)
    return pl.pallas_call(
        _ffn_down_norm_kernel,
        grid=(b, l // tm),
        in_specs=[
            pl.BlockSpec((1, tm, f), lambda bi, m: (bi, m, 0)),
            pl.BlockSpec((f, tn), lambda bi, m: (0, last)),
            pl.BlockSpec((1, tm, tn), lambda bi, m: (bi, m, last)),
            pl.BlockSpec((1, tm, d - tn), lambda bi, m: (bi, m, 0)),
            pl.BlockSpec((1, 6, tn), lambda bi, m: (bi if mod_row is None else mod_row, 0, last)),
            pl.BlockSpec((1, d), lambda bi, m: (0, 0)),
        ],
        out_specs=pl.BlockSpec((1, tm, d), lambda bi, m: (bi, m, 0)),
        out_shape=jax.ShapeDtypeStruct((b, l, d), F32),
        compiler_params=_cparams(("parallel", "parallel"), vmem),
        name="ffn_down_norm",
    )(act, w_bf16, x, left, mod, g)


def _pool_kernel(xm_ref, xp_ref, xn_ref, mod_ref, g_ref, w_ref, ps_ref, g2_ref, x_out, h_out, x1_scr, w_scr,
                 *, tm, n_m, seq_len):
    m = pl.program_id(1)

    @pl.when((pl.program_id(0) == 0) & (m == 0))
    def _():
        w_scr[...] = w_ref[...].astype(BF16)

    g = g_ref[...]
    shift = mod_ref[0, SHIFT1:SHIFT1 + 1, :]
    scale = mod_ref[0, SCALE1:SCALE1 + 1, :]
    xm = xm_ref[0]
    hp = _norm_mod(xp_ref[0], g, shift, scale) * (m > 0).astype(F32)
    hn = _norm_mod(xn_ref[0], g, shift, scale) * (m < n_m - 1).astype(F32)
    hh = jnp.concatenate([hp, _norm_mod(xm, g, shift, scale), hn], axis=0)
    rows = tm + 2 * HALO
    t = (m * tm + lax.broadcasted_iota(jnp.int32, (tm, 1), 0)).astype(F32)
    gw = hh.shape[1] // len(POOL_WINDOWS)
    out_gain = mod_ref[0, GATE1:GATE1 + 1, :] * ps_ref[...]
    for gi, w in enumerate(POOL_WINDOWS):
        sl = slice(gi * gw, (gi + 1) * gw)
        hg = hh[:, sl]
        acc = hg
        span = 1
        while span < w // 2:
            acc = acc + pltpu.roll(acc, rows - span, 0)
            span *= 2
        acc = acc + pltpu.roll(acc, w // 2, 0)
        win = acc[HALO:HALO + tm]
        cnt = jnp.minimum(t + (w // 2 - 1), seq_len - 1.0) - jnp.maximum(t - w // 2, 0.0) + 1.0
        p = (win / cnt - hg[HALO:HALO + tm]).astype(BF16)
        x1_scr[:, sl] = xm[:, sl] + out_gain[:, sl] * _dot(p, w_scr[gi])
    x1 = x1_scr[...]
    x_out[0] = x1
    h_out[0] = _norm_mod(x1, g2_ref[...], mod_ref[0, SHIFT2:SHIFT2 + 1, :],
                         mod_ref[0, SCALE2:SCALE2 + 1, :]).astype(BF16)


def _pool_mix(x, mod, mod_row, g, pool_w, pool_scale, g2):
    b, l, d = x.shape
    tm = _tile("pool_tm", l)
    n_m = l // tm
    hb = tm // HALO
    ng, gw, _ = pool_w.shape
    row = pl.BlockSpec((1, tm, d), lambda bi, m: (bi, m, 0))
    vec = pl.BlockSpec((1, d), lambda bi, m: (0, 0))
    vmem = 7 * tm * d * 4 + 2 * ng * gw * gw * 4 + 8 * (tm + 2 * HALO) * d * 4 + (6 << 20)
    return pl.pallas_call(
        functools.partial(_pool_kernel, tm=tm, n_m=n_m, seq_len=l),
        grid=(b, n_m),
        in_specs=[
            row,
            pl.BlockSpec((1, HALO, d), lambda bi, m: (bi, jnp.maximum(m * hb - 1, 0), 0)),
            pl.BlockSpec((1, HALO, d), lambda bi, m: (bi, jnp.minimum((m + 1) * hb, l // HALO - 1), 0)),
            pl.BlockSpec((1, 6, d), _mod_index(mod_row, 0)),
            vec,
            pl.BlockSpec((ng, gw, gw), lambda bi, m: (0, 0, 0)),
            vec, vec,
        ],
        out_specs=[row, row],
        out_shape=[jax.ShapeDtypeStruct((b, l, d), F32), jax.ShapeDtypeStruct((b, l, d), BF16)],
        scratch_shapes=[pltpu.VMEM((tm, d), F32), pltpu.VMEM((ng, gw, gw), BF16)],
        compiler_params=_cparams(("arbitrary", "arbitrary"), vmem),
        name="pool_mix",
    )(x, x, x, mod, g, pool_w, pool_scale, g2)


def _final_norm_kernel(x_ref, g_ref, o_ref):
    x = x_ref[0]
    o_ref[0] = x * lax.rsqrt(jnp.mean(x * x, axis=-1, keepdims=True) + EPS) * g_ref[...]


def _final_norm(x, g):
    b, l, d = x.shape
    tm = _tile("final_tm", l)
    return pl.pallas_call(
        _final_norm_kernel,
        grid=(b, l // tm),
        in_specs=[pl.BlockSpec((1, tm, d), lambda bi, m: (bi, m, 0)), pl.BlockSpec((1, d), lambda bi, m: (0, 0))],
        out_specs=pl.BlockSpec((1, tm, d), lambda bi, m: (bi, m, 0)),
        out_shape=jax.ShapeDtypeStruct((b, l, d), F32),
        compiler_params=_cparams(("parallel", "parallel"), 6 * tm * d * 4 + (4 << 20)),
        name="final_norm",
    )(x, g)


def _axial_angles(rows, dim):
    row = jnp.broadcast_to(jnp.arange(rows, dtype=F32)[:, None], (rows, GRID_W)).reshape(-1)
    col = jnp.broadcast_to(jnp.arange(GRID_W, dtype=F32)[None, :], (rows, GRID_W)).reshape(-1)
    n_freq = dim // 4
    inv = ROPE_BASE ** (-jnp.arange(n_freq, dtype=F32) / n_freq)
    return jnp.concatenate([row[:, None] * inv, col[:, None] * inv], axis=-1)


def _rope_tables(rows, dim):
    ang = _axial_angles(rows, dim)
    cos, sin = jnp.cos(ang), jnp.sin(ang)
    cos_t = jnp.concatenate([cos, cos], axis=-1)
    sin_t = jnp.concatenate([-sin, sin], axis=-1)
    pad = LANES - dim
    if pad:
        cos_t = jnp.pad(cos_t, ((0, 0), (0, pad)))
        sin_t = jnp.pad(sin_t, ((0, 0), (0, pad)))
    return cos_t, sin_t


def _mla_weights(g_q, w_uq, g_kv, w_ukv, heads, ret_w):
    q_rank, kv_rank = g_q.shape[0], g_kv.shape[0]
    wq = w_uq.reshape(q_rank, heads, HEAD_DIM + MLA_DR)
    wq = jnp.pad(wq, ((0, 0), (0, 0), (0, HEAD_DIM - MLA_DR))).reshape(q_rank, heads * 2 * HEAD_DIM).astype(BF16)
    wkv = w_ukv.reshape(kv_rank, heads, 2 * HEAD_DIM)
    wkn = wkv[:, :, :HEAD_DIM].reshape(kv_rank, ret_w).astype(BF16)
    wv = wkv[:, :, HEAD_DIM:].reshape(kv_rank, ret_w).astype(BF16)
    return g_q[None, :], wq, g_kv[None, :], wkn, wv


def _conv_ffn(h, x, mod, mod_row, w_up, conv_w, conv_b, w_down, layer, final_g=None):
    act, w_down_bf16 = _ffn_up(h, w_up, conv_w, conv_b[:, None, :], w_down, layer)
    d = x.shape[2]
    n_col_tiles = d // _tile("ffn_down_tn", d)
    if final_g is None:
        return _ffn_down(act, w_down_bf16, x, mod, mod_row)
    if n_col_tiles == 1:
        return _final_norm(_ffn_down(act, w_down_bf16, x, mod, mod_row), final_g)
    left = _ffn_down(act, w_down_bf16, x, mod, mod_row, n_col_tiles - 1)
    return _ffn_down_norm(act, w_down_bf16, x, left, mod, mod_row, final_g)


def kernel(x, c, ctx, c_ctx, ada_w, ada_b, norm1_g, norm2_g, ffn_w_up, ffn_conv_w, ffn_conv_b, ffn_w_down, mix_w_in, mla_q_norm_g, mla_w_uq, mla_kv_norm_g, mla_w_ukv, ret_decay_f, ret_decay_b, mix_w_out, pool_w, pool_scale, final_g):
    b, l, d = x.shape
    lc = ctx.shape[1]
    depth = ada_w.shape[0]
    heads = ret_decay_f.shape[1]
    ret_w = heads * HEAD_DIM
    rows = l // GRID_W
    rope_ret = _rope_tables(rows, HEAD_DIM)
    rope_mla = _rope_tables(rows, MLA_DR)

    ctx_row = b
    cc = jnp.concatenate([c, c_ctx[None, :], jnp.zeros((-(b + 1) % 8, d), F32)], axis=0)
    mods = _ada(cc, ada_w, ada_b).reshape(depth, cc.shape[0], 6, d)

    x_lat, x_ctx = x, ctx
    for layer in range(depth):
        j = layer // 2
        with_ctx = layer < depth - 1
        mod = mods[layer]
        g1 = norm1_g[layer][None, :]
        g2 = norm2_g[layer][None, :]
        h_ctx = None
        if layer % 2 == 0:
            w_in_t = jnp.swapaxes(mix_w_in[j], 0, 1)
            mla_w = _mla_weights(mla_q_norm_g[j], mla_w_uq[j], mla_kv_norm_g[j], mla_w_ukv[j], heads, ret_w)
            dec = jnp.stack([ret_decay_f[j], ret_decay_b[j]]).astype(F32)

            h1_lat, q_lat, k_lat, v_lat = _inproj_mla(x_lat, mod, None, g1, w_in_t, 4 * ret_w, *mla_w, rope_mla, heads)
            h1_ctx, q_ctx, k_ctx, v_ctx = _inproj_mla(x_ctx.reshape(1, b * lc, d), mod, ctx_row, g1, w_in_t,
                                                      4 * ret_w, *mla_w, None, heads)
            q_ctx, k_ctx, v_ctx = [a.reshape(b, lc, -1) for a in (q_ctx, k_ctx, v_ctx)]
            ret_lat, w_out_bf16 = _inproj_ret(h1_lat, w_in_t, rope_ret, ret_w, cast_extra=mix_w_out[j])
            ret_ctx = _inproj_ret(h1_ctx, w_in_t, None, ret_w).reshape(b, lc, 4 * ret_w)

            o_ret_lat, o_ret_ctx = _retention(dec, ret_lat, ret_ctx, heads)
            o_mla_lat = _attention(q_lat, k_ctx, v_ctx, k_lat, v_lat, heads)
            x_lat, h_lat = _mix_out(o_ret_lat, o_mla_lat, w_out_bf16, x_lat, mod, None, g2)
            if with_ctx:
                o_mla_ctx = _attention(q_ctx, k_ctx, v_ctx, None, None, heads)
                x_ctx, h_ctx = _mix_out(o_ret_ctx, o_mla_ctx, w_out_bf16, x_ctx, mod, ctx_row, g2)
        else:
            ps = pool_scale[j][None, :]
            if with_ctx:
                x_ctx, h_ctx = _pool_mix(x_ctx, mod, ctx_row, g1, pool_w[j], ps, g2)
            x_lat, h_lat = _pool_mix(x_lat, mod, None, g1, pool_w[j], ps, g2)
        last = layer == depth - 1
        x_lat = _conv_ffn(h_lat, x_lat, mod, None, ffn_w_up, ffn_conv_w, ffn_conv_b, ffn_w_down, layer,
                          final_g=final_g[None, :] if last else None)
        if with_ctx:
            x_ctx = _conv_ffn(h_ctx, x_ctx, mod, ctx_row, ffn_w_up, ffn_conv_w, ffn_conv_b, ffn_w_down, layer)
    return x_lat
```

```python
import functools

import jax
import jax.numpy as jnp
from jax import lax
from jax.experimental import pallas as pl
from jax.experimental.pallas import tpu as pltpu

F32 = jnp.float32
BF16 = jnp.bfloat16

GRID_W = 64
HEAD_DIM = 128
RET_CHUNK = 128
MLA_DR = 64
POOL_WINDOWS = (2, 4, 8, 16)
ROPE_BASE = 10000.0
EPS = 1e-6
LOG2_E = 1.4426950408889634

LANES = 128
MXU_DIM = 256
SUBLANES = 8
BF16_TILE_ROWS = 16
HALO = 16
V7X_VMEM_BYTES = 64 * 1024 * 1024
VMEM_CAP = V7X_VMEM_BYTES - 8 * 1024 * 1024

SHIFT1, SCALE1, GATE1, SHIFT2, SCALE2, GATE2 = range(6)

TILES = dict(
    inproj_ret_tm=2048, inproj_ret_piece_rows=256,
    inproj_mla_tm=512, inproj_mla_piece_rows=256,
    attn_tq=512, attn_heads_per_step=2,
    mix_out_tm=512, mix_out_piece_rows=256,
    ffn_up_tn=256, ffn_up_piece_rows=128, ffn_up_max_rows=2048,
    ffn_down_tm=512, ffn_down_tn=1024,
    pool_tm=512,
    final_tm=512,
    ada_tn=1024,
)


def _tile(name, extent):
    tile = min(extent, TILES[name])
    assert extent % tile == 0, f"{name}: extent {extent} is not a multiple of its tile {tile}"
    return tile


def _cparams(sem, vmem_bytes):
    return pltpu.CompilerParams(dimension_semantics=sem, vmem_limit_bytes=min(int(vmem_bytes), VMEM_CAP))


def _silu(x):
    return x / (1.0 + jnp.exp(-x))


def _norm_mod(x, g, shift, scale):
    gain = g * (1.0 + scale)
    return x * lax.rsqrt(jnp.mean(x * x, axis=-1, keepdims=True) + EPS) * gain + shift


def _dot(a, b):
    return jnp.dot(a, b, preferred_element_type=F32)


def _dot_nt(a, b):
    return lax.dot_general(a, b, (((1,), (1,)), ((), ())), preferred_element_type=F32)


def _dot_tn(a, b):
    return lax.dot_general(a, b, (((0,), (0,)), ((), ())), preferred_element_type=F32)


def _mod_index(mod_row, batch_axis, col_axis=None):
    def index(*ids):
        row = ids[batch_axis] if mod_row is None else mod_row
        return (row, 0, 0 if col_axis is None else ids[col_axis])
    return index


def _ada_kernel(c_ref, w_ref, b_ref, o_ref):
    s = _silu(c_ref[...]).astype(BF16)
    o_ref[0] = _dot(s, w_ref[0].astype(BF16)) + b_ref[0]


def _ada(cc, ada_w, ada_b):
    depth, d, n6 = ada_w.shape
    rows = cc.shape[0]
    tn = _tile("ada_tn", n6)
    return pl.pallas_call(
        _ada_kernel,
        grid=(depth, n6 // tn),
        in_specs=[
            pl.BlockSpec((rows, d), lambda l, n: (0, 0)),
            pl.BlockSpec((1, d, tn), lambda l, n: (l, 0, n)),
            pl.BlockSpec((1, 1, tn), lambda l, n: (l, 0, n)),
        ],
        out_specs=pl.BlockSpec((1, rows, tn), lambda l, n: (l, 0, n)),
        out_shape=jax.ShapeDtypeStruct((depth, rows, n6), F32),
        compiler_params=_cparams(("parallel", "parallel"), 2 * d * tn * 4 + d * tn * 2 + (8 << 20)),
        name="ada_mod",
    )(cc, ada_w, ada_b.reshape(depth, 1, n6))


def _inproj_ret_kernel(*refs, rope, cast_extra, heads_per_tile, k_scale, row_pieces):
    refs = list(refs)
    h_ref = refs.pop(0)
    if rope:
        cos_ref, sin_ref = refs.pop(0), refs.pop(0)
    w_ref = refs.pop(0)
    if cast_extra:
        extra_ref = refs.pop(0)
    o_ref = refs.pop(0)
    if cast_extra:
        extra_out = refs.pop(0)
    (w_scr,) = refs
    n = pl.program_id(0)

    @pl.when((pl.program_id(1) == 0) & (pl.program_id(2) == 0))
    def _():
        w_scr[...] = w_ref[...].astype(BF16)

    if cast_extra:
        extra_out[...] = extra_ref[...].astype(BF16)
    scale = jnp.where(n == 1, k_scale, 1.0).astype(F32)
    if rope:
        rotated = n < 2
        cos = jnp.where(rotated, cos_ref[...], 1.0) * scale
        sin = jnp.where(rotated, sin_ref[...], 0.0) * scale
    pr = h_ref.shape[1] // row_pieces
    for i in range(row_pieces):
        rs = slice(i * pr, (i + 1) * pr)
        z = _dot_nt(h_ref[0, rs, :], w_scr[...])
        if rope:
            for h in range(heads_per_tile):
                zh = z[:, h * HEAD_DIM:(h + 1) * HEAD_DIM]
                r = zh * cos[rs] + pltpu.roll(zh, HEAD_DIM // 2, 1) * sin[rs]
                o_ref[0, rs, h * HEAD_DIM:(h + 1) * HEAD_DIM] = r.astype(BF16)
        else:
            o_ref[0, rs, :] = (z * scale).astype(BF16)


def _inproj_ret(h, w_in_t, rope_tabs, ret_w, cast_extra=None):
    b, l, d = h.shape
    tm = _tile("inproj_ret_tm", l)
    tn = ret_w
    n_tiles = 4
    n_m = l // tm
    rope = rope_tabs is not None
    in_specs = [pl.BlockSpec((1, tm, d), lambda n, bi, m: (bi, m, 0))]
    args = [h]
    if rope:
        in_specs += [pl.BlockSpec((tm, HEAD_DIM), lambda n, bi, m: (m, 0))] * 2
        args += list(rope_tabs)
    in_specs.append(pl.BlockSpec((tn, d), lambda n, bi, m: (n, 0)))
    args.append(w_in_t)
    out_specs = [pl.BlockSpec((1, tm, tn), lambda n, bi, m: (bi, m, n))]
    out_shape = [jax.ShapeDtypeStruct((b, l, n_tiles * tn), BF16)]
    vmem = 2 * tm * d * 2 + 2 * d * tn * 4 + d * tn * 2 + 2 * tm * tn * 2 + 4 * tm * tn * 4 + (6 << 20)
    if cast_extra is not None:
        er, ec = cast_extra.shape
        band = er // (n_tiles * b * n_m)
        assert band * n_tiles * b * n_m == er and band % BF16_TILE_ROWS == 0, (er, band)
        extra_spec = pl.BlockSpec((band, ec), lambda n, bi, m: ((n * b + bi) * n_m + m, 0))
        in_specs.append(extra_spec)
        args.append(cast_extra)
        out_specs.append(extra_spec)
        out_shape.append(jax.ShapeDtypeStruct((er, ec), BF16))
        vmem += 2 * band * ec * 6
    outs = pl.pallas_call(
        functools.partial(_inproj_ret_kernel, rope=rope, cast_extra=cast_extra is not None,
                          heads_per_tile=tn // HEAD_DIM, k_scale=HEAD_DIM ** -0.5,
                          row_pieces=max(1, tm // TILES["inproj_ret_piece_rows"])),
        grid=(n_tiles, b, l // tm),
        in_specs=in_specs,
        out_specs=out_specs,
        out_shape=out_shape,
        scratch_shapes=[pltpu.VMEM((tn, d), BF16)],
        compiler_params=_cparams(("arbitrary", "arbitrary", "arbitrary"), vmem),
        name="inproj_ret_rope" if rope else "inproj_ret",
    )(*args)
    return outs if cast_extra is not None else outs[0]


def _rope_half_padded(x, cos, sin):
    rot = pltpu.roll(x, MLA_DR // 2, 1) + pltpu.roll(x, LANES - MLA_DR // 2, 1)
    return x * cos + rot * sin


def _inproj_mla_kernel(*refs, rope, heads, q_rank, kv_rank, q_scale, row_pieces):
    refs = list(refs)
    x_ref, mod_ref, g_ref = refs[:3]
    del refs[:3]
    if rope:
        cos_ref, sin_ref = refs[:2]
        del refs[:2]
    (wcq_ref, wckv_ref, wkr_ref, gq_ref, wq_ref, gkv_ref, wkn_ref, wv_ref,
     h_out, q_out, k_out, v_out, wt_scr) = refs

    @pl.when((pl.program_id(0) == 0) & (pl.program_id(1) == 0))
    def _():
        wt_scr[0:q_rank, :] = wcq_ref[...].astype(BF16)
        wt_scr[q_rank:q_rank + kv_rank, :] = wckv_ref[...].astype(BF16)
        wt_scr[q_rank + kv_rank:q_rank + kv_rank + MLA_DR, :] = wkr_ref[...].astype(BF16)
        wt_scr[q_rank + kv_rank + MLA_DR:, :] = jnp.zeros((LANES - MLA_DR, wt_scr.shape[1]), BF16)

    pr = x_ref.shape[1] // row_pieces
    for i in range(row_pieces):
        rs = slice(i * pr, (i + 1) * pr)
        h = _norm_mod(x_ref[0, rs, :], g_ref[...], mod_ref[0, SHIFT1:SHIFT1 + 1, :],
                      mod_ref[0, SCALE1:SCALE1 + 1, :]).astype(BF16)
        h_out[0, rs, :] = h
        zt = _dot_nt(h, wt_scr[...])
        cq = zt[:, :q_rank]
        ckv = zt[:, q_rank:q_rank + kv_rank]
        kr = zt[:, q_rank + kv_rank:]
        cqn = (cq * lax.rsqrt(jnp.mean(cq * cq, axis=-1, keepdims=True) + EPS) * gq_ref[...]).astype(BF16)
        ckvn = (ckv * lax.rsqrt(jnp.mean(ckv * ckv, axis=-1, keepdims=True) + EPS) * gkv_ref[...]).astype(BF16)
        q = _dot(cqn, wq_ref[...])
        kn = _dot(ckvn, wkn_ref[...])
        v_out[0, rs, :] = _dot(ckvn, wv_ref[...]).astype(BF16)
        if rope:
            cos = cos_ref[rs, :]
            sin = sin_ref[rs, :]
            kr = _rope_half_padded(kr, cos, sin)
        kr = kr.astype(BF16)
        for hd in range(heads):
            base = hd * 2 * HEAD_DIM
            qn = q[:, base:base + HEAD_DIM]
            qr = q[:, base + HEAD_DIM:base + 2 * HEAD_DIM]
            if rope:
                qr = _rope_half_padded(qr, cos, sin)
            q_out[0, rs, base:base + HEAD_DIM] = (qn * q_scale).astype(BF16)
            q_out[0, rs, base + HEAD_DIM:base + 2 * HEAD_DIM] = (qr * q_scale).astype(BF16)
            k_out[0, rs, base:base + HEAD_DIM] = kn[:, hd * HEAD_DIM:(hd + 1) * HEAD_DIM].astype(BF16)
            k_out[0, rs, base + HEAD_DIM:base + 2 * HEAD_DIM] = kr


def _inproj_mla(x, mod, mod_row, g, w_in_t, tail_row, gq, wq, gkv, wkn, wv, rope_tabs, heads):
    b, l, d = x.shape
    tm = _tile("inproj_mla_tm", l)
    q_rank, kv_rank = gq.shape[1], gkv.shape[1]
    rope = rope_tabs is not None
    full = lambda a: pl.BlockSpec(a.shape, lambda bi, m: (0,) * a.ndim)
    in_specs = [pl.BlockSpec((1, tm, d), lambda bi, m: (bi, m, 0)),
                pl.BlockSpec((1, 6, d), _mod_index(mod_row, 0)), full(g)]
    args = [x, mod, g]
    if rope:
        in_specs += [pl.BlockSpec((tm, LANES), lambda bi, m: (m, 0))] * 2
        args += list(rope_tabs)
    for start, rows in ((tail_row, q_rank), (tail_row + q_rank, kv_rank), (tail_row + q_rank + kv_rank, MLA_DR)):
        assert start % rows == 0, (start, rows)
        in_specs.append(pl.BlockSpec((rows, d), functools.partial(lambda blk, bi, m: (blk, 0), start // rows)))
        args.append(w_in_t)
    for a in (gq, wq, gkv, wkn, wv):
        in_specs.append(full(a))
        args.append(a)
    qw = heads * 2 * HEAD_DIM
    vw = heads * HEAD_DIM
    tail = q_rank + kv_rank + LANES
    out_spec = lambda w: pl.BlockSpec((1, tm, w), lambda bi, m: (bi, m, 0))
    wbytes = sum(a.size * a.dtype.itemsize for a in (wq, wkn, wv))
    vmem = (2 * tm * d * 4 + 2 * tail * d * 4 + tail * d * 2 + 2 * wbytes + 2 * tm * (d + 2 * qw + vw) * 2
            + tm * (3 * d * 4 + 6 * qw * 4) + (4 << 20))
    return pl.pallas_call(
        functools.partial(_inproj_mla_kernel, rope=rope, heads=heads, q_rank=q_rank, kv_rank=kv_rank,
                          q_scale=(HEAD_DIM + MLA_DR) ** -0.5 * LOG2_E,
                          row_pieces=max(1, tm // TILES["inproj_mla_piece_rows"])),
        grid=(b, l // tm),
        in_specs=in_specs,
        out_specs=[out_spec(d), out_spec(qw), out_spec(qw), out_spec(vw)],
        out_shape=[jax.ShapeDtypeStruct((b, l, d), BF16), jax.ShapeDtypeStruct((b, l, qw), BF16),
                   jax.ShapeDtypeStruct((b, l, qw), BF16), jax.ShapeDtypeStruct((b, l, vw), BF16)],
        scratch_shapes=[pltpu.VMEM((tail, d), BF16)],
        compiler_params=_cparams(("arbitrary", "arbitrary"), vmem),
        name="inproj_mla_rope" if rope else "inproj_mla",
    )(*args)


def _log_sigmoid(x):
    return jnp.minimum(x, 0.0) - jnp.log1p(jnp.exp(-jnp.abs(x)))


def _retention_kernel(dec_ref, ql_ref, kl_ref, vl_ref, gl_ref, qc_ref, kc_ref, vc_ref, gc_ref, ol_ref, oc_ref,
                      st_scr, *, n_ctx, n_lat, c):
    dk = HEAD_DIM
    hd = pl.program_id(1)
    lf = _log_sigmoid(jnp.full((c, c), dec_ref[0, hd], F32))
    lb = _log_sigmoid(jnp.full((c, c), dec_ref[1, hd], F32))
    ii = lax.broadcasted_iota(jnp.int32, (c, c), 0).astype(F32)
    jj = lax.broadcasted_iota(jnp.int32, (c, c), 1).astype(F32)
    diff = ii - jj
    fwd = diff >= 0
    decay = jnp.where(fwd, jnp.exp(jnp.where(fwd, diff, 0.0) * lf), jnp.exp(jnp.where(fwd, 0.0, -diff) * lb))
    ri = lax.broadcasted_iota(jnp.int32, (c, dk), 0).astype(F32)
    lfr = _log_sigmoid(jnp.full((c, dk), dec_ref[0, hd], F32))
    lbr = _log_sigmoid(jnp.full((c, dk), dec_ref[1, hd], F32))
    q_dec_f = jnp.exp((ri + 1.0) * lfr)
    k_dec_f = jnp.exp((c - 1.0 - ri) * lfr)
    q_dec_b = jnp.exp((c - ri) * lbr)
    k_dec_b = jnp.exp(ri * lbr)
    c_dec_f = jnp.exp(c * _log_sigmoid(jnp.full((dk, dk), dec_ref[0, hd], F32)))
    c_dec_b = jnp.exp(c * _log_sigmoid(jnp.full((dk, dk), dec_ref[1, hd], F32)))

    blocks = [(qc_ref, kc_ref, vc_ref, gc_ref, oc_ref, i) for i in range(n_ctx)]
    blocks += [(ql_ref, kl_ref, vl_ref, gl_ref, ol_ref, i) for i in range(n_lat)]

    def rows(ref, i):
        return ref[0, i * c:(i + 1) * c, :]

    for g, (_, k_ref, v_ref, _, _, i) in enumerate(blocks):
        kf = rows(k_ref, i).astype(F32)
        kk = jnp.concatenate([(kf * k_dec_f).astype(BF16), (kf * k_dec_b).astype(BF16)], axis=1)
        st_scr[g] = _dot_tn(kk, rows(v_ref, i))

    s = jnp.zeros((dk, dk), F32)
    for g in range(n_ctx + n_lat):
        kv = st_scr[g, :dk, :]
        st_scr[g, :dk, :] = s
        s = s * c_dec_f + kv
    s = jnp.zeros((dk, dk), F32)
    for g in list(range(n_ctx - 1, -1, -1)) + list(range(n_ctx + n_lat - 1, n_ctx - 1, -1)):
        kv = st_scr[g, dk:, :]
        st_scr[g, dk:, :] = s
        s = s * c_dec_b + kv

    for g, (q_ref, k_ref, v_ref, g_ref, o_ref, i) in enumerate(blocks):
        q = rows(q_ref, i)
        p = (_dot_nt(q, rows(k_ref, i)) * decay).astype(BF16)
        qf = q.astype(F32)
        qq = jnp.concatenate([(qf * q_dec_f).astype(BF16), (qf * q_dec_b).astype(BF16)], axis=1)
        o = _dot(p, rows(v_ref, i)) + _dot(qq, st_scr[g].astype(BF16))
        o = o * lax.rsqrt(jnp.mean(o * o, axis=-1, keepdims=True) + EPS)
        o_ref[0, i * c:(i + 1) * c, :] = (o * _silu(rows(g_ref, i).astype(F32))).astype(BF16)


def _retention(dec, ret_lat, ret_ctx, heads):
    b, l, _ = ret_lat.shape
    lc = ret_ctx.shape[1]
    c = MXU_DIM if (l % MXU_DIM == 0 and lc % MXU_DIM == 0) else RET_CHUNK
    n_lat, n_ctx = l // c, lc // c
    col = lambda rows, group: pl.BlockSpec((1, rows, HEAD_DIM), lambda bi, h: (bi, 0, group * heads + h))
    in_specs = [pl.BlockSpec(memory_space=pltpu.SMEM)]
    in_specs += [col(l, grp) for grp in range(4)] + [col(lc, grp) for grp in range(4)]
    out_col = lambda rows: pl.BlockSpec((1, rows, HEAD_DIM), lambda bi, h: (bi, 0, h))
    vmem = 2 * 5 * (l + lc) * HEAD_DIM * 2 + (n_lat + n_ctx) * 2 * HEAD_DIM * HEAD_DIM * 4 + (12 << 20)
    return pl.pallas_call(
        functools.partial(_retention_kernel, n_ctx=n_ctx, n_lat=n_lat, c=c),
        grid=(b, heads),
        in_specs=in_specs,
        out_specs=[out_col(l), out_col(lc)],
        out_shape=[jax.ShapeDtypeStruct((b, l, heads * HEAD_DIM), BF16),
                   jax.ShapeDtypeStruct((b, lc, heads * HEAD_DIM), BF16)],
        scratch_shapes=[pltpu.VMEM((n_lat + n_ctx, 2 * HEAD_DIM, HEAD_DIM), F32)],
        compiler_params=_cparams(("parallel", "parallel"), vmem),
        name="retention",
    )(dec, ret_lat, ret_lat, ret_lat, ret_lat, ret_ctx, ret_ctx, ret_ctx, ret_ctx)


def _attn_kernel(*refs, with_lat, tq):
    if with_lat:
        q_ref, kc_ref, vc_ref, kl_ref, vl_ref, o_ref = refs
    else:
        q_ref, kc_ref, vc_ref, o_ref = refs
    kw = 2 * HEAD_DIM
    lq = q_ref.shape[1]
    n_heads = o_ref.shape[2] // HEAD_DIM
    for hp in range(n_heads):
        kcols = slice(hp * kw, (hp + 1) * kw)
        vcols = slice(hp * HEAD_DIM, (hp + 1) * HEAD_DIM)
        tiles = [(r, tq) for r in range(0, lq, tq)]
        if hp == n_heads - 1 and tq % (2 * BF16_TILE_ROWS) == 0:
            r, rn = tiles.pop()
            tiles += [(r, rn // 2), (r + rn // 2, rn // 2)]
        for r0, rn in tiles:
            q = q_ref[0, r0:r0 + rn, kcols]
            sc = _dot_nt(q, kc_ref[0, :, kcols])
            m = jnp.max(sc, axis=-1, keepdims=True)
            if with_lat:
                sl = _dot_nt(q, kl_ref[0, :, kcols])
                m = jnp.maximum(m, jnp.max(sl, axis=-1, keepdims=True))
            pc = jnp.exp2(sc - m)
            den = jnp.sum(pc, axis=-1, keepdims=True)
            o = _dot(pc.astype(BF16), vc_ref[0, :, vcols])
            if with_lat:
                p_lat = jnp.exp2(sl - m)
                den = den + jnp.sum(p_lat, axis=-1, keepdims=True)
                o = o + _dot(p_lat.astype(BF16), vl_ref[0, :, vcols])
            o_ref[0, r0:r0 + rn, vcols] = (o / den).astype(BF16)


def _attention(q, k_ctx, v_ctx, k_lat, v_lat, heads):
    b, lq, _ = q.shape
    tq = _tile("attn_tq", lq)
    hps = _tile("attn_heads_per_step", heads)
    lc = k_ctx.shape[1]
    with_lat = k_lat is not None
    kw = 2 * HEAD_DIM
    head = lambda rows, w: pl.BlockSpec((1, rows, hps * w), lambda bi, h: (bi, 0, h))
    in_specs = [head(lq, kw), head(lc, kw), head(lc, HEAD_DIM)]
    args = [q, k_ctx, v_ctx]
    lk = lc
    if with_lat:
        ll = k_lat.shape[1]
        lk += ll
        in_specs += [head(ll, kw), head(ll, HEAD_DIM)]
        args += [k_lat, v_lat]
    vmem = 2 * hps * (lq * kw + lk * (kw + HEAD_DIM) + lq * HEAD_DIM) * 2 + 8 * tq * lk * 4 + (8 << 20)
    return pl.pallas_call(
        functools.partial(_attn_kernel, with_lat=with_lat, tq=tq),
        grid=(b, heads // hps),
        in_specs=in_specs,
        out_specs=head(lq, HEAD_DIM),
        out_shape=jax.ShapeDtypeStruct((b, lq, heads * HEAD_DIM), BF16),
        compiler_params=_cparams(("parallel", "parallel"), vmem),
        name="mla_attn" if with_lat else "mla_attn_ctx",
    )(*args)


def _mix_out_kernel(a1_ref, a2_ref, w_ref, x_ref, mod_ref, g2_ref, x_out, h_out, *, row_pieces):
    k1 = a1_ref.shape[2]
    pr = x_ref.shape[1] // row_pieces
    for i in range(row_pieces):
        rs = slice(i * pr, (i + 1) * pr)
        y = _dot(a1_ref[0, rs, :], w_ref[:k1, :]) + _dot(a2_ref[0, rs, :], w_ref[k1:, :])
        x1 = x_ref[0, rs, :] + mod_ref[0, GATE1:GATE1 + 1, :] * y
        x_out[0, rs, :] = x1
        h_out[0, rs, :] = _norm_mod(x1, g2_ref[...], mod_ref[0, SHIFT2:SHIFT2 + 1, :],
                                    mod_ref[0, SCALE2:SCALE2 + 1, :]).astype(BF16)


def _mix_out(a1, a2, w_bf16, x, mod, mod_row, g2):
    b, l, d = x.shape
    tm = _tile("mix_out_tm", l)
    k1, k2 = a1.shape[2], a2.shape[2]
    row = lambda w: pl.BlockSpec((1, tm, w), lambda bi, m: (bi, m, 0))
    vmem = (k1 + k2) * d * 2 + 2 * tm * (k1 + k2) * 2 + 4 * tm * d * 4 + 2 * tm * d * 2 + 6 * tm * d * 4 + (4 << 20)
    return pl.pallas_call(
        functools.partial(_mix_out_kernel, row_pieces=max(1, tm // TILES["mix_out_piece_rows"])),
        grid=(b, l // tm),
        in_specs=[row(k1), row(k2),
                  pl.BlockSpec((k1 + k2, d), lambda bi, m: (0, 0)),
                  row(d),
                  pl.BlockSpec((1, 6, d), _mod_index(mod_row, 0)),
                  pl.BlockSpec((1, d), lambda bi, m: (0, 0))],
        out_specs=[row(d), row(d)],
        out_shape=[jax.ShapeDtypeStruct((b, l, d), F32), jax.ShapeDtypeStruct((b, l, d), BF16)],
        compiler_params=_cparams(("parallel", "parallel"), vmem),
        name="mix_out",
    )(a1, a2, w_bf16, x, mod, g2)


def _ffn_up_kernel(h_ref, wa_ref, wg_ref, cwa_ref, cwg_ref, cba_ref, cbg_ref, wd_ref, o_ref, wd_out,
                   wa_scr, wg_scr, *, row_pieces):
    @pl.when(pl.program_id(1) == 0)
    def _():
        wa_scr[...] = wa_ref[...].astype(BF16)
        wg_scr[...] = wg_ref[...].astype(BF16)

    wd_out[...] = wd_ref[...].astype(BF16)

    l = h_ref.shape[1]
    pr = l // row_pieces
    sub = lax.broadcasted_iota(jnp.int32, (SUBLANES, 1), 0)

    def conv(w_scr, cw_ref, cb_ref):
        u = jnp.concatenate([_dot(h_ref[0, i * pr:(i + 1) * pr, :], w_scr[...]) for i in range(row_pieces)], axis=0)
        prev = pltpu.roll(u, 1, 0)
        nxt = pltpu.roll(u, l - 1, 0)
        prev = jnp.concatenate([jnp.where(sub == 0, 0.0, prev[:SUBLANES]), prev[SUBLANES:]], axis=0)
        nxt = jnp.concatenate([nxt[:-SUBLANES], jnp.where(sub == SUBLANES - 1, 0.0, nxt[-SUBLANES:])], axis=0)
        cw = cw_ref[...]
        return cw[0:1, :] * prev + cw[1:2, :] * u + cw[2:3, :] * nxt + cb_ref[...]

    a = conv(wa_scr, cwa_ref, cba_ref)
    gt = conv(wg_scr, cwg_ref, cbg_ref)
    o_ref[0] = (_silu(gt) * a).astype(BF16)


def _ffn_up(h, w_up, conv_w, conv_b, w_down, layer):
    b, l, d = h.shape
    d_out = w_down.shape[2]
    f = w_up.shape[2] // 2
    tn = _tile("ffn_up_tn", f)
    n_n = f // tn
    assert l <= TILES["ffn_up_max_rows"], "the row tile is the whole sequence"
    row_pieces = max(1, l // TILES["ffn_up_piece_rows"])
    assert l % (row_pieces * BF16_TILE_ROWS) == 0, (l, row_pieces)
    col = lambda rows, off: pl.BlockSpec((None, rows, tn), lambda n, bi: (layer, 0, n + off))
    band = f // (n_n * b)
    assert band * n_n * b == f and band % BF16_TILE_ROWS == 0, (f, band)
    vmem = (2 * l * d * 2 + 4 * d * tn * 4 + 2 * d * tn * 2 + 2 * band * d_out * 6
            + 2 * l * tn * 2 + 10 * l * tn * 4 + (4 << 20))
    return pl.pallas_call(
        functools.partial(_ffn_up_kernel, row_pieces=row_pieces),
        grid=(n_n, b),
        in_specs=[
            pl.BlockSpec((1, l, d), lambda n, bi: (bi, 0, 0)),
            col(d, 0), col(d, n_n), col(3, 0), col(3, n_n), col(1, 0), col(1, n_n),
            pl.BlockSpec((None, band, d_out), lambda n, bi: (layer, n * b + bi, 0)),
        ],
        out_specs=[pl.BlockSpec((1, l, tn), lambda n, bi: (bi, 0, n)),
                   pl.BlockSpec((band, d_out), lambda n, bi: (n * b + bi, 0))],
        out_shape=[jax.ShapeDtypeStruct((b, l, f), BF16), jax.ShapeDtypeStruct((f, d_out), BF16)],
        scratch_shapes=[pltpu.VMEM((d, tn), BF16), pltpu.VMEM((d, tn), BF16)],
        compiler_params=_cparams(("arbitrary", "arbitrary"), vmem),
        name="ffn_up",
    )(h, w_up, w_up, conv_w, conv_w, conv_b, conv_b, w_down)


def _ffn_down_kernel(a_ref, w_ref, x_ref, mod_ref, o_ref):
    y = _dot(a_ref[0], w_ref[...])
    o_ref[0] = x_ref[0] + mod_ref[0, GATE2:GATE2 + 1, :] * y


def _ffn_down(act, w_bf16, x, mod, mod_row, n_col_tiles=None):
    b, l, d = x.shape
    tm = _tile("ffn_down_tm", l)
    tn = _tile("ffn_down_tn", d)
    f = act.shape[2]
    if n_col_tiles is None:
        n_col_tiles = d // tn
    vmem = 2 * tm * f * 2 + 2 * f * tn * 2 + 5 * tm * tn * 4 + (6 << 20)
    return pl.pallas_call(
        _ffn_down_kernel,
        grid=(n_col_tiles, b, l // tm),
        in_specs=[
            pl.BlockSpec((1, tm, f), lambda n, bi, m: (bi, m, 0)),
            pl.BlockSpec((f, tn), lambda n, bi, m: (0, n)),
            pl.BlockSpec((1, tm, tn), lambda n, bi, m: (bi, m, n)),
            pl.BlockSpec((1, 6, tn), _mod_index(mod_row, 1, col_axis=0)),
        ],
        out_specs=pl.BlockSpec((1, tm, tn), lambda n, bi, m: (bi, m, n)),
        out_shape=jax.ShapeDtypeStruct((b, l, n_col_tiles * tn), F32),
        compiler_params=_cparams(("parallel", "parallel", "parallel"), vmem),
        name="ffn_down",
    )(act, w_bf16, x, mod)


def _ffn_down_norm_kernel(a_ref, w_ref, x_ref, left_ref, mod_ref, g_ref, o_ref):
    y = _dot(a_ref[0], w_ref[...])
    right = x_ref[0] + mod_ref[0, GATE2:GATE2 + 1, :] * y
    left = left_ref[0]
    wl = left.shape[1]
    d = wl + right.shape[1]
    ms = (jnp.sum(left * left, axis=-1, keepdims=True) + jnp.sum(right * right, axis=-1, keepdims=True)) / d
    r = lax.rsqrt(ms + EPS)
    o_ref[0, :, :wl] = left * r * g_ref[:, :wl]
    o_ref[0, :, wl:] = right * r * g_ref[:, wl:]


def _ffn_down_norm(act, w_bf16, x, left, mod, mod_row, g):
    b, l, d = x.shape
    tm = _tile("ffn_down_tm", l)
    tn = _tile("ffn_down_tn", d)
    f = act.shape[2]
    last = d // tn - 1
    vmem = 2 * tm * f * 2 + 2 * f * tn * 2 + 6 * tm * d * 4 + (6 << 20)
    return pl.pallas_call(
        _ffn_down_norm_kernel,
        grid=(b, l // tm),
        in_specs=[
            pl.BlockSpec((1, tm, f), lambda bi, m: (bi, m, 0)),
            pl.BlockSpec((f, tn), lambda bi, m: (0, last)),
            pl.BlockSpec((1, tm, tn), lambda bi, m: (bi, m, last)),
            pl.BlockSpec((1, tm, d - tn), lambda bi, m: (bi, m, 0)),
            pl.BlockSpec((1, 6, tn), lambda bi, m: (bi if mod_row is None else mod_row, 0, last)),
            pl.BlockSpec((1, d), lambda bi, m: (0, 0)),
        ],
        out_specs=pl.BlockSpec((1, tm, d), lambda bi, m: (bi, m, 0)),
        out_shape=jax.ShapeDtypeStruct((b, l, d), F32),
        compiler_params=_cparams(("parallel", "parallel"), vmem),
        name="ffn_down_norm",
    )(act, w_bf16, x, left, mod, g)


def _pool_kernel(xm_ref, xp_ref, xn_ref, mod_ref, g_ref, w_ref, ps_ref, g2_ref, x_out, h_out, x1_scr, w_scr,
                 *, tm, n_m, seq_len):
    m = pl.program_id(1)

    @pl.when((pl.program_id(0) == 0) & (m == 0))
    def _():
        w_scr[...] = w_ref[...].astype(BF16)

    g = g_ref[...]
    shift = mod_ref[0, SHIFT1:SHIFT1 + 1, :]
    scale = mod_ref[0, SCALE1:SCALE1 + 1, :]
    xm = xm_ref[0]
    hp = _norm_mod(xp_ref[0], g, shift, scale) * (m > 0).astype(F32)
    hn = _norm_mod(xn_ref[0], g, shift, scale) * (m < n_m - 1).astype(F32)
    hh = jnp.concatenate([hp, _norm_mod(xm, g, shift, scale), hn], axis=0)
    rows = tm + 2 * HALO
    t = (m * tm + lax.broadcasted_iota(jnp.int32, (tm, 1), 0)).astype(F32)
    gw = hh.shape[1] // len(POOL_WINDOWS)
    out_gain = mod_ref[0, GATE1:GATE1 + 1, :] * ps_ref[...]
    for gi, w in enumerate(POOL_WINDOWS):
        sl = slice(gi * gw, (gi + 1) * gw)
        hg = hh[:, sl]
        acc = hg
        span = 1
        while span < w // 2:
            acc = acc + pltpu.roll(acc, rows - span, 0)
            span *= 2
        acc = acc + pltpu.roll(acc, w // 2, 0)
        win = acc[HALO:HALO + tm]
        cnt = jnp.minimum(t + (w // 2 - 1), seq_len - 1.0) - jnp.maximum(t - w // 2, 0.0) + 1.0
        p = (win / cnt - hg[HALO:HALO + tm]).astype(BF16)
        x1_scr[:, sl] = xm[:, sl] + out_gain[:, sl] * _dot(p, w_scr[gi])
    x1 = x1_scr[...]
    x_out[0] = x1
    h_out[0] = _norm_mod(x1, g2_ref[...], mod_ref[0, SHIFT2:SHIFT2 + 1, :],
                         mod_ref[0, SCALE2:SCALE2 + 1, :]).astype(BF16)


def _pool_mix(x, mod, mod_row, g, pool_w, pool_scale, g2):
    b, l, d = x.shape
    tm = _tile("pool_tm", l)
    n_m = l // tm
    hb = tm // HALO
    ng, gw, _ = pool_w.shape
    row = pl.BlockSpec((1, tm, d), lambda bi, m: (bi, m, 0))
    vec = pl.BlockSpec((1, d), lambda bi, m: (0, 0))
    vmem = 7 * tm * d * 4 + 2 * ng * gw * gw * 4 + 8 * (tm + 2 * HALO) * d * 4 + (6 << 20)
    return pl.pallas_call(
        functools.partial(_pool_kernel, tm=tm, n_m=n_m, seq_len=l),
        grid=(b, n_m),
        in_specs=[
            row,
            pl.BlockSpec((1, HALO, d), lambda bi, m: (bi, jnp.maximum(m * hb - 1, 0), 0)),
            pl.BlockSpec((1, HALO, d), lambda bi, m: (bi, jnp.minimum((m + 1) * hb, l // HALO - 1), 0)),
            pl.BlockSpec((1, 6, d), _mod_index(mod_row, 0)),
            vec,
            pl.BlockSpec((ng, gw, gw), lambda bi, m: (0, 0, 0)),
            vec, vec,
        ],
        out_specs=[row, row],
        out_shape=[jax.ShapeDtypeStruct((b, l, d), F32), jax.ShapeDtypeStruct((b, l, d), BF16)],
        scratch_shapes=[pltpu.VMEM((tm, d), F32), pltpu.VMEM((ng, gw, gw), BF16)],
        compiler_params=_cparams(("arbitrary", "arbitrary"), vmem),
        name="pool_mix",
    )(x, x, x, mod, g, pool_w, pool_scale, g2)


def _final_norm_kernel(x_ref, g_ref, o_ref):
    x = x_ref[0]
    o_ref[0] = x * lax.rsqrt(jnp.mean(x * x, axis=-1, keepdims=True) + EPS) * g_ref[...]


def _final_norm(x, g):
    b, l, d = x.shape
    tm = _tile("final_tm", l)
    return pl.pallas_call(
        _final_norm_kernel,
        grid=(b, l // tm),
        in_specs=[pl.BlockSpec((1, tm, d), lambda bi, m: (bi, m, 0)), pl.BlockSpec((1, d), lambda bi, m: (0, 0))],
        out_specs=pl.BlockSpec((1, tm, d), lambda bi, m: (bi, m, 0)),
        out_shape=jax.ShapeDtypeStruct((b, l, d), F32),
        compiler_params=_cparams(("parallel", "parallel"), 6 * tm * d * 4 + (4 << 20)),
        name="final_norm",
    )(x, g)


def _axial_angles(rows, dim):
    row = jnp.broadcast_to(jnp.arange(rows, dtype=F32)[:, None], (rows, GRID_W)).reshape(-1)
    col = jnp.broadcast_to(jnp.arange(GRID_W, dtype=F32)[None, :], (rows, GRID_W)).reshape(-1)
    n_freq = dim // 4
    inv = ROPE_BASE ** (-jnp.arange(n_freq, dtype=F32) / n_freq)
    return jnp.concatenate([row[:, None] * inv, col[:, None] * inv], axis=-1)


def _rope_tables(rows, dim):
    ang = _axial_angles(rows, dim)
    cos, sin = jnp.cos(ang), jnp.sin(ang)
    cos_t = jnp.concatenate([cos, cos], axis=-1)
    sin_t = jnp.concatenate([-sin, sin], axis=-1)
    pad = LANES - dim
    if pad:
        cos_t = jnp.pad(cos_t, ((0, 0), (0, pad)))
        sin_t = jnp.pad(sin_t, ((0, 0), (0, pad)))
    return cos_t, sin_t


def _mla_weights(g_q, w_uq, g_kv, w_ukv, heads, ret_w):
    q_rank, kv_rank = g_q.shape[0], g_kv.shape[0]
    wq = w_uq.reshape(q_rank, heads, HEAD_DIM + MLA_DR)
    wq = jnp.pad(wq, ((0, 0), (0, 0), (0, HEAD_DIM - MLA_DR))).reshape(q_rank, heads * 2 * HEAD_DIM).astype(BF16)
    wkv = w_ukv.reshape(kv_rank, heads, 2 * HEAD_DIM)
    wkn = wkv[:, :, :HEAD_DIM].reshape(kv_rank, ret_w).astype(BF16)
    wv = wkv[:, :, HEAD_DIM:].reshape(kv_rank, ret_w).astype(BF16)
    return g_q[None, :], wq, g_kv[None, :], wkn, wv


def _conv_ffn(h, x, mod, mod_row, w_up, conv_w, conv_b, w_down, layer, final_g=None):
    act, w_down_bf16 = _ffn_up(h, w_up, conv_w, conv_b[:, None, :], w_down, layer)
    d = x.shape[2]
    n_col_tiles = d // _tile("ffn_down_tn", d)
    if final_g is None:
        return _ffn_down(act, w_down_bf16, x, mod, mod_row)
    if n_col_tiles == 1:
        return _final_norm(_ffn_down(act, w_down_bf16, x, mod, mod_row), final_g)
    left = _ffn_down(act, w_down_bf16, x, mod, mod_row, n_col_tiles - 1)
    return _ffn_down_norm(act, w_down_bf16, x, left, mod, mod_row, final_g)


def kernel(x, c, ctx, c_ctx, ada_w, ada_b, norm1_g, norm2_g, ffn_w_up, ffn_conv_w, ffn_conv_b, ffn_w_down, mix_w_in, mla_q_norm_g, mla_w_uq, mla_kv_norm_g, mla_w_ukv, ret_decay_f, ret_decay_b, mix_w_out, pool_w, pool_scale, final_g):
    b, l, d = x.shape
    lc = ctx.shape[1]
    depth = ada_w.shape[0]
    heads = ret_decay_f.shape[1]
    ret_w = heads * HEAD_DIM
    rows = l // GRID_W
    rope_ret = _rope_tables(rows, HEAD_DIM)
    rope_mla = _rope_tables(rows, MLA_DR)

    ctx_row = b
    cc = jnp.concatenate([c, c_ctx[None, :], jnp.zeros((-(b + 1) % 8, d), F32)], axis=0)
    mods = _ada(cc, ada_w, ada_b).reshape(depth, cc.shape[0], 6, d)

    x_lat, x_ctx = x, ctx
    for layer in range(depth):
        j = layer // 2
        with_ctx = layer < depth - 1
        mod = mods[layer]
        g1 = norm1_g[layer][None, :]
        g2 = norm2_g[layer][None, :]
        h_ctx = None
        if layer % 2 == 0:
            w_in_t = jnp.swapaxes(mix_w_in[j], 0, 1)
            mla_w = _mla_weights(mla_q_norm_g[j], mla_w_uq[j], mla_kv_norm_g[j], mla_w_ukv[j], heads, ret_w)
            dec = jnp.stack([ret_decay_f[j], ret_decay_b[j]]).astype(F32)

            h1_lat, q_lat, k_lat, v_lat = _inproj_mla(x_lat, mod, None, g1, w_in_t, 4 * ret_w, *mla_w, rope_mla, heads)
            h1_ctx, q_ctx, k_ctx, v_ctx = _inproj_mla(x_ctx.reshape(1, b * lc, d), mod, ctx_row, g1, w_in_t,
                                                      4 * ret_w, *mla_w, None, heads)
            q_ctx, k_ctx, v_ctx = [a.reshape(b, lc, -1) for a in (q_ctx, k_ctx, v_ctx)]
            ret_lat, w_out_bf16 = _inproj_ret(h1_lat, w_in_t, rope_ret, ret_w, cast_extra=mix_w_out[j])
            ret_ctx = _inproj_ret(h1_ctx, w_in_t, None, ret_w).reshape(b, lc, 4 * ret_w)

            o_ret_lat, o_ret_ctx = _retention(dec, ret_lat, ret_ctx, heads)
            o_mla_lat = _attention(q_lat, k_ctx, v_ctx, k_lat, v_lat, heads)
            x_lat, h_lat = _mix_out(o_ret_lat, o_mla_lat, w_out_bf16, x_lat, mod, None, g2)
            if with_ctx:
                o_mla_ctx = _attention(q_ctx, k_ctx, v_ctx, None, None, heads)
                x_ctx, h_ctx = _mix_out(o_ret_ctx, o_mla_ctx, w_out_bf16, x_ctx, mod, ctx_row, g2)
        else:
            ps = pool_scale[j][None, :]
            if with_ctx:
                x_ctx, h_ctx = _pool_mix(x_ctx, mod, ctx_row, g1, pool_w[j], ps, g2)
            x_lat, h_lat = _pool_mix(x_lat, mod, None, g1, pool_w[j], ps, g2)
        last = layer == depth - 1
        x_lat = _conv_ffn(h_lat, x_lat, mod, None, ffn_w_up, ffn_conv_w, ffn_conv_b, ffn_w_down, layer,
                          final_g=final_g[None, :] if last else None)
        if with_ctx:
            x_ctx = _conv_ffn(h_ctx, x_ctx, mod, ctx_row, ffn_w_up, ffn_conv_w, ffn_conv_b, ffn_w_down, layer)
    return x_lat
```

```python
import functools

import jax
import jax.numpy as jnp
import numpy as np
from jax import lax
from jax.experimental import pallas as pl
from jax.experimental.pallas import tpu as pltpu

F32 = jnp.float32
BF16 = jnp.bfloat16

GRID_W = 64
HEAD_DIM = 128
RET_CHUNK = 128
MLA_DR = 64
POOL_WINDOWS = (2, 4, 8, 16)
ROPE_BASE = 10000.0
EPS = 1e-6
LOG2_E = 1.4426950408889634

LANES = 128
MXU_DIM = 256
SUBLANES = 8
BF16_TILE_ROWS = 16
HALO = 16
V7X_VMEM_BYTES = 64 * 1024 * 1024
VMEM_CAP = V7X_VMEM_BYTES - 8 * 1024 * 1024

SHIFT1, SCALE1, GATE1, SHIFT2, SCALE2, GATE2 = range(6)

TILES = dict(
    inproj_ret_tm=2048, inproj_ret_piece_rows=256,
    inproj_mla_tm=512, inproj_mla_piece_rows=256,
    attn_tq=512, attn_heads_per_step=2,
    mix_out_tm=512, mix_out_piece_rows=256,
    ffn_up_tn=256, ffn_up_piece_rows=128, ffn_up_max_rows=2048,
    ffn_down_tm=512, ffn_down_tn=1024,
    pool_tm=512,
    final_tm=512,
    ada_tn=1024,
)


def _tile(name, extent):
    tile = min(extent, TILES[name])
    assert extent % tile == 0, f"{name}: extent {extent} is not a multiple of its tile {tile}"
    return tile


def _cparams(sem, vmem_bytes):
    return pltpu.CompilerParams(dimension_semantics=sem, vmem_limit_bytes=min(int(vmem_bytes), VMEM_CAP))


def _silu(x):
    return x / (1.0 + jnp.exp(-x))


def _norm_mod(x, g, shift, scale):
    gain = g * (1.0 + scale)
    return x * lax.rsqrt(jnp.mean(x * x, axis=-1, keepdims=True) + EPS) * gain + shift


def _dot(a, b):
    return jnp.dot(a, b, preferred_element_type=F32)


def _dot_nt(a, b):
    return lax.dot_general(a, b, (((1,), (1,)), ((), ())), preferred_element_type=F32)


def _dot_tn(a, b):
    return lax.dot_general(a, b, (((0,), (0,)), ((), ())), preferred_element_type=F32)


def _mod_index(mod_row, batch_axis, col_axis=None):
    def index(*ids):
        row = ids[batch_axis] if mod_row is None else mod_row
        return (row, 0, 0 if col_axis is None else ids[col_axis])
    return index


def _ada_kernel(c_ref, w_ref, b_ref, o_ref):
    s = _silu(c_ref[...]).astype(BF16)
    o_ref[0] = _dot(s, w_ref[0].astype(BF16)) + b_ref[0]


def _ada(cc, ada_w, ada_b):
    depth, d, n6 = ada_w.shape
    rows = cc.shape[0]
    tn = _tile("ada_tn", n6)
    return pl.pallas_call(
        _ada_kernel,
        grid=(depth, n6 // tn),
        in_specs=[
            pl.BlockSpec((rows, d), lambda l, n: (0, 0)),
            pl.BlockSpec((1, d, tn), lambda l, n: (l, 0, n)),
            pl.BlockSpec((1, 1, tn), lambda l, n: (l, 0, n)),
        ],
        out_specs=pl.BlockSpec((1, rows, tn), lambda l, n: (l, 0, n)),
        out_shape=jax.ShapeDtypeStruct((depth, rows, n6), F32),
        compiler_params=_cparams(("parallel", "parallel"), 2 * d * tn * 4 + d * tn * 2 + (8 << 20)),
        name="ada_mod",
    )(cc, ada_w, ada_b.reshape(depth, 1, n6))


def _inproj_ret_kernel(*refs, rope, cast_extra, heads_per_tile, k_scale, row_pieces):
    refs = list(refs)
    h_ref = refs.pop(0)
    if rope:
        cos_ref, sin_ref = refs.pop(0), refs.pop(0)
    w_ref = refs.pop(0)
    if cast_extra:
        extra_ref = refs.pop(0)
    o_ref = refs.pop(0)
    if cast_extra:
        extra_out = refs.pop(0)
    (w_scr,) = refs
    n = pl.program_id(0)

    @pl.when((pl.program_id(1) == 0) & (pl.program_id(2) == 0))
    def _():
        w_scr[...] = w_ref[...].astype(BF16)

    if cast_extra:
        extra_out[...] = extra_ref[...].astype(BF16)
    scale = jnp.where(n == 1, k_scale, 1.0).astype(F32)
    if rope:
        rotated = n < 2
        cos = jnp.where(rotated, cos_ref[...], 1.0) * scale
        sin = jnp.where(rotated, sin_ref[...], 0.0) * scale
    pr = h_ref.shape[1] // row_pieces
    for i in range(row_pieces):
        rs = slice(i * pr, (i + 1) * pr)
        z = _dot_nt(h_ref[0, rs, :], w_scr[...])
        if rope:
            for h in range(heads_per_tile):
                zh = z[:, h * HEAD_DIM:(h + 1) * HEAD_DIM]
                r = zh * cos[rs] + pltpu.roll(zh, HEAD_DIM // 2, 1) * sin[rs]
                o_ref[0, rs, h * HEAD_DIM:(h + 1) * HEAD_DIM] = r.astype(BF16)
        else:
            o_ref[0, rs, :] = (z * scale).astype(BF16)


def _inproj_ret(h, w_in_t, rope_tabs, ret_w, cast_extra=None):
    b, l, d = h.shape
    tm = _tile("inproj_ret_tm", l)
    tn = ret_w
    n_tiles = 4
    n_m = l // tm
    rope = rope_tabs is not None
    in_specs = [pl.BlockSpec((1, tm, d), lambda n, bi, m: (bi, m, 0))]
    args = [h]
    if rope:
        in_specs += [pl.BlockSpec((tm, HEAD_DIM), lambda n, bi, m: (m, 0))] * 2
        args += list(rope_tabs)
    in_specs.append(pl.BlockSpec((tn, d), lambda n, bi, m: (n, 0)))
    args.append(w_in_t)
    out_specs = [pl.BlockSpec((1, tm, tn), lambda n, bi, m: (bi, m, n))]
    out_shape = [jax.ShapeDtypeStruct((b, l, n_tiles * tn), BF16)]
    vmem = 2 * tm * d * 2 + 2 * d * tn * 4 + d * tn * 2 + 2 * tm * tn * 2 + 4 * tm * tn * 4 + (6 << 20)
    if cast_extra is not None:
        er, ec = cast_extra.shape
        band = er // (n_tiles * b * n_m)
        assert band * n_tiles * b * n_m == er and band % BF16_TILE_ROWS == 0, (er, band)
        extra_spec = pl.BlockSpec((band, ec), lambda n, bi, m: ((n * b + bi) * n_m + m, 0))
        in_specs.append(extra_spec)
        args.append(cast_extra)
        out_specs.append(extra_spec)
        out_shape.append(jax.ShapeDtypeStruct((er, ec), BF16))
        vmem += 2 * band * ec * 6
    outs = pl.pallas_call(
        functools.partial(_inproj_ret_kernel, rope=rope, cast_extra=cast_extra is not None,
                          heads_per_tile=tn // HEAD_DIM, k_scale=HEAD_DIM ** -0.5,
                          row_pieces=max(1, tm // TILES["inproj_ret_piece_rows"])),
        grid=(n_tiles, b, l // tm),
        in_specs=in_specs,
        out_specs=out_specs,
        out_shape=out_shape,
        scratch_shapes=[pltpu.VMEM((tn, d), BF16)],
        compiler_params=_cparams(("arbitrary", "arbitrary", "arbitrary"), vmem),
        name="inproj_ret_rope" if rope else "inproj_ret",
    )(*args)
    return outs if cast_extra is not None else outs[0]


def _rope_half_padded(x, cos, sin):
    rot = pltpu.roll(x, MLA_DR // 2, 1) + pltpu.roll(x, LANES - MLA_DR // 2, 1)
    return x * cos + rot * sin


def _inproj_mla_kernel(*refs, rope, heads, q_rank, kv_rank, q_scale, row_pieces):
    refs = list(refs)
    x_ref, mod_ref, g_ref = refs[:3]
    del refs[:3]
    if rope:
        cos_ref, sin_ref = refs[:2]
        del refs[:2]
    (wcq_ref, wckv_ref, wkr_ref, gq_ref, wq_ref, gkv_ref, wkn_ref, wv_ref,
     h_out, q_out, k_out, v_out, wt_scr) = refs

    @pl.when((pl.program_id(0) == 0) & (pl.program_id(1) == 0))
    def _():
        wt_scr[0:q_rank, :] = wcq_ref[...].astype(BF16)
        wt_scr[q_rank:q_rank + kv_rank, :] = wckv_ref[...].astype(BF16)
        wt_scr[q_rank + kv_rank:q_rank + kv_rank + MLA_DR, :] = wkr_ref[...].astype(BF16)
        wt_scr[q_rank + kv_rank + MLA_DR:, :] = jnp.zeros((LANES - MLA_DR, wt_scr.shape[1]), BF16)

    pr = x_ref.shape[1] // row_pieces
    for i in range(row_pieces):
        rs = slice(i * pr, (i + 1) * pr)
        h = _norm_mod(x_ref[0, rs, :], g_ref[...], mod_ref[0, SHIFT1:SHIFT1 + 1, :],
                      mod_ref[0, SCALE1:SCALE1 + 1, :]).astype(BF16)
        h_out[0, rs, :] = h
        zt = _dot_nt(h, wt_scr[...])
        cq = zt[:, :q_rank]
        ckv = zt[:, q_rank:q_rank + kv_rank]
        kr = zt[:, q_rank + kv_rank:]
        cqn = (cq * lax.rsqrt(jnp.mean(cq * cq, axis=-1, keepdims=True) + EPS) * gq_ref[...]).astype(BF16)
        ckvn = (ckv * lax.rsqrt(jnp.mean(ckv * ckv, axis=-1, keepdims=True) + EPS) * gkv_ref[...]).astype(BF16)
        q = _dot(cqn, wq_ref[...])
        kn = _dot(ckvn, wkn_ref[...])
        v_out[0, rs, :] = _dot(ckvn, wv_ref[...]).astype(BF16)
        if rope:
            cos = cos_ref[rs, :]
            sin = sin_ref[rs, :]
            kr = _rope_half_padded(kr, cos, sin)
        kr = kr.astype(BF16)
        for hd in range(heads):
            base = hd * 2 * HEAD_DIM
            qn = q[:, base:base + HEAD_DIM]
            qr = q[:, base + HEAD_DIM:base + 2 * HEAD_DIM]
            if rope:
                qr = _rope_half_padded(qr, cos, sin)
            q_out[0, rs, base:base + HEAD_DIM] = (qn * q_scale).astype(BF16)
            q_out[0, rs, base + HEAD_DIM:base + 2 * HEAD_DIM] = (qr * q_scale).astype(BF16)
            k_out[0, rs, base:base + HEAD_DIM] = kn[:, hd * HEAD_DIM:(hd + 1) * HEAD_DIM].astype(BF16)
            k_out[0, rs, base + HEAD_DIM:base + 2 * HEAD_DIM] = kr


def _inproj_mla(x, mod, mod_row, g, w_in_t, tail_row, gq, wq, gkv, wkn, wv, rope_tabs, heads):
    b, l, d = x.shape
    tm = _tile("inproj_mla_tm", l)
    q_rank, kv_rank = gq.shape[1], gkv.shape[1]
    rope = rope_tabs is not None
    full = lambda a: pl.BlockSpec(a.shape, lambda bi, m: (0,) * a.ndim)
    in_specs = [pl.BlockSpec((1, tm, d), lambda bi, m: (bi, m, 0)),
                pl.BlockSpec((1, 6, d), _mod_index(mod_row, 0)), full(g)]
    args = [x, mod, g]
    if rope:
        in_specs += [pl.BlockSpec((tm, LANES), lambda bi, m: (m, 0))] * 2
        args += list(rope_tabs)
    for start, rows in ((tail_row, q_rank), (tail_row + q_rank, kv_rank), (tail_row + q_rank + kv_rank, MLA_DR)):
        assert start % rows == 0, (start, rows)
        in_specs.append(pl.BlockSpec((rows, d), functools.partial(lambda blk, bi, m: (blk, 0), start // rows)))
        args.append(w_in_t)
    for a in (gq, wq, gkv, wkn, wv):
        in_specs.append(full(a))
        args.append(a)
    qw = heads * 2 * HEAD_DIM
    vw = heads * HEAD_DIM
    tail = q_rank + kv_rank + LANES
    out_spec = lambda w: pl.BlockSpec((1, tm, w), lambda bi, m: (bi, m, 0))
    wbytes = sum(a.size * a.dtype.itemsize for a in (wq, wkn, wv))
    vmem = (2 * tm * d * 4 + 2 * tail * d * 4 + tail * d * 2 + 2 * wbytes + 2 * tm * (d + 2 * qw + vw) * 2
            + tm * (3 * d * 4 + 6 * qw * 4) + (4 << 20))
    return pl.pallas_call(
        functools.partial(_inproj_mla_kernel, rope=rope, heads=heads, q_rank=q_rank, kv_rank=kv_rank,
                          q_scale=(HEAD_DIM + MLA_DR) ** -0.5 * LOG2_E,
                          row_pieces=max(1, tm // TILES["inproj_mla_piece_rows"])),
        grid=(b, l // tm),
        in_specs=in_specs,
        out_specs=[out_spec(d), out_spec(qw), out_spec(qw), out_spec(vw)],
        out_shape=[jax.ShapeDtypeStruct((b, l, d), BF16), jax.ShapeDtypeStruct((b, l, qw), BF16),
                   jax.ShapeDtypeStruct((b, l, qw), BF16), jax.ShapeDtypeStruct((b, l, vw), BF16)],
        scratch_shapes=[pltpu.VMEM((tail, d), BF16)],
        compiler_params=_cparams(("arbitrary", "arbitrary"), vmem),
        name="inproj_mla_rope" if rope else "inproj_mla",
    )(*args)


def _log_sigmoid(x):
    return jnp.minimum(x, 0.0) - jnp.log1p(jnp.exp(-jnp.abs(x)))


def _retention_kernel(dec_ref, ql_ref, kl_ref, vl_ref, gl_ref, qc_ref, kc_ref, vc_ref, gc_ref, ol_ref, oc_ref,
                      st_scr, *, n_ctx, n_lat, c):
    dk = HEAD_DIM
    hd = pl.program_id(1)
    lf = _log_sigmoid(jnp.full((c, c), dec_ref[0, hd], F32))
    lb = _log_sigmoid(jnp.full((c, c), dec_ref[1, hd], F32))
    ii = lax.broadcasted_iota(jnp.int32, (c, c), 0).astype(F32)
    jj = lax.broadcasted_iota(jnp.int32, (c, c), 1).astype(F32)
    diff = ii - jj
    fwd = diff >= 0
    decay = jnp.where(fwd, jnp.exp(jnp.where(fwd, diff, 0.0) * lf), jnp.exp(jnp.where(fwd, 0.0, -diff) * lb))
    ri = lax.broadcasted_iota(jnp.int32, (c, dk), 0).astype(F32)
    lfr = _log_sigmoid(jnp.full((c, dk), dec_ref[0, hd], F32))
    lbr = _log_sigmoid(jnp.full((c, dk), dec_ref[1, hd], F32))
    q_dec_f = jnp.exp((ri + 1.0) * lfr)
    k_dec_f = jnp.exp((c - 1.0 - ri) * lfr)
    q_dec_b = jnp.exp((c - ri) * lbr)
    k_dec_b = jnp.exp(ri * lbr)
    c_dec_f = jnp.exp(c * _log_sigmoid(jnp.full((dk, dk), dec_ref[0, hd], F32)))
    c_dec_b = jnp.exp(c * _log_sigmoid(jnp.full((dk, dk), dec_ref[1, hd], F32)))

    blocks = [(qc_ref, kc_ref, vc_ref, gc_ref, oc_ref, i) for i in range(n_ctx)]
    blocks += [(ql_ref, kl_ref, vl_ref, gl_ref, ol_ref, i) for i in range(n_lat)]

    def rows(ref, i):
        return ref[0, i * c:(i + 1) * c, :]

    for g, (_, k_ref, v_ref, _, _, i) in enumerate(blocks):
        kf = rows(k_ref, i).astype(F32)
        kk = jnp.concatenate([(kf * k_dec_f).astype(BF16), (kf * k_dec_b).astype(BF16)], axis=1)
        st_scr[g] = _dot_tn(kk, rows(v_ref, i))

    s = jnp.zeros((dk, dk), F32)
    for g in range(n_ctx + n_lat):
        kv = st_scr[g, :dk, :]
        st_scr[g, :dk, :] = s
        s = s * c_dec_f + kv
    s = jnp.zeros((dk, dk), F32)
    for g in list(range(n_ctx - 1, -1, -1)) + list(range(n_ctx + n_lat - 1, n_ctx - 1, -1)):
        kv = st_scr[g, dk:, :]
        st_scr[g, dk:, :] = s
        s = s * c_dec_b + kv

    for g, (q_ref, k_ref, v_ref, g_ref, o_ref, i) in enumerate(blocks):
        q = rows(q_ref, i)
        p = (_dot_nt(q, rows(k_ref, i)) * decay).astype(BF16)
        qf = q.astype(F32)
        qq = jnp.concatenate([(qf * q_dec_f).astype(BF16), (qf * q_dec_b).astype(BF16)], axis=1)
        o = _dot(p, rows(v_ref, i)) + _dot(qq, st_scr[g].astype(BF16))
        o = o * lax.rsqrt(jnp.mean(o * o, axis=-1, keepdims=True) + EPS)
        o_ref[0, i * c:(i + 1) * c, :] = (o * _silu(rows(g_ref, i).astype(F32))).astype(BF16)


def _retention(dec, ret_lat, ret_ctx, heads):
    b, l, _ = ret_lat.shape
    lc = ret_ctx.shape[1]
    c = MXU_DIM if (l % MXU_DIM == 0 and lc % MXU_DIM == 0) else RET_CHUNK
    n_lat, n_ctx = l // c, lc // c
    col = lambda rows, group: pl.BlockSpec((1, rows, HEAD_DIM), lambda bi, h: (bi, 0, group * heads + h))
    in_specs = [pl.BlockSpec(memory_space=pltpu.SMEM)]
    in_specs += [col(l, grp) for grp in range(4)] + [col(lc, grp) for grp in range(4)]
    out_col = lambda rows: pl.BlockSpec((1, rows, HEAD_DIM), lambda bi, h: (bi, 0, h))
    vmem = 2 * 5 * (l + lc) * HEAD_DIM * 2 + (n_lat + n_ctx) * 2 * HEAD_DIM * HEAD_DIM * 4 + (12 << 20)
    return pl.pallas_call(
        functools.partial(_retention_kernel, n_ctx=n_ctx, n_lat=n_lat, c=c),
        grid=(b, heads),
        in_specs=in_specs,
        out_specs=[out_col(l), out_col(lc)],
        out_shape=[jax.ShapeDtypeStruct((b, l, heads * HEAD_DIM), BF16),
                   jax.ShapeDtypeStruct((b, lc, heads * HEAD_DIM), BF16)],
        scratch_shapes=[pltpu.VMEM((n_lat + n_ctx, 2 * HEAD_DIM, HEAD_DIM), F32)],
        compiler_params=_cparams(("parallel", "parallel"), vmem),
        name="retention",
    )(dec, ret_lat, ret_lat, ret_lat, ret_lat, ret_ctx, ret_ctx, ret_ctx, ret_ctx)


def _attn_kernel(*refs, with_lat, tq):
    if with_lat:
        q_ref, kc_ref, vc_ref, kl_ref, vl_ref, o_ref = refs
    else:
        q_ref, kc_ref, vc_ref, o_ref = refs
    kw = 2 * HEAD_DIM
    lq = q_ref.shape[1]
    n_heads = o_ref.shape[2] // HEAD_DIM
    for hp in range(n_heads):
        kcols = slice(hp * kw, (hp + 1) * kw)
        vcols = slice(hp * HEAD_DIM, (hp + 1) * HEAD_DIM)
        tiles = [(r, tq) for r in range(0, lq, tq)]
        if hp == n_heads - 1 and tq % (2 * BF16_TILE_ROWS) == 0:
            r, rn = tiles.pop()
            tiles += [(r, rn // 2), (r + rn // 2, rn // 2)]
        for r0, rn in tiles:
            q = q_ref[0, r0:r0 + rn, kcols]
            sc = _dot_nt(q, kc_ref[0, :, kcols])
            m = jnp.max(sc, axis=-1, keepdims=True)
            if with_lat:
                sl = _dot_nt(q, kl_ref[0, :, kcols])
                m = jnp.maximum(m, jnp.max(sl, axis=-1, keepdims=True))
            pc = jnp.exp2(sc - m)
            den = jnp.sum(pc, axis=-1, keepdims=True)
            o = _dot(pc.astype(BF16), vc_ref[0, :, vcols])
            if with_lat:
                p_lat = jnp.exp2(sl - m)
                den = den + jnp.sum(p_lat, axis=-1, keepdims=True)
                o = o + _dot(p_lat.astype(BF16), vl_ref[0, :, vcols])
            o_ref[0, r0:r0 + rn, vcols] = (o / den).astype(BF16)


def _attention(q, k_ctx, v_ctx, k_lat, v_lat, heads):
    b, lq, _ = q.shape
    tq = _tile("attn_tq", lq)
    hps = _tile("attn_heads_per_step", heads)
    lc = k_ctx.shape[1]
    with_lat = k_lat is not None
    kw = 2 * HEAD_DIM
    head = lambda rows, w: pl.BlockSpec((1, rows, hps * w), lambda bi, h: (bi, 0, h))
    in_specs = [head(lq, kw), head(lc, kw), head(lc, HEAD_DIM)]
    args = [q, k_ctx, v_ctx]
    lk = lc
    if with_lat:
        ll = k_lat.shape[1]
        lk += ll
        in_specs += [head(ll, kw), head(ll, HEAD_DIM)]
        args += [k_lat, v_lat]
    vmem = 2 * hps * (lq * kw + lk * (kw + HEAD_DIM) + lq * HEAD_DIM) * 2 + 8 * tq * lk * 4 + (8 << 20)
    return pl.pallas_call(
        functools.partial(_attn_kernel, with_lat=with_lat, tq=tq),
        grid=(b, heads // hps),
        in_specs=in_specs,
        out_specs=head(lq, HEAD_DIM),
        out_shape=jax.ShapeDtypeStruct((b, lq, heads * HEAD_DIM), BF16),
        compiler_params=_cparams(("parallel", "parallel"), vmem),
        name="mla_attn" if with_lat else "mla_attn_ctx",
    )(*args)


def _mix_out_kernel(a1_ref, a2_ref, w_ref, x_ref, mod_ref, g2_ref, x_out, h_out, *, row_pieces):
    k1 = a1_ref.shape[2]
    pr = x_ref.shape[1] // row_pieces
    for i in range(row_pieces):
        rs = slice(i * pr, (i + 1) * pr)
        y = _dot(a1_ref[0, rs, :], w_ref[:k1, :]) + _dot(a2_ref[0, rs, :], w_ref[k1:, :])
        x1 = x_ref[0, rs, :] + mod_ref[0, GATE1:GATE1 + 1, :] * y
        x_out[0, rs, :] = x1
        h_out[0, rs, :] = _norm_mod(x1, g2_ref[...], mod_ref[0, SHIFT2:SHIFT2 + 1, :],
                                    mod_ref[0, SCALE2:SCALE2 + 1, :]).astype(BF16)


def _mix_out(a1, a2, w_bf16, x, mod, mod_row, g2):
    b, l, d = x.shape
    tm = _tile("mix_out_tm", l)
    k1, k2 = a1.shape[2], a2.shape[2]
    row = lambda w: pl.BlockSpec((1, tm, w), lambda bi, m: (bi, m, 0))
    vmem = (k1 + k2) * d * 2 + 2 * tm * (k1 + k2) * 2 + 4 * tm * d * 4 + 2 * tm * d * 2 + 6 * tm * d * 4 + (4 << 20)
    return pl.pallas_call(
        functools.partial(_mix_out_kernel, row_pieces=max(1, tm // TILES["mix_out_piece_rows"])),
        grid=(b, l // tm),
        in_specs=[row(k1), row(k2),
                  pl.BlockSpec((k1 + k2, d), lambda bi, m: (0, 0)),
                  row(d),
                  pl.BlockSpec((1, 6, d), _mod_index(mod_row, 0)),
                  pl.BlockSpec((1, d), lambda bi, m: (0, 0))],
        out_specs=[row(d), row(d)],
        out_shape=[jax.ShapeDtypeStruct((b, l, d), F32), jax.ShapeDtypeStruct((b, l, d), BF16)],
        compiler_params=_cparams(("parallel", "parallel"), vmem),
        name="mix_out",
    )(a1, a2, w_bf16, x, mod, g2)


def _ffn_up_kernel(h_ref, wa_ref, wg_ref, cwa_ref, cwg_ref, cba_ref, cbg_ref, wd_ref, o_ref, wd_out,
                   wa_scr, wg_scr, *, row_pieces):
    @pl.when(pl.program_id(1) == 0)
    def _():
        wa_scr[...] = wa_ref[...].astype(BF16)
        wg_scr[...] = wg_ref[...].astype(BF16)

    wd_out[...] = wd_ref[...].astype(BF16)

    l = h_ref.shape[1]
    pr = l // row_pieces
    sub = lax.broadcasted_iota(jnp.int32, (SUBLANES, 1), 0)

    def conv(w_scr, cw_ref, cb_ref):
        u = jnp.concatenate([_dot(h_ref[0, i * pr:(i + 1) * pr, :], w_scr[...]) for i in range(row_pieces)], axis=0)
        prev = pltpu.roll(u, 1, 0)
        nxt = pltpu.roll(u, l - 1, 0)
        prev = jnp.concatenate([jnp.where(sub == 0, 0.0, prev[:SUBLANES]), prev[SUBLANES:]], axis=0)
        nxt = jnp.concatenate([nxt[:-SUBLANES], jnp.where(sub == SUBLANES - 1, 0.0, nxt[-SUBLANES:])], axis=0)
        cw = cw_ref[...]
        return cw[0:1, :] * prev + cw[1:2, :] * u + cw[2:3, :] * nxt + cb_ref[...]

    a = conv(wa_scr, cwa_ref, cba_ref)
    gt = conv(wg_scr, cwg_ref, cbg_ref)
    o_ref[0] = (_silu(gt) * a).astype(BF16)


def _ffn_up(h, w_up, conv_w, conv_b, w_down, layer):
    b, l, d = h.shape
    d_out = w_down.shape[2]
    f = w_up.shape[2] // 2
    tn = _tile("ffn_up_tn", f)
    n_n = f // tn
    assert l <= TILES["ffn_up_max_rows"], "the row tile is the whole sequence"
    row_pieces = max(1, l // TILES["ffn_up_piece_rows"])
    assert l % (row_pieces * BF16_TILE_ROWS) == 0, (l, row_pieces)
    col = lambda rows, off: pl.BlockSpec((None, rows, tn), lambda n, bi: (layer, 0, n + off))
    band = f // (n_n * b)
    assert band * n_n * b == f and band % BF16_TILE_ROWS == 0, (f, band)
    vmem = (2 * l * d * 2 + 4 * d * tn * 4 + 2 * d * tn * 2 + 2 * band * d_out * 6
            + 2 * l * tn * 2 + 10 * l * tn * 4 + (4 << 20))
    return pl.pallas_call(
        functools.partial(_ffn_up_kernel, row_pieces=row_pieces),
        grid=(n_n, b),
        in_specs=[
            pl.BlockSpec((1, l, d), lambda n, bi: (bi, 0, 0)),
            col(d, 0), col(d, n_n), col(3, 0), col(3, n_n), col(1, 0), col(1, n_n),
            pl.BlockSpec((None, band, d_out), lambda n, bi: (layer, n * b + bi, 0)),
        ],
        out_specs=[pl.BlockSpec((1, l, tn), lambda n, bi: (bi, 0, n)),
                   pl.BlockSpec((band, d_out), lambda n, bi: (n * b + bi, 0))],
        out_shape=[jax.ShapeDtypeStruct((b, l, f), BF16), jax.ShapeDtypeStruct((f, d_out), BF16)],
        scratch_shapes=[pltpu.VMEM((d, tn), BF16), pltpu.VMEM((d, tn), BF16)],
        compiler_params=_cparams(("arbitrary", "arbitrary"), vmem),
        name="ffn_up",
    )(h, w_up, w_up, conv_w, conv_w, conv_b, conv_b, w_down)


def _ffn_down_kernel(a_ref, w_ref, x_ref, mod_ref, o_ref):
    y = _dot(a_ref[0], w_ref[...])
    o_ref[0] = x_ref[0] + mod_ref[0, GATE2:GATE2 + 1, :] * y


def _ffn_down(act, w_bf16, x, mod, mod_row, n_col_tiles=None):
    b, l, d = x.shape
    tm = _tile("ffn_down_tm", l)
    tn = _tile("ffn_down_tn", d)
    f = act.shape[2]
    if n_col_tiles is None:
        n_col_tiles = d // tn
    vmem = 2 * tm * f * 2 + 2 * f * tn * 2 + 5 * tm * tn * 4 + (6 << 20)
    return pl.pallas_call(
        _ffn_down_kernel,
        grid=(n_col_tiles, b, l // tm),
        in_specs=[
            pl.BlockSpec((1, tm, f), lambda n, bi, m: (bi, m, 0)),
            pl.BlockSpec((f, tn), lambda n, bi, m: (0, n)),
            pl.BlockSpec((1, tm, tn), lambda n, bi, m: (bi, m, n)),
            pl.BlockSpec((1, 6, tn), _mod_index(mod_row, 1, col_axis=0)),
        ],
        out_specs=pl.BlockSpec((1, tm, tn), lambda n, bi, m: (bi, m, n)),
        out_shape=jax.ShapeDtypeStruct((b, l, n_col_tiles * tn), F32),
        compiler_params=_cparams(("parallel", "parallel", "parallel"), vmem),
        name="ffn_down",
    )(act, w_bf16, x, mod)


def _ffn_down_norm_kernel(a_ref, w_ref, x_ref, left_ref, mod_ref, g_ref, o_ref):
    y = _dot(a_ref[0], w_ref[...])
    right = x_ref[0] + mod_ref[0, GATE2:GATE2 + 1, :] * y
    left = left_ref[0]
    wl = left.shape[1]
    d = wl + right.shape[1]
    ms = (jnp.sum(left * left, axis=-1, keepdims=True) + jnp.sum(right * right, axis=-1, keepdims=True)) / d
    r = lax.rsqrt(ms + EPS)
    o_ref[0, :, :wl] = left * r * g_ref[:, :wl]
    o_ref[0, :, wl:] = right * r * g_ref[:, wl:]


def _ffn_down_norm(act, w_bf16, x, left, mod, mod_row, g):
    b, l, d = x.shape
    tm = _tile("ffn_down_tm", l)
    tn = _tile("ffn_down_tn", d)
    f = act.shape[2]
    last = d // tn - 1
    vmem = 2 * tm * f * 2 + 2 * f * tn * 2 + 6 * tm * d * 4 + (6 << 20)
    return pl.pallas_call(
        _ffn_down_norm_kernel,
        grid=(b, l // tm),
        in_specs=[
            pl.BlockSpec((1, tm, f), lambda bi, m: (bi, m, 0)),
            pl.BlockSpec((f, tn), lambda bi, m: (0, last)),
            pl.BlockSpec((1, tm, tn), lambda bi, m: (bi, m, last)),
            pl.BlockSpec((1, tm, d - tn), lambda bi, m: (bi, m, 0)),
            pl.BlockSpec((1, 6, tn), lambda bi, m: (bi if mod_row is None else mod_row, 0, last)),
            pl.BlockSpec((1, d), lambda bi, m: (0, 0)),
        ],
        out_specs=pl.BlockSpec((1, tm, d), lambda bi, m: (bi, m, 0)),
        out_shape=jax.ShapeDtypeStruct((b, l, d), F32),
        compiler_params=_cparams(("parallel", "parallel"), vmem),
        name="ffn_down_norm",
    )(act, w_bf16, x, left, mod, g)


def _pool_kernel(xm_ref, xp_ref, xn_ref, mod_ref, g_ref, w_ref, ps_ref, g2_ref, x_out, h_out, x1_scr, w_scr,
                 *, tm, n_m, seq_len):
    m = pl.program_id(1)

    @pl.when((pl.program_id(0) == 0) & (m == 0))
    def _():
        w_scr[...] = w_ref[...].astype(BF16)

    g = g_ref[...]
    shift = mod_ref[0, SHIFT1:SHIFT1 + 1, :]
    scale = mod_ref[0, SCALE1:SCALE1 + 1, :]
    xm = xm_ref[0]
    hp = _norm_mod(xp_ref[0], g, shift, scale) * (m > 0).astype(F32)
    hn = _norm_mod(xn_ref[0], g, shift, scale) * (m < n_m - 1).astype(F32)
    hh = jnp.concatenate([hp, _norm_mod(xm, g, shift, scale), hn], axis=0)
    rows = tm + 2 * HALO
    t = (m * tm + lax.broadcasted_iota(jnp.int32, (tm, 1), 0)).astype(F32)
    gw = hh.shape[1] // len(POOL_WINDOWS)
    out_gain = mod_ref[0, GATE1:GATE1 + 1, :] * ps_ref[...]
    for gi, w in enumerate(POOL_WINDOWS):
        sl = slice(gi * gw, (gi + 1) * gw)
        hg = hh[:, sl]
        acc = hg
        span = 1
        while span < w // 2:
            acc = acc + pltpu.roll(acc, rows - span, 0)
            span *= 2
        acc = acc + pltpu.roll(acc, w // 2, 0)
        win = acc[HALO:HALO + tm]
        cnt = jnp.minimum(t + (w // 2 - 1), seq_len - 1.0) - jnp.maximum(t - w // 2, 0.0) + 1.0
        p = (win / cnt - hg[HALO:HALO + tm]).astype(BF16)
        x1_scr[:, sl] = xm[:, sl] + out_gain[:, sl] * _dot(p, w_scr[gi])
    x1 = x1_scr[...]
    x_out[0] = x1
    h_out[0] = _norm_mod(x1, g2_ref[...], mod_ref[0, SHIFT2:SHIFT2 + 1, :],
                         mod_ref[0, SCALE2:SCALE2 + 1, :]).astype(BF16)


def _pool_mix(x, mod, mod_row, g, pool_w, pool_scale, g2):
    b, l, d = x.shape
    tm = _tile("pool_tm", l)
    n_m = l // tm
    hb = tm // HALO
    ng, gw, _ = pool_w.shape
    row = pl.BlockSpec((1, tm, d), lambda bi, m: (bi, m, 0))
    vec = pl.BlockSpec((1, d), lambda bi, m: (0, 0))
    vmem = 7 * tm * d * 4 + 2 * ng * gw * gw * 4 + 8 * (tm + 2 * HALO) * d * 4 + (6 << 20)
    return pl.pallas_call(
        functools.partial(_pool_kernel, tm=tm, n_m=n_m, seq_len=l),
        grid=(b, n_m),
        in_specs=[
            row,
            pl.BlockSpec((1, HALO, d), lambda bi, m: (bi, jnp.maximum(m * hb - 1, 0), 0)),
            pl.BlockSpec((1, HALO, d), lambda bi, m: (bi, jnp.minimum((m + 1) * hb, l // HALO - 1), 0)),
            pl.BlockSpec((1, 6, d), _mod_index(mod_row, 0)),
            vec,
            pl.BlockSpec((ng, gw, gw), lambda bi, m: (0, 0, 0)),
            vec, vec,
        ],
        out_specs=[row, row],
        out_shape=[jax.ShapeDtypeStruct((b, l, d), F32), jax.ShapeDtypeStruct((b, l, d), BF16)],
        scratch_shapes=[pltpu.VMEM((tm, d), F32), pltpu.VMEM((ng, gw, gw), BF16)],
        compiler_params=_cparams(("arbitrary", "arbitrary"), vmem),
        name="pool_mix",
    )(x, x, x, mod, g, pool_w, pool_scale, g2)


def _final_norm_kernel(x_ref, g_ref, o_ref):
    x = x_ref[0]
    o_ref[0] = x * lax.rsqrt(jnp.mean(x * x, axis=-1, keepdims=True) + EPS) * g_ref[...]


def _final_norm(x, g):
    b, l, d = x.shape
    tm = _tile("final_tm", l)
    return pl.pallas_call(
        _final_norm_kernel,
        grid=(b, l // tm),
        in_specs=[pl.BlockSpec((1, tm, d), lambda bi, m: (bi, m, 0)), pl.BlockSpec((1, d), lambda bi, m: (0, 0))],
        out_specs=pl.BlockSpec((1, tm, d), lambda bi, m: (bi, m, 0)),
        out_shape=jax.ShapeDtypeStruct((b, l, d), F32),
        compiler_params=_cparams(("parallel", "parallel"), 6 * tm * d * 4 + (4 << 20)),
        name="final_norm",
    )(x, g)


def _axial_angles(rows, dim):
    row = np.repeat(np.arange(rows, dtype=np.float32), GRID_W)
    col = np.tile(np.arange(GRID_W, dtype=np.float32), rows)
    n_freq = dim // 4
    inv = np.float32(ROPE_BASE) ** (-np.arange(n_freq, dtype=np.float32) / np.float32(n_freq))
    return np.concatenate([row[:, None] * inv, col[:, None] * inv], axis=-1).astype(np.float32)


def _rope_tables(rows, dim):
    ang = _axial_angles(rows, dim)
    cos, sin = np.cos(ang), np.sin(ang)
    cos_t = np.concatenate([cos, cos], axis=-1)
    sin_t = np.concatenate([-sin, sin], axis=-1)
    pad = LANES - dim
    if pad:
        cos_t = np.pad(cos_t, ((0, 0), (0, pad)))
        sin_t = np.pad(sin_t, ((0, 0), (0, pad)))
    return jnp.asarray(cos_t, F32), jnp.asarray(sin_t, F32)


def _mla_weights(g_q, w_uq, g_kv, w_ukv, heads, ret_w):
    q_rank, kv_rank = g_q.shape[0], g_kv.shape[0]
    wq = w_uq.reshape(q_rank, heads, HEAD_DIM + MLA_DR)
    wq = jnp.pad(wq, ((0, 0), (0, 0), (0, HEAD_DIM - MLA_DR))).reshape(q_rank, heads * 2 * HEAD_DIM).astype(BF16)
    wkv = w_ukv.reshape(kv_rank, heads, 2 * HEAD_DIM)
    wkn = wkv[:, :, :HEAD_DIM].reshape(kv_rank, ret_w).astype(BF16)
    wv = wkv[:, :, HEAD_DIM:].reshape(kv_rank, ret_w).astype(BF16)
    return g_q[None, :], wq, g_kv[None, :], wkn, wv


def _conv_ffn(h, x, mod, mod_row, w_up, conv_w, conv_b, w_down, layer, final_g=None):
    act, w_down_bf16 = _ffn_up(h, w_up, conv_w, conv_b[:, None, :], w_down, layer)
    d = x.shape[2]
    n_col_tiles = d // _tile("ffn_down_tn", d)
    if final_g is None:
        return _ffn_down(act, w_down_bf16, x, mod, mod_row)
    if n_col_tiles == 1:
        return _final_norm(_ffn_down(act, w_down_bf16, x, mod, mod_row), final_g)
    left = _ffn_down(act, w_down_bf16, x, mod, mod_row, n_col_tiles - 1)
    return _ffn_down_norm(act, w_down_bf16, x, left, mod, mod_row, final_g)


def kernel(x, c, ctx, c_ctx, ada_w, ada_b, norm1_g, norm2_g, ffn_w_up, ffn_conv_w, ffn_conv_b, ffn_w_down, mix_w_in, mla_q_norm_g, mla_w_uq, mla_kv_norm_g, mla_w_ukv, ret_decay_f, ret_decay_b, mix_w_out, pool_w, pool_scale, final_g):
    b, l, d = x.shape
    lc = ctx.shape[1]
    depth = ada_w.shape[0]
    heads = ret_decay_f.shape[1]
    ret_w = heads * HEAD_DIM
    rows = l // GRID_W
    rope_ret = _rope_tables(rows, HEAD_DIM)
    rope_mla = _rope_tables(rows, MLA_DR)

    ctx_row = b
    cc = jnp.concatenate([c, c_ctx[None, :], jnp.zeros((-(b + 1) % 8, d), F32)], axis=0)
    mods = _ada(cc, ada_w, ada_b).reshape(depth, cc.shape[0], 6, d)

    x_lat, x_ctx = x, ctx
    for layer in range(depth):
        j = layer // 2
        with_ctx = layer < depth - 1
        mod = mods[layer]
        g1 = norm1_g[layer][None, :]
        g2 = norm2_g[layer][None, :]
        h_ctx = None
        if layer % 2 == 0:
            w_in_t = jnp.swapaxes(mix_w_in[j], 0, 1)
            mla_w = _mla_weights(mla_q_norm_g[j], mla_w_uq[j], mla_kv_norm_g[j], mla_w_ukv[j], heads, ret_w)
            dec = jnp.stack([ret_decay_f[j], ret_decay_b[j]]).astype(F32)

            h1_lat, q_lat, k_lat, v_lat = _inproj_mla(x_lat, mod, None, g1, w_in_t, 4 * ret_w, *mla_w, rope_mla, heads)
            h1_ctx, q_ctx, k_ctx, v_ctx = _inproj_mla(x_ctx.reshape(1, b * lc, d), mod, ctx_row, g1, w_in_t,
                                                      4 * ret_w, *mla_w, None, heads)
            q_ctx, k_ctx, v_ctx = [a.reshape(b, lc, -1) for a in (q_ctx, k_ctx, v_ctx)]
            ret_lat, w_out_bf16 = _inproj_ret(h1_lat, w_in_t, rope_ret, ret_w, cast_extra=mix_w_out[j])
            ret_ctx = _inproj_ret(h1_ctx, w_in_t, None, ret_w).reshape(b, lc, 4 * ret_w)

            o_ret_lat, o_ret_ctx = _retention(dec, ret_lat, ret_ctx, heads)
            o_mla_lat = _attention(q_lat, k_ctx, v_ctx, k_lat, v_lat, heads)
            x_lat, h_lat = _mix_out(o_ret_lat, o_mla_lat, w_out_bf16, x_lat, mod, None, g2)
            if with_ctx:
                o_mla_ctx = _attention(q_ctx, k_ctx, v_ctx, None, None, heads)
                x_ctx, h_ctx = _mix_out(o_ret_ctx, o_mla_ctx, w_out_bf16, x_ctx, mod, ctx_row, g2)
        else:
            ps = pool_scale[j][None, :]
            if with_ctx:
                x_ctx, h_ctx = _pool_mix(x_ctx, mod, ctx_row, g1, pool_w[j], ps, g2)
            x_lat, h_lat = _pool_mix(x_lat, mod, None, g1, pool_w[j], ps, g2)
        last = layer == depth - 1
        x_lat = _conv_ffn(h_lat, x_lat, mod, None, ffn_w_up, ffn_conv_w, ffn_conv_b, ffn_w_down, layer,
                          final_g=final_g[None, :] if last else None)
        if with_ctx:
            x_ctx = _conv_ffn(h_ctx, x_ctx, mod, ctx_row, ffn_w_up, ffn_conv_w, ffn_conv_b, ffn_w_down, layer)
    return x_lat
```

```python
import functools

import jax
import jax.numpy as jnp
import numpy as np
from jax import lax
from jax.experimental import pallas as pl
from jax.experimental.pallas import tpu as pltpu

F32 = jnp.float32
BF16 = jnp.bfloat16

GRID_W = 64
HEAD_DIM = 128
RET_CHUNK = 128
MLA_DR = 64
POOL_WINDOWS = (2, 4, 8, 16)
ROPE_BASE = 10000.0
EPS = 1e-6
LOG2_E = 1.4426950408889634

LANES = 128
MXU_DIM = 256
SUBLANES = 8
BF16_TILE_ROWS = 16
HALO = 16
V7X_VMEM_BYTES = 64 * 1024 * 1024
VMEM_CAP = V7X_VMEM_BYTES - 8 * 1024 * 1024

SHIFT1, SCALE1, GATE1, SHIFT2, SCALE2, GATE2 = range(6)

TILES = dict(
    inproj_ret_tm=2048, inproj_ret_piece_rows=256,
    inproj_mla_tm=512, inproj_mla_piece_rows=256,
    attn_tq=512, attn_heads_per_step=2,
    mix_out_tm=512, mix_out_piece_rows=256,
    ffn_up_tn=256, ffn_up_piece_rows=128, ffn_up_max_rows=2048,
    ffn_down_tm=512, ffn_down_tn=1024,
    pool_tm=512,
    final_tm=512,
    ada_tn=2048,
)


def _tile(name, extent):
    tile = min(extent, TILES[name])
    assert extent % tile == 0, f"{name}: extent {extent} is not a multiple of its tile {tile}"
    return tile


def _cparams(sem, vmem_bytes):
    return pltpu.CompilerParams(dimension_semantics=sem, vmem_limit_bytes=min(int(vmem_bytes), VMEM_CAP))


def _silu(x):
    return x / (1.0 + jnp.exp(-x))


def _norm_mod(x, g, shift, scale):
    gain = g * (1.0 + scale)
    return x * lax.rsqrt(jnp.mean(x * x, axis=-1, keepdims=True) + EPS) * gain + shift


def _dot(a, b):
    return jnp.dot(a, b, preferred_element_type=F32)


def _dot_nt(a, b):
    return lax.dot_general(a, b, (((1,), (1,)), ((), ())), preferred_element_type=F32)


def _dot_tn(a, b):
    return lax.dot_general(a, b, (((0,), (0,)), ((), ())), preferred_element_type=F32)


def _mod_index(mod_row, batch_axis, col_axis=None):
    def index(*ids):
        row = ids[batch_axis] if mod_row is None else mod_row
        return (row, 0, 0 if col_axis is None else ids[col_axis])
    return index


def _ada_kernel(c_ref, w_ref, b_ref, o_ref):
    s = _silu(c_ref[...]).astype(BF16)
    o_ref[0] = _dot(s, w_ref[0].astype(BF16)) + b_ref[0]


def _ada(cc, ada_w, ada_b):
    depth, d, n6 = ada_w.shape
    rows = cc.shape[0]
    tn = _tile("ada_tn", n6)
    return pl.pallas_call(
        _ada_kernel,
        grid=(depth, n6 // tn),
        in_specs=[
            pl.BlockSpec((rows, d), lambda l, n: (0, 0)),
            pl.BlockSpec((1, d, tn), lambda l, n: (l, 0, n)),
            pl.BlockSpec((1, 1, tn), lambda l, n: (l, 0, n)),
        ],
        out_specs=pl.BlockSpec((1, rows, tn), lambda l, n: (l, 0, n)),
        out_shape=jax.ShapeDtypeStruct((depth, rows, n6), F32),
        compiler_params=_cparams(("parallel", "parallel"), 2 * d * tn * 4 + d * tn * 2 + (8 << 20)),
        name="ada_mod",
    )(cc, ada_w, ada_b.reshape(depth, 1, n6))


def _inproj_ret_kernel(*refs, rope, cast_extra, heads_per_tile, k_scale, row_pieces):
    refs = list(refs)
    h_ref = refs.pop(0)
    if rope:
        cos_ref, sin_ref = refs.pop(0), refs.pop(0)
    w_ref = refs.pop(0)
    if cast_extra:
        extra_ref = refs.pop(0)
    o_ref = refs.pop(0)
    if cast_extra:
        extra_out = refs.pop(0)
    (w_scr,) = refs
    n = pl.program_id(0)

    @pl.when((pl.program_id(1) == 0) & (pl.program_id(2) == 0))
    def _():
        w_scr[...] = w_ref[...].astype(BF16)

    if cast_extra:
        extra_out[...] = extra_ref[...].astype(BF16)
    scale = jnp.where(n == 1, k_scale, 1.0).astype(F32)
    if rope:
        rotated = n < 2
        cos = jnp.where(rotated, cos_ref[...], 1.0) * scale
        sin = jnp.where(rotated, sin_ref[...], 0.0) * scale
    pr = h_ref.shape[1] // row_pieces
    for i in range(row_pieces):
        rs = slice(i * pr, (i + 1) * pr)
        z = _dot_nt(h_ref[0, rs, :], w_scr[...])
        if rope:
            for h in range(heads_per_tile):
                zh = z[:, h * HEAD_DIM:(h + 1) * HEAD_DIM]
                r = zh * cos[rs] + pltpu.roll(zh, HEAD_DIM // 2, 1) * sin[rs]
                o_ref[0, rs, h * HEAD_DIM:(h + 1) * HEAD_DIM] = r.astype(BF16)
        else:
            o_ref[0, rs, :] = (z * scale).astype(BF16)


def _inproj_ret(h, w_in_t, rope_tabs, ret_w, cast_extra=None):
    b, l, d = h.shape
    tm = _tile("inproj_ret_tm", l)
    tn = ret_w
    n_tiles = 4
    n_m = l // tm
    rope = rope_tabs is not None
    in_specs = [pl.BlockSpec((1, tm, d), lambda n, bi, m: (bi, m, 0))]
    args = [h]
    if rope:
        in_specs += [pl.BlockSpec((tm, HEAD_DIM), lambda n, bi, m: (m, 0))] * 2
        args += list(rope_tabs)
    in_specs.append(pl.BlockSpec((tn, d), lambda n, bi, m: (n, 0)))
    args.append(w_in_t)
    out_specs = [pl.BlockSpec((1, tm, tn), lambda n, bi, m: (bi, m, n))]
    out_shape = [jax.ShapeDtypeStruct((b, l, n_tiles * tn), BF16)]
    vmem = 2 * tm * d * 2 + 2 * d * tn * 4 + d * tn * 2 + 2 * tm * tn * 2 + 4 * tm * tn * 4 + (6 << 20)
    if cast_extra is not None:
        er, ec = cast_extra.shape
        band = er // (n_tiles * b * n_m)
        assert band * n_tiles * b * n_m == er and band % BF16_TILE_ROWS == 0, (er, band)
        extra_spec = pl.BlockSpec((band, ec), lambda n, bi, m: ((n * b + bi) * n_m + m, 0))
        in_specs.append(extra_spec)
        args.append(cast_extra)
        out_specs.append(extra_spec)
        out_shape.append(jax.ShapeDtypeStruct((er, ec), BF16))
        vmem += 2 * band * ec * 6
    outs = pl.pallas_call(
        functools.partial(_inproj_ret_kernel, rope=rope, cast_extra=cast_extra is not None,
                          heads_per_tile=tn // HEAD_DIM, k_scale=HEAD_DIM ** -0.5,
                          row_pieces=max(1, tm // TILES["inproj_ret_piece_rows"])),
        grid=(n_tiles, b, l // tm),
        in_specs=in_specs,
        out_specs=out_specs,
        out_shape=out_shape,
        scratch_shapes=[pltpu.VMEM((tn, d), BF16)],
        compiler_params=_cparams(("arbitrary", "arbitrary", "arbitrary"), vmem),
        name="inproj_ret_rope" if rope else "inproj_ret",
    )(*args)
    return outs if cast_extra is not None else outs[0]


def _rope_half_padded(x, cos, sin):
    rot = pltpu.roll(x, MLA_DR // 2, 1) + pltpu.roll(x, LANES - MLA_DR // 2, 1)
    return x * cos + rot * sin


def _inproj_mla_kernel(*refs, rope, heads, q_rank, kv_rank, q_scale, row_pieces):
    refs = list(refs)
    x_ref, mod_ref, g_ref = refs[:3]
    del refs[:3]
    if rope:
        cos_ref, sin_ref = refs[:2]
        del refs[:2]
    (wcq_ref, wckv_ref, wkr_ref, gq_ref, wq_ref, gkv_ref, wkn_ref, wv_ref,
     h_out, q_out, k_out, v_out, wt_scr) = refs

    @pl.when((pl.program_id(0) == 0) & (pl.program_id(1) == 0))
    def _():
        wt_scr[0:q_rank, :] = wcq_ref[...].astype(BF16)
        wt_scr[q_rank:q_rank + kv_rank, :] = wckv_ref[...].astype(BF16)
        wt_scr[q_rank + kv_rank:q_rank + kv_rank + MLA_DR, :] = wkr_ref[...].astype(BF16)
        wt_scr[q_rank + kv_rank + MLA_DR:, :] = jnp.zeros((LANES - MLA_DR, wt_scr.shape[1]), BF16)

    pr = x_ref.shape[1] // row_pieces
    for i in range(row_pieces):
        rs = slice(i * pr, (i + 1) * pr)
        h = _norm_mod(x_ref[0, rs, :], g_ref[...], mod_ref[0, SHIFT1:SHIFT1 + 1, :],
                      mod_ref[0, SCALE1:SCALE1 + 1, :]).astype(BF16)
        h_out[0, rs, :] = h
        zt = _dot_nt(h, wt_scr[...])
        cq = zt[:, :q_rank]
        ckv = zt[:, q_rank:q_rank + kv_rank]
        kr = zt[:, q_rank + kv_rank:]
        cqn = (cq * lax.rsqrt(jnp.mean(cq * cq, axis=-1, keepdims=True) + EPS) * gq_ref[...]).astype(BF16)
        ckvn = (ckv * lax.rsqrt(jnp.mean(ckv * ckv, axis=-1, keepdims=True) + EPS) * gkv_ref[...]).astype(BF16)
        q = _dot(cqn, wq_ref[...])
        kn = _dot(ckvn, wkn_ref[...])
        v_out[0, rs, :] = _dot(ckvn, wv_ref[...]).astype(BF16)
        if rope:
            cos = cos_ref[rs, :]
            sin = sin_ref[rs, :]
            kr = _rope_half_padded(kr, cos, sin)
        kr = kr.astype(BF16)
        for hd in range(heads):
            base = hd * 2 * HEAD_DIM
            qn = q[:, base:base + HEAD_DIM]
            qr = q[:, base + HEAD_DIM:base + 2 * HEAD_DIM]
            if rope:
                qr = _rope_half_padded(qr, cos, sin)
            q_out[0, rs, base:base + HEAD_DIM] = (qn * q_scale).astype(BF16)
            q_out[0, rs, base + HEAD_DIM:base + 2 * HEAD_DIM] = (qr * q_scale).astype(BF16)
            k_out[0, rs, base:base + HEAD_DIM] = kn[:, hd * HEAD_DIM:(hd + 1) * HEAD_DIM].astype(BF16)
            k_out[0, rs, base + HEAD_DIM:base + 2 * HEAD_DIM] = kr


def _inproj_mla(x, mod, mod_row, g, w_in_t, tail_row, gq, wq, gkv, wkn, wv, rope_tabs, heads):
    b, l, d = x.shape
    tm = _tile("inproj_mla_tm", l)
    q_rank, kv_rank = gq.shape[1], gkv.shape[1]
    rope = rope_tabs is not None
    full = lambda a: pl.BlockSpec(a.shape, lambda bi, m: (0,) * a.ndim)
    in_specs = [pl.BlockSpec((1, tm, d), lambda bi, m: (bi, m, 0)),
                pl.BlockSpec((1, 6, d), _mod_index(mod_row, 0)), full(g)]
    args = [x, mod, g]
    if rope:
        in_specs += [pl.BlockSpec((tm, LANES), lambda bi, m: (m, 0))] * 2
        args += list(rope_tabs)
    for start, rows in ((tail_row, q_rank), (tail_row + q_rank, kv_rank), (tail_row + q_rank + kv_rank, MLA_DR)):
        assert start % rows == 0, (start, rows)
        in_specs.append(pl.BlockSpec((rows, d), functools.partial(lambda blk, bi, m: (blk, 0), start // rows)))
        args.append(w_in_t)
    for a in (gq, wq, gkv, wkn, wv):
        in_specs.append(full(a))
        args.append(a)
    qw = heads * 2 * HEAD_DIM
    vw = heads * HEAD_DIM
    tail = q_rank + kv_rank + LANES
    out_spec = lambda w: pl.BlockSpec((1, tm, w), lambda bi, m: (bi, m, 0))
    wbytes = sum(a.size * a.dtype.itemsize for a in (wq, wkn, wv))
    vmem = (2 * tm * d * 4 + 2 * tail * d * 4 + tail * d * 2 + 2 * wbytes + 2 * tm * (d + 2 * qw + vw) * 2
            + tm * (3 * d * 4 + 6 * qw * 4) + (4 << 20))
    return pl.pallas_call(
        functools.partial(_inproj_mla_kernel, rope=rope, heads=heads, q_rank=q_rank, kv_rank=kv_rank,
                          q_scale=(HEAD_DIM + MLA_DR) ** -0.5 * LOG2_E,
                          row_pieces=max(1, tm // TILES["inproj_mla_piece_rows"])),
        grid=(b, l // tm),
        in_specs=in_specs,
        out_specs=[out_spec(d), out_spec(qw), out_spec(qw), out_spec(vw)],
        out_shape=[jax.ShapeDtypeStruct((b, l, d), BF16), jax.ShapeDtypeStruct((b, l, qw), BF16),
                   jax.ShapeDtypeStruct((b, l, qw), BF16), jax.ShapeDtypeStruct((b, l, vw), BF16)],
        scratch_shapes=[pltpu.VMEM((tail, d), BF16)],
        compiler_params=_cparams(("arbitrary", "arbitrary"), vmem),
        name="inproj_mla_rope" if rope else "inproj_mla",
    )(*args)


def _log_sigmoid(x):
    return jnp.minimum(x, 0.0) - jnp.log1p(jnp.exp(-jnp.abs(x)))


def _retention_kernel(dec_ref, ql_ref, kl_ref, vl_ref, gl_ref, qc_ref, kc_ref, vc_ref, gc_ref, ol_ref, oc_ref,
                      st_scr, *, n_ctx, n_lat, c):
    dk = HEAD_DIM
    hd = pl.program_id(1)
    lf = _log_sigmoid(jnp.full((c, c), dec_ref[0, hd], F32))
    lb = _log_sigmoid(jnp.full((c, c), dec_ref[1, hd], F32))
    ii = lax.broadcasted_iota(jnp.int32, (c, c), 0).astype(F32)
    jj = lax.broadcasted_iota(jnp.int32, (c, c), 1).astype(F32)
    diff = ii - jj
    fwd = diff >= 0
    decay = jnp.where(fwd, jnp.exp(jnp.where(fwd, diff, 0.0) * lf), jnp.exp(jnp.where(fwd, 0.0, -diff) * lb))
    ri = lax.broadcasted_iota(jnp.int32, (c, dk), 0).astype(F32)
    lfr = _log_sigmoid(jnp.full((c, dk), dec_ref[0, hd], F32))
    lbr = _log_sigmoid(jnp.full((c, dk), dec_ref[1, hd], F32))
    q_dec_f = jnp.exp((ri + 1.0) * lfr)
    k_dec_f = jnp.exp((c - 1.0 - ri) * lfr)
    q_dec_b = jnp.exp((c - ri) * lbr)
    k_dec_b = jnp.exp(ri * lbr)
    c_dec_f = jnp.exp(c * _log_sigmoid(jnp.full((dk, dk), dec_ref[0, hd], F32)))
    c_dec_b = jnp.exp(c * _log_sigmoid(jnp.full((dk, dk), dec_ref[1, hd], F32)))

    blocks = [(qc_ref, kc_ref, vc_ref, gc_ref, oc_ref, i) for i in range(n_ctx)]
    blocks += [(ql_ref, kl_ref, vl_ref, gl_ref, ol_ref, i) for i in range(n_lat)]

    def rows(ref, i):
        return ref[0, i * c:(i + 1) * c, :]

    for g, (_, k_ref, v_ref, _, _, i) in enumerate(blocks):
        kf = rows(k_ref, i).astype(F32)
        kk = jnp.concatenate([(kf * k_dec_f).astype(BF16), (kf * k_dec_b).astype(BF16)], axis=1)
        st_scr[g] = _dot_tn(kk, rows(v_ref, i))

    s = jnp.zeros((dk, dk), F32)
    for g in range(n_ctx + n_lat):
        kv = st_scr[g, :dk, :]
        st_scr[g, :dk, :] = s
        s = s * c_dec_f + kv
    s = jnp.zeros((dk, dk), F32)
    for g in list(range(n_ctx - 1, -1, -1)) + list(range(n_ctx + n_lat - 1, n_ctx - 1, -1)):
        kv = st_scr[g, dk:, :]
        st_scr[g, dk:, :] = s
        s = s * c_dec_b + kv

    for g, (q_ref, k_ref, v_ref, g_ref, o_ref, i) in enumerate(blocks):
        q = rows(q_ref, i)
        p = (_dot_nt(q, rows(k_ref, i)) * decay).astype(BF16)
        qf = q.astype(F32)
        qq = jnp.concatenate([(qf * q_dec_f).astype(BF16), (qf * q_dec_b).astype(BF16)], axis=1)
        o = _dot(p, rows(v_ref, i)) + _dot(qq, st_scr[g].astype(BF16))
        o = o * lax.rsqrt(jnp.mean(o * o, axis=-1, keepdims=True) + EPS)
        o_ref[0, i * c:(i + 1) * c, :] = (o * _silu(rows(g_ref, i).astype(F32))).astype(BF16)


def _retention(dec, ret_lat, ret_ctx, heads):
    b, l, _ = ret_lat.shape
    lc = ret_ctx.shape[1]
    c = MXU_DIM if (l % MXU_DIM == 0 and lc % MXU_DIM == 0) else RET_CHUNK
    n_lat, n_ctx = l // c, lc // c
    col = lambda rows, group: pl.BlockSpec((1, rows, HEAD_DIM), lambda bi, h: (bi, 0, group * heads + h))
    in_specs = [pl.BlockSpec(memory_space=pltpu.SMEM)]
    in_specs += [col(l, grp) for grp in range(4)] + [col(lc, grp) for grp in range(4)]
    out_col = lambda rows: pl.BlockSpec((1, rows, HEAD_DIM), lambda bi, h: (bi, 0, h))
    vmem = 2 * 5 * (l + lc) * HEAD_DIM * 2 + (n_lat + n_ctx) * 2 * HEAD_DIM * HEAD_DIM * 4 + (12 << 20)
    return pl.pallas_call(
        functools.partial(_retention_kernel, n_ctx=n_ctx, n_lat=n_lat, c=c),
        grid=(b, heads),
        in_specs=in_specs,
        out_specs=[out_col(l), out_col(lc)],
        out_shape=[jax.ShapeDtypeStruct((b, l, heads * HEAD_DIM), BF16),
                   jax.ShapeDtypeStruct((b, lc, heads * HEAD_DIM), BF16)],
        scratch_shapes=[pltpu.VMEM((n_lat + n_ctx, 2 * HEAD_DIM, HEAD_DIM), F32)],
        compiler_params=_cparams(("parallel", "parallel"), vmem),
        name="retention",
    )(dec, ret_lat, ret_lat, ret_lat, ret_lat, ret_ctx, ret_ctx, ret_ctx, ret_ctx)


def _attn_kernel(*refs, with_lat, tq):
    if with_lat:
        q_ref, kc_ref, vc_ref, kl_ref, vl_ref, o_ref = refs
    else:
        q_ref, kc_ref, vc_ref, o_ref = refs
    kw = 2 * HEAD_DIM
    lq = q_ref.shape[1]
    n_heads = o_ref.shape[2] // HEAD_DIM
    for hp in range(n_heads):
        kcols = slice(hp * kw, (hp + 1) * kw)
        vcols = slice(hp * HEAD_DIM, (hp + 1) * HEAD_DIM)
        tiles = [(r, tq) for r in range(0, lq, tq)]
        if hp == n_heads - 1 and tq % (2 * BF16_TILE_ROWS) == 0:
            r, rn = tiles.pop()
            tiles += [(r, rn // 2), (r + rn // 2, rn // 2)]
        for r0, rn in tiles:
            q = q_ref[0, r0:r0 + rn, kcols]
            sc = _dot_nt(q, kc_ref[0, :, kcols])
            m = jnp.max(sc, axis=-1, keepdims=True)
            if with_lat:
                sl = _dot_nt(q, kl_ref[0, :, kcols])
                m = jnp.maximum(m, jnp.max(sl, axis=-1, keepdims=True))
            pc = jnp.exp2(sc - m)
            den = jnp.sum(pc, axis=-1, keepdims=True)
            o = _dot(pc.astype(BF16), vc_ref[0, :, vcols])
            if with_lat:
                p_lat = jnp.exp2(sl - m)
                den = den + jnp.sum(p_lat, axis=-1, keepdims=True)
                o = o + _dot(p_lat.astype(BF16), vl_ref[0, :, vcols])
            o_ref[0, r0:r0 + rn, vcols] = (o / den).astype(BF16)


def _attention(q, k_ctx, v_ctx, k_lat, v_lat, heads):
    b, lq, _ = q.shape
    tq = _tile("attn_tq", lq)
    hps = _tile("attn_heads_per_step", heads)
    lc = k_ctx.shape[1]
    with_lat = k_lat is not None
    kw = 2 * HEAD_DIM
    head = lambda rows, w: pl.BlockSpec((1, rows, hps * w), lambda bi, h: (bi, 0, h))
    in_specs = [head(lq, kw), head(lc, kw), head(lc, HEAD_DIM)]
    args = [q, k_ctx, v_ctx]
    lk = lc
    if with_lat:
        ll = k_lat.shape[1]
        lk += ll
        in_specs += [head(ll, kw), head(ll, HEAD_DIM)]
        args += [k_lat, v_lat]
    vmem = 2 * hps * (lq * kw + lk * (kw + HEAD_DIM) + lq * HEAD_DIM) * 2 + 8 * tq * lk * 4 + (8 << 20)
    return pl.pallas_call(
        functools.partial(_attn_kernel, with_lat=with_lat, tq=tq),
        grid=(b, heads // hps),
        in_specs=in_specs,
        out_specs=head(lq, HEAD_DIM),
        out_shape=jax.ShapeDtypeStruct((b, lq, heads * HEAD_DIM), BF16),
        compiler_params=_cparams(("parallel", "parallel"), vmem),
        name="mla_attn" if with_lat else "mla_attn_ctx",
    )(*args)


def _mix_out_kernel(a1_ref, a2_ref, w_ref, x_ref, mod_ref, g2_ref, x_out, h_out, *, row_pieces):
    k1 = a1_ref.shape[2]
    pr = x_ref.shape[1] // row_pieces
    for i in range(row_pieces):
        rs = slice(i * pr, (i + 1) * pr)
        y = _dot(a1_ref[0, rs, :], w_ref[:k1, :]) + _dot(a2_ref[0, rs, :], w_ref[k1:, :])
        x1 = x_ref[0, rs, :] + mod_ref[0, GATE1:GATE1 + 1, :] * y
        x_out[0, rs, :] = x1
        h_out[0, rs, :] = _norm_mod(x1, g2_ref[...], mod_ref[0, SHIFT2:SHIFT2 + 1, :],
                                    mod_ref[0, SCALE2:SCALE2 + 1, :]).astype(BF16)


def _mix_out(a1, a2, w_bf16, x, mod, mod_row, g2):
    b, l, d = x.shape
    tm = _tile("mix_out_tm", l)
    k1, k2 = a1.shape[2], a2.shape[2]
    row = lambda w: pl.BlockSpec((1, tm, w), lambda bi, m: (bi, m, 0))
    vmem = (k1 + k2) * d * 2 + 2 * tm * (k1 + k2) * 2 + 4 * tm * d * 4 + 2 * tm * d * 2 + 6 * tm * d * 4 + (4 << 20)
    return pl.pallas_call(
        functools.partial(_mix_out_kernel, row_pieces=max(1, tm // TILES["mix_out_piece_rows"])),
        grid=(b, l // tm),
        in_specs=[row(k1), row(k2),
                  pl.BlockSpec((k1 + k2, d), lambda bi, m: (0, 0)),
                  row(d),
                  pl.BlockSpec((1, 6, d), _mod_index(mod_row, 0)),
                  pl.BlockSpec((1, d), lambda bi, m: (0, 0))],
        out_specs=[row(d), row(d)],
        out_shape=[jax.ShapeDtypeStruct((b, l, d), F32), jax.ShapeDtypeStruct((b, l, d), BF16)],
        compiler_params=_cparams(("parallel", "parallel"), vmem),
        name="mix_out",
    )(a1, a2, w_bf16, x, mod, g2)


def _ffn_up_kernel(h_ref, wa_ref, wg_ref, cwa_ref, cwg_ref, cba_ref, cbg_ref, wd_ref, o_ref, wd_out,
                   wa_scr, wg_scr, *, row_pieces):
    @pl.when(pl.program_id(1) == 0)
    def _():
        wa_scr[...] = wa_ref[...].astype(BF16)
        wg_scr[...] = wg_ref[...].astype(BF16)

    wd_out[...] = wd_ref[...].astype(BF16)

    l = h_ref.shape[1]
    pr = l // row_pieces
    sub = lax.broadcasted_iota(jnp.int32, (SUBLANES, 1), 0)

    def conv(w_scr, cw_ref, cb_ref):
        u = jnp.concatenate([_dot(h_ref[0, i * pr:(i + 1) * pr, :], w_scr[...]) for i in range(row_pieces)], axis=0)
        prev = pltpu.roll(u, 1, 0)
        nxt = pltpu.roll(u, l - 1, 0)
        prev = jnp.concatenate([jnp.where(sub == 0, 0.0, prev[:SUBLANES]), prev[SUBLANES:]], axis=0)
        nxt = jnp.concatenate([nxt[:-SUBLANES], jnp.where(sub == SUBLANES - 1, 0.0, nxt[-SUBLANES:])], axis=0)
        cw = cw_ref[...]
        return cw[0:1, :] * prev + cw[1:2, :] * u + cw[2:3, :] * nxt + cb_ref[...]

    a = conv(wa_scr, cwa_ref, cba_ref)
    gt = conv(wg_scr, cwg_ref, cbg_ref)
    o_ref[0] = (_silu(gt) * a).astype(BF16)


def _ffn_up(h, w_up, conv_w, conv_b, w_down, layer):
    b, l, d = h.shape
    d_out = w_down.shape[2]
    f = w_up.shape[2] // 2
    tn = _tile("ffn_up_tn", f)
    n_n = f // tn
    assert l <= TILES["ffn_up_max_rows"], "the row tile is the whole sequence"
    row_pieces = max(1, l // TILES["ffn_up_piece_rows"])
    assert l % (row_pieces * BF16_TILE_ROWS) == 0, (l, row_pieces)
    col = lambda rows, off: pl.BlockSpec((None, rows, tn), lambda n, bi: (layer, 0, n + off))
    band = f // (n_n * b)
    assert band * n_n * b == f and band % BF16_TILE_ROWS == 0, (f, band)
    vmem = (2 * l * d * 2 + 4 * d * tn * 4 + 2 * d * tn * 2 + 2 * band * d_out * 6
            + 2 * l * tn * 2 + 10 * l * tn * 4 + (4 << 20))
    return pl.pallas_call(
        functools.partial(_ffn_up_kernel, row_pieces=row_pieces),
        grid=(n_n, b),
        in_specs=[
            pl.BlockSpec((1, l, d), lambda n, bi: (bi, 0, 0)),
            col(d, 0), col(d, n_n), col(3, 0), col(3, n_n), col(1, 0), col(1, n_n),
            pl.BlockSpec((None, band, d_out), lambda n, bi: (layer, n * b + bi, 0)),
        ],
        out_specs=[pl.BlockSpec((1, l, tn), lambda n, bi: (bi, 0, n)),
                   pl.BlockSpec((band, d_out), lambda n, bi: (n * b + bi, 0))],
        out_shape=[jax.ShapeDtypeStruct((b, l, f), BF16), jax.ShapeDtypeStruct((f, d_out), BF16)],
        scratch_shapes=[pltpu.VMEM((d, tn), BF16), pltpu.VMEM((d, tn), BF16)],
        compiler_params=_cparams(("arbitrary", "arbitrary"), vmem),
        name="ffn_up",
    )(h, w_up, w_up, conv_w, conv_w, conv_b, conv_b, w_down)


def _ffn_down_kernel(a_ref, w_ref, x_ref, mod_ref, o_ref):
    y = _dot(a_ref[0], w_ref[...])
    o_ref[0] = x_ref[0] + mod_ref[0, GATE2:GATE2 + 1, :] * y


def _ffn_down(act, w_bf16, x, mod, mod_row, n_col_tiles=None):
    b, l, d = x.shape
    tm = _tile("ffn_down_tm", l)
    tn = _tile("ffn_down_tn", d)
    f = act.shape[2]
    if n_col_tiles is None:
        n_col_tiles = d // tn
    vmem = 2 * tm * f * 2 + 2 * f * tn * 2 + 5 * tm * tn * 4 + (6 << 20)
    return pl.pallas_call(
        _ffn_down_kernel,
        grid=(n_col_tiles, b, l // tm),
        in_specs=[
            pl.BlockSpec((1, tm, f), lambda n, bi, m: (bi, m, 0)),
            pl.BlockSpec((f, tn), lambda n, bi, m: (0, n)),
            pl.BlockSpec((1, tm, tn), lambda n, bi, m: (bi, m, n)),
            pl.BlockSpec((1, 6, tn), _mod_index(mod_row, 1, col_axis=0)),
        ],
        out_specs=pl.BlockSpec((1, tm, tn), lambda n, bi, m: (bi, m, n)),
        out_shape=jax.ShapeDtypeStruct((b, l, n_col_tiles * tn), F32),
        compiler_params=_cparams(("parallel", "parallel", "parallel"), vmem),
        name="ffn_down",
    )(act, w_bf16, x, mod)


def _ffn_down_norm_kernel(a_ref, w_ref, x_ref, left_ref, mod_ref, g_ref, o_ref):
    y = _dot(a_ref[0], w_ref[...])
    right = x_ref[0] + mod_ref[0, GATE2:GATE2 + 1, :] * y
    left = left_ref[0]
    wl = left.shape[1]
    d = wl + right.shape[1]
    ms = (jnp.sum(left * left, axis=-1, keepdims=True) + jnp.sum(right * right, axis=-1, keepdims=True)) / d
    r = lax.rsqrt(ms + EPS)
    o_ref[0, :, :wl] = left * r * g_ref[:, :wl]
    o_ref[0, :, wl:] = right * r * g_ref[:, wl:]


def _ffn_down_norm(act, w_bf16, x, left, mod, mod_row, g):
    b, l, d = x.shape
    tm = _tile("ffn_down_tm", l)
    tn = _tile("ffn_down_tn", d)
    f = act.shape[2]
    last = d // tn - 1
    vmem = 2 * tm * f * 2 + 2 * f * tn * 2 + 6 * tm * d * 4 + (6 << 20)
    return pl.pallas_call(
        _ffn_down_norm_kernel,
        grid=(b, l // tm),
        in_specs=[
            pl.BlockSpec((1, tm, f), lambda bi, m: (bi, m, 0)),
            pl.BlockSpec((f, tn), lambda bi, m: (0, last)),
            pl.BlockSpec((1, tm, tn), lambda bi, m: (bi, m, last)),
            pl.BlockSpec((1, tm, d - tn), lambda bi, m: (bi, m, 0)),
            pl.BlockSpec((1, 6, tn), lambda bi, m: (bi if mod_row is None else mod_row, 0, last)),
            pl.BlockSpec((1, d), lambda bi, m: (0, 0)),
        ],
        out_specs=pl.BlockSpec((1, tm, d), lambda bi, m: (bi, m, 0)),
        out_shape=jax.ShapeDtypeStruct((b, l, d), F32),
        compiler_params=_cparams(("parallel", "parallel"), vmem),
        name="ffn_down_norm",
    )(act, w_bf16, x, left, mod, g)


def _pool_kernel(xm_ref, xp_ref, xn_ref, mod_ref, g_ref, w_ref, ps_ref, g2_ref, x_out, h_out, x1_scr, w_scr,
                 *, tm, n_m, seq_len):
    m = pl.program_id(1)

    @pl.when((pl.program_id(0) == 0) & (m == 0))
    def _():
        w_scr[...] = w_ref[...].astype(BF16)

    g = g_ref[...]
    shift = mod_ref[0, SHIFT1:SHIFT1 + 1, :]
    scale = mod_ref[0, SCALE1:SCALE1 + 1, :]
    xm = xm_ref[0]
    hp = _norm_mod(xp_ref[0], g, shift, scale) * (m > 0).astype(F32)
    hn = _norm_mod(xn_ref[0], g, shift, scale) * (m < n_m - 1).astype(F32)
    hh = jnp.concatenate([hp, _norm_mod(xm, g, shift, scale), hn], axis=0)
    rows = tm + 2 * HALO
    t = (m * tm + lax.broadcasted_iota(jnp.int32, (tm, 1), 0)).astype(F32)
    gw = hh.shape[1] // len(POOL_WINDOWS)
    out_gain = mod_ref[0, GATE1:GATE1 + 1, :] * ps_ref[...]
    for gi, w in enumerate(POOL_WINDOWS):
        sl = slice(gi * gw, (gi + 1) * gw)
        hg = hh[:, sl]
        acc = hg
        span = 1
        while span < w // 2:
            acc = acc + pltpu.roll(acc, rows - span, 0)
            span *= 2
        acc = acc + pltpu.roll(acc, w // 2, 0)
        win = acc[HALO:HALO + tm]
        cnt = jnp.minimum(t + (w // 2 - 1), seq_len - 1.0) - jnp.maximum(t - w // 2, 0.0) + 1.0
        p = (win / cnt - hg[HALO:HALO + tm]).astype(BF16)
        x1_scr[:, sl] = xm[:, sl] + out_gain[:, sl] * _dot(p, w_scr[gi])
    x1 = x1_scr[...]
    x_out[0] = x1
    h_out[0] = _norm_mod(x1, g2_ref[...], mod_ref[0, SHIFT2:SHIFT2 + 1, :],
                         mod_ref[0, SCALE2:SCALE2 + 1, :]).astype(BF16)


def _pool_mix(x, mod, mod_row, g, pool_w, pool_scale, g2):
    b, l, d = x.shape
    tm = _tile("pool_tm", l)
    n_m = l // tm
    hb = tm // HALO
    ng, gw, _ = pool_w.shape
    row = pl.BlockSpec((1, tm, d), lambda bi, m: (bi, m, 0))
    vec = pl.BlockSpec((1, d), lambda bi, m: (0, 0))
    vmem = 7 * tm * d * 4 + 2 * ng * gw * gw * 4 + 8 * (tm + 2 * HALO) * d * 4 + (6 << 20)
    return pl.pallas_call(
        functools.partial(_pool_kernel, tm=tm, n_m=n_m, seq_len=l),
        grid=(b, n_m),
        in_specs=[
            row,
            pl.BlockSpec((1, HALO, d), lambda bi, m: (bi, jnp.maximum(m * hb - 1, 0), 0)),
            pl.BlockSpec((1, HALO, d), lambda bi, m: (bi, jnp.minimum((m + 1) * hb, l // HALO - 1), 0)),
            pl.BlockSpec((1, 6, d), _mod_index(mod_row, 0)),
            vec,
            pl.BlockSpec((ng, gw, gw), lambda bi, m: (0, 0, 0)),
            vec, vec,
        ],
        out_specs=[row, row],
        out_shape=[jax.ShapeDtypeStruct((b, l, d), F32), jax.ShapeDtypeStruct((b, l, d), BF16)],
        scratch_shapes=[pltpu.VMEM((tm, d), F32), pltpu.VMEM((ng, gw, gw), BF16)],
        compiler_params=_cparams(("arbitrary", "arbitrary"), vmem),
        name="pool_mix",
    )(x, x, x, mod, g, pool_w, pool_scale, g2)


def _final_norm_kernel(x_ref, g_ref, o_ref):
    x = x_ref[0]
    o_ref[0] = x * lax.rsqrt(jnp.mean(x * x, axis=-1, keepdims=True) + EPS) * g_ref[...]


def _final_norm(x, g):
    b, l, d = x.shape
    tm = _tile("final_tm", l)
    return pl.pallas_call(
        _final_norm_kernel,
        grid=(b, l // tm),
        in_specs=[pl.BlockSpec((1, tm, d), lambda bi, m: (bi, m, 0)), pl.BlockSpec((1, d), lambda bi, m: (0, 0))],
        out_specs=pl.BlockSpec((1, tm, d), lambda bi, m: (bi, m, 0)),
        out_shape=jax.ShapeDtypeStruct((b, l, d), F32),
        compiler_params=_cparams(("parallel", "parallel"), 6 * tm * d * 4 + (4 << 20)),
        name="final_norm",
    )(x, g)


def _axial_angles(rows, dim):
    row = np.repeat(np.arange(rows, dtype=np.float32), GRID_W)
    col = np.tile(np.arange(GRID_W, dtype=np.float32), rows)
    n_freq = dim // 4
    inv = np.float32(ROPE_BASE) ** (-np.arange(n_freq, dtype=np.float32) / np.float32(n_freq))
    return np.concatenate([row[:, None] * inv, col[:, None] * inv], axis=-1).astype(np.float32)


def _rope_tables(rows, dim):
    ang = _axial_angles(rows, dim)
    cos, sin = np.cos(ang), np.sin(ang)
    cos_t = np.concatenate([cos, cos], axis=-1)
    sin_t = np.concatenate([-sin, sin], axis=-1)
    pad = LANES - dim
    if pad:
        cos_t = np.pad(cos_t, ((0, 0), (0, pad)))
        sin_t = np.pad(sin_t, ((0, 0), (0, pad)))
    return jnp.asarray(cos_t, F32), jnp.asarray(sin_t, F32)


def _mla_weights(g_q, w_uq, g_kv, w_ukv, heads, ret_w):
    q_rank, kv_rank = g_q.shape[0], g_kv.shape[0]
    wq = w_uq.reshape(q_rank, heads, HEAD_DIM + MLA_DR)
    wq = jnp.pad(wq, ((0, 0), (0, 0), (0, HEAD_DIM - MLA_DR))).reshape(q_rank, heads * 2 * HEAD_DIM).astype(BF16)
    wkv = w_ukv.reshape(kv_rank, heads, 2 * HEAD_DIM)
    wkn = wkv[:, :, :HEAD_DIM].reshape(kv_rank, ret_w).astype(BF16)
    wv = wkv[:, :, HEAD_DIM:].reshape(kv_rank, ret_w).astype(BF16)
    return g_q[None, :], wq, g_kv[None, :], wkn, wv


def _conv_ffn(h, x, mod, mod_row, w_up, conv_w, conv_b, w_down, layer, final_g=None):
    act, w_down_bf16 = _ffn_up(h, w_up, conv_w, conv_b[:, None, :], w_down, layer)
    d = x.shape[2]
    n_col_tiles = d // _tile("ffn_down_tn", d)
    if final_g is None:
        return _ffn_down(act, w_down_bf16, x, mod, mod_row)
    if n_col_tiles == 1:
        return _final_norm(_ffn_down(act, w_down_bf16, x, mod, mod_row), final_g)
    left = _ffn_down(act, w_down_bf16, x, mod, mod_row, n_col_tiles - 1)
    return _ffn_down_norm(act, w_down_bf16, x, left, mod, mod_row, final_g)


def kernel(x, c, ctx, c_ctx, ada_w, ada_b, norm1_g, norm2_g, ffn_w_up, ffn_conv_w, ffn_conv_b, ffn_w_down, mix_w_in, mla_q_norm_g, mla_w_uq, mla_kv_norm_g, mla_w_ukv, ret_decay_f, ret_decay_b, mix_w_out, pool_w, pool_scale, final_g):
    b, l, d = x.shape
    lc = ctx.shape[1]
    depth = ada_w.shape[0]
    heads = ret_decay_f.shape[1]
    ret_w = heads * HEAD_DIM
    rows = l // GRID_W
    rope_ret = _rope_tables(rows, HEAD_DIM)
    rope_mla = _rope_tables(rows, MLA_DR)

    ctx_row = b
    cc = jnp.concatenate([c, c_ctx[None, :], jnp.zeros((-(b + 1) % 8, d), F32)], axis=0)
    mods = _ada(cc, ada_w, ada_b).reshape(depth, cc.shape[0], 6, d)

    x_lat, x_ctx = x, ctx
    for layer in range(depth):
        j = layer // 2
        with_ctx = layer < depth - 1
        mod = mods[layer]
        g1 = norm1_g[layer][None, :]
        g2 = norm2_g[layer][None, :]
        h_ctx = None
        if layer % 2 == 0:
            w_in_t = jnp.swapaxes(mix_w_in[j], 0, 1)
            mla_w = _mla_weights(mla_q_norm_g[j], mla_w_uq[j], mla_kv_norm_g[j], mla_w_ukv[j], heads, ret_w)
            dec = jnp.stack([ret_decay_f[j], ret_decay_b[j]]).astype(F32)

            h1_lat, q_lat, k_lat, v_lat = _inproj_mla(x_lat, mod, None, g1, w_in_t, 4 * ret_w, *mla_w, rope_mla, heads)
            h1_ctx, q_ctx, k_ctx, v_ctx = _inproj_mla(x_ctx.reshape(1, b * lc, d), mod, ctx_row, g1, w_in_t,
                                                      4 * ret_w, *mla_w, None, heads)
            q_ctx, k_ctx, v_ctx = [a.reshape(b, lc, -1) for a in (q_ctx, k_ctx, v_ctx)]
            ret_lat, w_out_bf16 = _inproj_ret(h1_lat, w_in_t, rope_ret, ret_w, cast_extra=mix_w_out[j])
            ret_ctx = _inproj_ret(h1_ctx, w_in_t, None, ret_w).reshape(b, lc, 4 * ret_w)

            o_ret_lat, o_ret_ctx = _retention(dec, ret_lat, ret_ctx, heads)
            o_mla_lat = _attention(q_lat, k_ctx, v_ctx, k_lat, v_lat, heads)
            x_lat, h_lat = _mix_out(o_ret_lat, o_mla_lat, w_out_bf16, x_lat, mod, None, g2)
            if with_ctx:
                o_mla_ctx = _attention(q_ctx, k_ctx, v_ctx, None, None, heads)
                x_ctx, h_ctx = _mix_out(o_ret_ctx, o_mla_ctx, w_out_bf16, x_ctx, mod, ctx_row, g2)
        else:
            ps = pool_scale[j][None, :]
            if with_ctx:
                x_ctx, h_ctx = _pool_mix(x_ctx, mod, ctx_row, g1, pool_w[j], ps, g2)
            x_lat, h_lat = _pool_mix(x_lat, mod, None, g1, pool_w[j], ps, g2)
        last = layer == depth - 1
        x_lat = _conv_ffn(h_lat, x_lat, mod, None, ffn_w_up, ffn_conv_w, ffn_conv_b, ffn_w_down, layer,
                          final_g=final_g[None, :] if last else None)
        if with_ctx:
            x_ctx = _conv_ffn(h_ctx, x_ctx, mod, ctx_row, ffn_w_up, ffn_conv_w, ffn_conv_b, ffn_w_down, layer)
    return x_lat
```

```python
import functools

import jax
import jax.numpy as jnp
import numpy as np
from jax import lax
from jax.experimental import pallas as pl
from jax.experimental.pallas import tpu as pltpu

F32 = jnp.float32
BF16 = jnp.bfloat16

GRID_W = 64
HEAD_DIM = 128
RET_CHUNK = 128
MLA_DR = 64
POOL_WINDOWS = (2, 4, 8, 16)
ROPE_BASE = 10000.0
EPS = 1e-6
LOG2_E = 1.4426950408889634

LANES = 128
MXU_DIM = 256
SUBLANES = 8
BF16_TILE_ROWS = 16
HALO = 16
V7X_VMEM_BYTES = 64 * 1024 * 1024
VMEM_CAP = V7X_VMEM_BYTES - 8 * 1024 * 1024

SHIFT1, SCALE1, GATE1, SHIFT2, SCALE2, GATE2 = range(6)

TILES = dict(
    inproj_ret_tm=2048, inproj_ret_piece_rows=256,
    inproj_mla_tm=512, inproj_mla_piece_rows=256,
    attn_tq=512, attn_heads_per_step=2,
    mix_out_tm=512, mix_out_piece_rows=256,
    ffn_up_tn=256, ffn_up_piece_rows=128, ffn_up_max_rows=2048,
    ffn_down_tm=512, ffn_down_tn=1024,
    pool_tm=512,
    final_tm=512,
    ada_tn=2048,
)


def _tile(name, extent):
    tile = min(extent, TILES[name])
    assert extent % tile == 0, f"{name}: extent {extent} is not a multiple of its tile {tile}"
    return tile


def _cparams(sem, vmem_bytes):
    return pltpu.CompilerParams(dimension_semantics=sem, vmem_limit_bytes=min(int(vmem_bytes), VMEM_CAP))


def _silu(x):
    return x / (1.0 + jnp.exp(-x))


def _norm_mod(x, g, shift, scale):
    gain = g * (1.0 + scale)
    return x * lax.rsqrt(jnp.mean(x * x, axis=-1, keepdims=True) + EPS) * gain + shift


def _dot(a, b):
    return jnp.dot(a, b, preferred_element_type=F32)


def _dot_nt(a, b):
    return lax.dot_general(a, b, (((1,), (1,)), ((), ())), preferred_element_type=F32)


def _dot_tn(a, b):
    return lax.dot_general(a, b, (((0,), (0,)), ((), ())), preferred_element_type=F32)


def _mod_index(mod_row, batch_axis, col_axis=None):
    def index(*ids):
        row = ids[batch_axis] if mod_row is None else mod_row
        return (row, 0, 0 if col_axis is None else ids[col_axis])
    return index


def _ada_kernel(c_ref, w_ref, b_ref, o_ref):
    s = _silu(c_ref[...]).astype(BF16)
    o_ref[0] = _dot(s, w_ref[0].astype(BF16)) + b_ref[0]


def _ada(cc, ada_w, ada_b):
    depth, d, n6 = ada_w.shape
    rows = cc.shape[0]
    tn = _tile("ada_tn", n6)
    return pl.pallas_call(
        _ada_kernel,
        grid=(depth, n6 // tn),
        in_specs=[
            pl.BlockSpec((rows, d), lambda l, n: (0, 0)),
            pl.BlockSpec((1, d, tn), lambda l, n: (l, 0, n)),
            pl.BlockSpec((1, 1, tn), lambda l, n: (l, 0, n)),
        ],
        out_specs=pl.BlockSpec((1, rows, tn), lambda l, n: (l, 0, n)),
        out_shape=jax.ShapeDtypeStruct((depth, rows, n6), F32),
        compiler_params=_cparams(("parallel", "parallel"), 2 * d * tn * 4 + d * tn * 2 + (8 << 20)),
        name="ada_mod",
    )(cc, ada_w, ada_b.reshape(depth, 1, n6))


def _inproj_ret_kernel(*refs, rope, cast_extra, heads_per_tile, k_scale, row_pieces):
    refs = list(refs)
    h_ref = refs.pop(0)
    if rope:
        cos_ref, sin_ref = refs.pop(0), refs.pop(0)
    w_ref = refs.pop(0)
    if cast_extra:
        extra_ref = refs.pop(0)
    o_ref = refs.pop(0)
    if cast_extra:
        extra_out = refs.pop(0)
    (w_scr,) = refs
    n = pl.program_id(0)

    @pl.when((pl.program_id(1) == 0) & (pl.program_id(2) == 0))
    def _():
        w_scr[...] = w_ref[...].astype(BF16)

    if cast_extra:
        extra_out[...] = extra_ref[...].astype(BF16)
    scale = jnp.where(n == 1, k_scale, 1.0).astype(F32)
    if rope:
        rotated = n < 2
        cos = jnp.where(rotated, cos_ref[...], 1.0) * scale
        sin = jnp.where(rotated, sin_ref[...], 0.0) * scale
    pr = h_ref.shape[1] // row_pieces
    for i in range(row_pieces):
        rs = slice(i * pr, (i + 1) * pr)
        z = _dot_nt(h_ref[0, rs, :], w_scr[...])
        if rope:
            for h in range(heads_per_tile):
                zh = z[:, h * HEAD_DIM:(h + 1) * HEAD_DIM]
                r = zh * cos[rs] + pltpu.roll(zh, HEAD_DIM // 2, 1) * sin[rs]
                o_ref[0, rs, h * HEAD_DIM:(h + 1) * HEAD_DIM] = r.astype(BF16)
        else:
            o_ref[0, rs, :] = (z * scale).astype(BF16)


def _inproj_ret(h, w_in_t, rope_tabs, ret_w, cast_extra=None):
    b, l, d = h.shape
    tm = _tile("inproj_ret_tm", l)
    tn = ret_w
    n_tiles = 4
    n_m = l // tm
    rope = rope_tabs is not None
    in_specs = [pl.BlockSpec((1, tm, d), lambda n, bi, m: (bi, m, 0))]
    args = [h]
    if rope:
        in_specs += [pl.BlockSpec((tm, HEAD_DIM), lambda n, bi, m: (m, 0))] * 2
        args += list(rope_tabs)
    in_specs.append(pl.BlockSpec((tn, d), lambda n, bi, m: (n, 0)))
    args.append(w_in_t)
    out_specs = [pl.BlockSpec((1, tm, tn), lambda n, bi, m: (bi, m, n))]
    out_shape = [jax.ShapeDtypeStruct((b, l, n_tiles * tn), BF16)]
    vmem = 2 * tm * d * 2 + 2 * d * tn * 4 + d * tn * 2 + 2 * tm * tn * 2 + 4 * tm * tn * 4 + (6 << 20)
    if cast_extra is not None:
        er, ec = cast_extra.shape
        band = er // (n_tiles * b * n_m)
        assert band * n_tiles * b * n_m == er and band % BF16_TILE_ROWS == 0, (er, band)
        extra_spec = pl.BlockSpec((band, ec), lambda n, bi, m: ((n * b + bi) * n_m + m, 0))
        in_specs.append(extra_spec)
        args.append(cast_extra)
        out_specs.append(extra_spec)
        out_shape.append(jax.ShapeDtypeStruct((er, ec), BF16))
        vmem += 2 * band * ec * 6
    outs = pl.pallas_call(
        functools.partial(_inproj_ret_kernel, rope=rope, cast_extra=cast_extra is not None,
                          heads_per_tile=tn // HEAD_DIM, k_scale=HEAD_DIM ** -0.5,
                          row_pieces=max(1, tm // TILES["inproj_ret_piece_rows"])),
        grid=(n_tiles, b, l // tm),
        in_specs=in_specs,
        out_specs=out_specs,
        out_shape=out_shape,
        scratch_shapes=[pltpu.VMEM((tn, d), BF16)],
        compiler_params=_cparams(("arbitrary", "arbitrary", "arbitrary"), vmem),
        name="inproj_ret_rope" if rope else "inproj_ret",
    )(*args)
    return outs if cast_extra is not None else outs[0]


def _rope_half_padded(x, cos, sin):
    rot = pltpu.roll(x, MLA_DR // 2, 1) + pltpu.roll(x, LANES - MLA_DR // 2, 1)
    return x * cos + rot * sin


def _inproj_mla_kernel(*refs, rope, heads, q_rank, kv_rank, q_scale, row_pieces):
    refs = list(refs)
    x_ref, mod_ref, g_ref = refs[:3]
    del refs[:3]
    if rope:
        cos_ref, sin_ref = refs[:2]
        del refs[:2]
    (wcq_ref, wckv_ref, wkr_ref, gq_ref, wq_ref, gkv_ref, wkn_ref, wv_ref,
     h_out, q_out, k_out, v_out, wt_scr) = refs

    @pl.when((pl.program_id(0) == 0) & (pl.program_id(1) == 0))
    def _():
        wt_scr[0:q_rank, :] = wcq_ref[...].astype(BF16)
        wt_scr[q_rank:q_rank + kv_rank, :] = wckv_ref[...].astype(BF16)
        wt_scr[q_rank + kv_rank:q_rank + kv_rank + MLA_DR, :] = wkr_ref[...].astype(BF16)
        wt_scr[q_rank + kv_rank + MLA_DR:, :] = jnp.zeros((LANES - MLA_DR, wt_scr.shape[1]), BF16)

    pr = x_ref.shape[1] // row_pieces
    for i in range(row_pieces):
        rs = slice(i * pr, (i + 1) * pr)
        h = _norm_mod(x_ref[0, rs, :], g_ref[...], mod_ref[0, SHIFT1:SHIFT1 + 1, :],
                      mod_ref[0, SCALE1:SCALE1 + 1, :]).astype(BF16)
        h_out[0, rs, :] = h
        zt = _dot_nt(h, wt_scr[...])
        cq = zt[:, :q_rank]
        ckv = zt[:, q_rank:q_rank + kv_rank]
        kr = zt[:, q_rank + kv_rank:]
        cqn = (cq * lax.rsqrt(jnp.mean(cq * cq, axis=-1, keepdims=True) + EPS) * gq_ref[...]).astype(BF16)
        ckvn = (ckv * lax.rsqrt(jnp.mean(ckv * ckv, axis=-1, keepdims=True) + EPS) * gkv_ref[...]).astype(BF16)
        q = _dot(cqn, wq_ref[...])
        kn = _dot(ckvn, wkn_ref[...])
        v_out[0, rs, :] = _dot(ckvn, wv_ref[...]).astype(BF16)
        if rope:
            cos = cos_ref[rs, :]
            sin = sin_ref[rs, :]
            kr = _rope_half_padded(kr, cos, sin)
        kr = kr.astype(BF16)
        for hd in range(heads):
            base = hd * 2 * HEAD_DIM
            qn = q[:, base:base + HEAD_DIM]
            qr = q[:, base + HEAD_DIM:base + 2 * HEAD_DIM]
            if rope:
                qr = _rope_half_padded(qr, cos, sin)
            q_out[0, rs, base:base + HEAD_DIM] = (qn * q_scale).astype(BF16)
            q_out[0, rs, base + HEAD_DIM:base + 2 * HEAD_DIM] = (qr * q_scale).astype(BF16)
            k_out[0, rs, base:base + HEAD_DIM] = kn[:, hd * HEAD_DIM:(hd + 1) * HEAD_DIM].astype(BF16)
            k_out[0, rs, base + HEAD_DIM:base + 2 * HEAD_DIM] = kr


def _inproj_mla(x, mod, mod_row, g, w_in_t, tail_row, gq, wq, gkv, wkn, wv, rope_tabs, heads):
    b, l, d = x.shape
    tm = _tile("inproj_mla_tm", l)
    q_rank, kv_rank = gq.shape[1], gkv.shape[1]
    rope = rope_tabs is not None
    full = lambda a: pl.BlockSpec(a.shape, lambda bi, m: (0,) * a.ndim)
    in_specs = [pl.BlockSpec((1, tm, d), lambda bi, m: (bi, m, 0)),
                pl.BlockSpec((1, 6, d), _mod_index(mod_row, 0)), full(g)]
    args = [x, mod, g]
    if rope:
        in_specs += [pl.BlockSpec((tm, LANES), lambda bi, m: (m, 0))] * 2
        args += list(rope_tabs)
    for start, rows in ((tail_row, q_rank), (tail_row + q_rank, kv_rank), (tail_row + q_rank + kv_rank, MLA_DR)):
        assert start % rows == 0, (start, rows)
        in_specs.append(pl.BlockSpec((rows, d), functools.partial(lambda blk, bi, m: (blk, 0), start // rows)))
        args.append(w_in_t)
    for a in (gq, wq, gkv, wkn, wv):
        in_specs.append(full(a))
        args.append(a)
    qw = heads * 2 * HEAD_DIM
    vw = heads * HEAD_DIM
    tail = q_rank + kv_rank + LANES
    out_spec = lambda w: pl.BlockSpec((1, tm, w), lambda bi, m: (bi, m, 0))
    wbytes = sum(a.size * a.dtype.itemsize for a in (wq, wkn, wv))
    vmem = (2 * tm * d * 4 + 2 * tail * d * 4 + tail * d * 2 + 2 * wbytes + 2 * tm * (d + 2 * qw + vw) * 2
            + tm * (3 * d * 4 + 6 * qw * 4) + (4 << 20))
    return pl.pallas_call(
        functools.partial(_inproj_mla_kernel, rope=rope, heads=heads, q_rank=q_rank, kv_rank=kv_rank,
                          q_scale=(HEAD_DIM + MLA_DR) ** -0.5 * LOG2_E,
                          row_pieces=max(1, tm // TILES["inproj_mla_piece_rows"])),
        grid=(b, l // tm),
        in_specs=in_specs,
        out_specs=[out_spec(d), out_spec(qw), out_spec(qw), out_spec(vw)],
        out_shape=[jax.ShapeDtypeStruct((b, l, d), BF16), jax.ShapeDtypeStruct((b, l, qw), BF16),
                   jax.ShapeDtypeStruct((b, l, qw), BF16), jax.ShapeDtypeStruct((b, l, vw), BF16)],
        scratch_shapes=[pltpu.VMEM((tail, d), BF16)],
        compiler_params=_cparams(("arbitrary", "arbitrary"), vmem),
        name="inproj_mla_rope" if rope else "inproj_mla",
    )(*args)


def _log_sigmoid(x):
    return jnp.minimum(x, 0.0) - jnp.log1p(jnp.exp(-jnp.abs(x)))


def _retention_kernel(dec_ref, ql_ref, kl_ref, vl_ref, gl_ref, qc_ref, kc_ref, vc_ref, gc_ref, ol_ref, oc_ref,
                      st_scr, *, n_ctx, n_lat, c):
    dk = HEAD_DIM
    hd = pl.program_id(1)
    lf = _log_sigmoid(jnp.full((c, c), dec_ref[0, hd], F32))
    lb = _log_sigmoid(jnp.full((c, c), dec_ref[1, hd], F32))
    ii = lax.broadcasted_iota(jnp.int32, (c, c), 0).astype(F32)
    jj = lax.broadcasted_iota(jnp.int32, (c, c), 1).astype(F32)
    diff = ii - jj
    fwd = diff >= 0
    decay = jnp.where(fwd, jnp.exp(jnp.where(fwd, diff, 0.0) * lf), jnp.exp(jnp.where(fwd, 0.0, -diff) * lb))
    ri = lax.broadcasted_iota(jnp.int32, (c, dk), 0).astype(F32)
    lfr = _log_sigmoid(jnp.full((c, dk), dec_ref[0, hd], F32))
    lbr = _log_sigmoid(jnp.full((c, dk), dec_ref[1, hd], F32))
    q_dec_f = jnp.exp((ri + 1.0) * lfr)
    k_dec_f = jnp.exp((c - 1.0 - ri) * lfr)
    q_dec_b = jnp.exp((c - ri) * lbr)
    k_dec_b = jnp.exp(ri * lbr)
    c_dec_f = jnp.exp(c * _log_sigmoid(jnp.full((dk, dk), dec_ref[0, hd], F32)))
    c_dec_b = jnp.exp(c * _log_sigmoid(jnp.full((dk, dk), dec_ref[1, hd], F32)))

    blocks = [(qc_ref, kc_ref, vc_ref, gc_ref, oc_ref, i) for i in range(n_ctx)]
    blocks += [(ql_ref, kl_ref, vl_ref, gl_ref, ol_ref, i) for i in range(n_lat)]

    def rows(ref, i):
        return ref[0, i * c:(i + 1) * c, :]

    for g, (_, k_ref, v_ref, _, _, i) in enumerate(blocks):
        kf = rows(k_ref, i).astype(F32)
        kk = jnp.concatenate([(kf * k_dec_f).astype(BF16), (kf * k_dec_b).astype(BF16)], axis=1)
        st_scr[g] = _dot_tn(kk, rows(v_ref, i))

    s = jnp.zeros((dk, dk), F32)
    for g in range(n_ctx + n_lat):
        kv = st_scr[g, :dk, :]
        st_scr[g, :dk, :] = s
        s = s * c_dec_f + kv
    s = jnp.zeros((dk, dk), F32)
    for g in list(range(n_ctx - 1, -1, -1)) + list(range(n_ctx + n_lat - 1, n_ctx - 1, -1)):
        kv = st_scr[g, dk:, :]
        st_scr[g, dk:, :] = s
        s = s * c_dec_b + kv

    for g, (q_ref, k_ref, v_ref, g_ref, o_ref, i) in enumerate(blocks):
        q = rows(q_ref, i)
        p = (_dot_nt(q, rows(k_ref, i)) * decay).astype(BF16)
        qf = q.astype(F32)
        qq = jnp.concatenate([(qf * q_dec_f).astype(BF16), (qf * q_dec_b).astype(BF16)], axis=1)
        o = _dot(p, rows(v_ref, i)) + _dot(qq, st_scr[g].astype(BF16))
        o = o * lax.rsqrt(jnp.mean(o * o, axis=-1, keepdims=True) + EPS)
        o_ref[0, i * c:(i + 1) * c, :] = (o * _silu(rows(g_ref, i).astype(F32))).astype(BF16)


def _retention(dec, ret_lat, ret_ctx, heads):
    b, l, _ = ret_lat.shape
    lc = ret_ctx.shape[1]
    c = MXU_DIM if (l % MXU_DIM == 0 and lc % MXU_DIM == 0) else RET_CHUNK
    n_lat, n_ctx = l // c, lc // c
    col = lambda rows, group: pl.BlockSpec((1, rows, HEAD_DIM), lambda bi, h: (bi, 0, group * heads + h))
    in_specs = [pl.BlockSpec(memory_space=pltpu.SMEM)]
    in_specs += [col(l, grp) for grp in range(4)] + [col(lc, grp) for grp in range(4)]
    out_col = lambda rows: pl.BlockSpec((1, rows, HEAD_DIM), lambda bi, h: (bi, 0, h))
    vmem = 2 * 5 * (l + lc) * HEAD_DIM * 2 + (n_lat + n_ctx) * 2 * HEAD_DIM * HEAD_DIM * 4 + (12 << 20)
    return pl.pallas_call(
        functools.partial(_retention_kernel, n_ctx=n_ctx, n_lat=n_lat, c=c),
        grid=(b, heads),
        in_specs=in_specs,
        out_specs=[out_col(l), out_col(lc)],
        out_shape=[jax.ShapeDtypeStruct((b, l, heads * HEAD_DIM), BF16),
                   jax.ShapeDtypeStruct((b, lc, heads * HEAD_DIM), BF16)],
        scratch_shapes=[pltpu.VMEM((n_lat + n_ctx, 2 * HEAD_DIM, HEAD_DIM), F32)],
        compiler_params=_cparams(("parallel", "parallel"), vmem),
        name="retention",
    )(dec, ret_lat, ret_lat, ret_lat, ret_lat, ret_ctx, ret_ctx, ret_ctx, ret_ctx)


def _attn_kernel(*refs, with_lat, tq):
    if with_lat:
        q_ref, kc_ref, vc_ref, kl_ref, vl_ref, o_ref = refs
    else:
        q_ref, kc_ref, vc_ref, o_ref = refs
    kw = 2 * HEAD_DIM
    lq = q_ref.shape[1]
    n_heads = o_ref.shape[2] // HEAD_DIM
    for hp in range(n_heads):
        kcols = slice(hp * kw, (hp + 1) * kw)
        vcols = slice(hp * HEAD_DIM, (hp + 1) * HEAD_DIM)
        tiles = [(r, tq) for r in range(0, lq, tq)]
        if hp == n_heads - 1 and tq % (2 * BF16_TILE_ROWS) == 0:
            r, rn = tiles.pop()
            tiles += [(r, rn // 2), (r + rn // 2, rn // 2)]
        for r0, rn in tiles:
            q = q_ref[0, r0:r0 + rn, kcols]
            sc = _dot_nt(q, kc_ref[0, :, kcols])
            m = jnp.max(sc, axis=-1, keepdims=True)
            if with_lat:
                sl = _dot_nt(q, kl_ref[0, :, kcols])
                m = jnp.maximum(m, jnp.max(sl, axis=-1, keepdims=True))
            pc = jnp.exp2(sc - m)
            den = jnp.sum(pc, axis=-1, keepdims=True)
            o = _dot(pc.astype(BF16), vc_ref[0, :, vcols])
            if with_lat:
                p_lat = jnp.exp2(sl - m)
                den = den + jnp.sum(p_lat, axis=-1, keepdims=True)
                o = o + _dot(p_lat.astype(BF16), vl_ref[0, :, vcols])
            o_ref[0, r0:r0 + rn, vcols] = (o / den).astype(BF16)


def _attention(q, k_ctx, v_ctx, k_lat, v_lat, heads):
    b, lq, _ = q.shape
    tq = _tile("attn_tq", lq)
    hps = _tile("attn_heads_per_step", heads)
    lc = k_ctx.shape[1]
    with_lat = k_lat is not None
    kw = 2 * HEAD_DIM
    head = lambda rows, w: pl.BlockSpec((1, rows, hps * w), lambda bi, h: (bi, 0, h))
    in_specs = [head(lq, kw), head(lc, kw), head(lc, HEAD_DIM)]
    args = [q, k_ctx, v_ctx]
    lk = lc
    if with_lat:
        ll = k_lat.shape[1]
        lk += ll
        in_specs += [head(ll, kw), head(ll, HEAD_DIM)]
        args += [k_lat, v_lat]
    vmem = 2 * hps * (lq * kw + lk * (kw + HEAD_DIM) + lq * HEAD_DIM) * 2 + 8 * tq * lk * 4 + (8 << 20)
    return pl.pallas_call(
        functools.partial(_attn_kernel, with_lat=with_lat, tq=tq),
        grid=(b, heads // hps),
        in_specs=in_specs,
        out_specs=head(lq, HEAD_DIM),
        out_shape=jax.ShapeDtypeStruct((b, lq, heads * HEAD_DIM), BF16),
        compiler_params=_cparams(("parallel", "parallel"), vmem),
        name="mla_attn" if with_lat else "mla_attn_ctx",
    )(*args)


def _mix_out_kernel(a1_ref, a2_ref, w_ref, x_ref, mod_ref, g2_ref, x_out, h_out, *, row_pieces):
    k1 = a1_ref.shape[2]
    pr = x_ref.shape[1] // row_pieces
    for i in range(row_pieces):
        rs = slice(i * pr, (i + 1) * pr)
        y = _dot(a1_ref[0, rs, :], w_ref[:k1, :]) + _dot(a2_ref[0, rs, :], w_ref[k1:, :])
        x1 = x_ref[0, rs, :] + mod_ref[0, GATE1:GATE1 + 1, :] * y
        x_out[0, rs, :] = x1
        h_out[0, rs, :] = _norm_mod(x1, g2_ref[...], mod_ref[0, SHIFT2:SHIFT2 + 1, :],
                                    mod_ref[0, SCALE2:SCALE2 + 1, :]).astype(BF16)


def _mix_out(a1, a2, w_bf16, x, mod, mod_row, g2):
    b, l, d = x.shape
    tm = _tile("mix_out_tm", l)
    k1, k2 = a1.shape[2], a2.shape[2]
    row = lambda w: pl.BlockSpec((1, tm, w), lambda bi, m: (bi, m, 0))
    vmem = (k1 + k2) * d * 2 + 2 * tm * (k1 + k2) * 2 + 4 * tm * d * 4 + 2 * tm * d * 2 + 6 * tm * d * 4 + (4 << 20)
    return pl.pallas_call(
        functools.partial(_mix_out_kernel, row_pieces=max(1, tm // TILES["mix_out_piece_rows"])),
        grid=(b, l // tm),
        in_specs=[row(k1), row(k2),
                  pl.BlockSpec((k1 + k2, d), lambda bi, m: (0, 0)),
                  row(d),
                  pl.BlockSpec((1, 6, d), _mod_index(mod_row, 0)),
                  pl.BlockSpec((1, d), lambda bi, m: (0, 0))],
        out_specs=[row(d), row(d)],
        out_shape=[jax.ShapeDtypeStruct((b, l, d), F32), jax.ShapeDtypeStruct((b, l, d), BF16)],
        compiler_params=_cparams(("parallel", "parallel"), vmem),
        name="mix_out",
    )(a1, a2, w_bf16, x, mod, g2)


def _ffn_up_kernel(h_ref, wa_ref, wg_ref, cwa_ref, cwg_ref, cba_ref, cbg_ref, wd_ref, o_ref, wd_out,
                   wa_scr, wg_scr, *, row_pieces):
    @pl.when(pl.program_id(1) == 0)
    def _():
        wa_scr[...] = wa_ref[...].astype(BF16)
        wg_scr[...] = wg_ref[...].astype(BF16)

    wd_out[...] = wd_ref[...].astype(BF16)

    l = h_ref.shape[1]
    pr = l // row_pieces
    sub = lax.broadcasted_iota(jnp.int32, (SUBLANES, 1), 0)

    def conv(w_scr, cw_ref, cb_ref):
        u = jnp.concatenate([_dot(h_ref[0, i * pr:(i + 1) * pr, :], w_scr[...]) for i in range(row_pieces)], axis=0)
        prev = pltpu.roll(u, 1, 0)
        nxt = pltpu.roll(u, l - 1, 0)
        prev = jnp.concatenate([jnp.where(sub == 0, 0.0, prev[:SUBLANES]), prev[SUBLANES:]], axis=0)
        nxt = jnp.concatenate([nxt[:-SUBLANES], jnp.where(sub == SUBLANES - 1, 0.0, nxt[-SUBLANES:])], axis=0)
        cw = cw_ref[...]
        return cw[0:1, :] * prev + cw[1:2, :] * u + cw[2:3, :] * nxt + cb_ref[...]

    a = conv(wa_scr, cwa_ref, cba_ref)
    gt = conv(wg_scr, cwg_ref, cbg_ref)
    o_ref[0] = (_silu(gt) * a).astype(BF16)


def _ffn_up(h, w_up, conv_w, conv_b, w_down, layer):
    b, l, d = h.shape
    d_out = w_down.shape[2]
    f = w_up.shape[2] // 2
    tn = _tile("ffn_up_tn", f)
    n_n = f // tn
    assert l <= TILES["ffn_up_max_rows"], "the row tile is the whole sequence"
    row_pieces = max(1, l // TILES["ffn_up_piece_rows"])
    assert l % (row_pieces * BF16_TILE_ROWS) == 0, (l, row_pieces)
    col = lambda rows, off: pl.BlockSpec((None, rows, tn), lambda n, bi: (layer, 0, n + off))
    band = f // (n_n * b)
    assert band * n_n * b == f and band % BF16_TILE_ROWS == 0, (f, band)
    vmem = (2 * l * d * 2 + 4 * d * tn * 4 + 2 * d * tn * 2 + 2 * band * d_out * 6
            + 2 * l * tn * 2 + 10 * l * tn * 4 + (4 << 20))
    return pl.pallas_call(
        functools.partial(_ffn_up_kernel, row_pieces=row_pieces),
        grid=(n_n, b),
        in_specs=[
            pl.BlockSpec((1, l, d), lambda n, bi: (bi, 0, 0)),
            col(d, 0), col(d, n_n), col(3, 0), col(3, n_n), col(1, 0), col(1, n_n),
            pl.BlockSpec((None, band, d_out), lambda n, bi: (layer, n * b + bi, 0)),
        ],
        out_specs=[pl.BlockSpec((1, l, tn), lambda n, bi: (bi, 0, n)),
                   pl.BlockSpec((band, d_out), lambda n, bi: (n * b + bi, 0))],
        out_shape=[jax.ShapeDtypeStruct((b, l, f), BF16), jax.ShapeDtypeStruct((f, d_out), BF16)],
        scratch_shapes=[pltpu.VMEM((d, tn), BF16), pltpu.VMEM((d, tn), BF16)],
        compiler_params=_cparams(("arbitrary", "arbitrary"), vmem),
        name="ffn_up",
    )(h, w_up, w_up, conv_w, conv_w, conv_b, conv_b, w_down)


def _ffn_down_kernel(a_ref, w_ref, x_ref, mod_ref, o_ref):
    y = _dot(a_ref[0], w_ref[...])
    o_ref[0] = x_ref[0] + mod_ref[0, GATE2:GATE2 + 1, :] * y


def _ffn_down(act, w_bf16, x, mod, mod_row, n_col_tiles=None):
    b, l, d = x.shape
    tm = _tile("ffn_down_tm", l)
    tn = _tile("ffn_down_tn", d)
    f = act.shape[2]
    if n_col_tiles is None:
        n_col_tiles = d // tn
    vmem = 2 * tm * f * 2 + 2 * f * tn * 2 + 5 * tm * tn * 4 + (6 << 20)
    return pl.pallas_call(
        _ffn_down_kernel,
        grid=(n_col_tiles, b, l // tm),
        in_specs=[
            pl.BlockSpec((1, tm, f), lambda n, bi, m: (bi, m, 0)),
            pl.BlockSpec((f, tn), lambda n, bi, m: (0, n)),
            pl.BlockSpec((1, tm, tn), lambda n, bi, m: (bi, m, n)),
            pl.BlockSpec((1, 6, tn), _mod_index(mod_row, 1, col_axis=0)),
        ],
        out_specs=pl.BlockSpec((1, tm, tn), lambda n, bi, m: (bi, m, n)),
        out_shape=jax.ShapeDtypeStruct((b, l, n_col_tiles * tn), F32),
        compiler_params=_cparams(("parallel", "parallel", "parallel"), vmem),
        name="ffn_down",
    )(act, w_bf16, x, mod)


def _ffn_down_norm_kernel(a_ref, w_ref, x_ref, mod_ref, g_ref, o_ref, *, col_pieces):
    d = x_ref.shape[2]
    pc = d // col_pieces
    ss = None
    for j in range(col_pieces):
        cs = slice(j * pc, (j + 1) * pc)
        xo = x_ref[0, :, cs] + mod_ref[0, GATE2:GATE2 + 1, cs] * _dot(a_ref[0], w_ref[:, cs])
        o_ref[0, :, cs] = xo
        p = jnp.sum(xo * xo, axis=-1, keepdims=True)
        ss = p if ss is None else ss + p
    o_ref[0] = o_ref[0] * lax.rsqrt(ss / d + EPS) * g_ref[...]


def _ffn_down_norm(act, w_bf16, x, mod, mod_row, g):
    b, l, d = x.shape
    tm = _tile("ffn_down_tm", l)
    col_pieces = d // _tile("ffn_down_tn", d)
    f = act.shape[2]
    vmem = 2 * tm * f * 2 + f * d * 2 + 6 * tm * d * 4 + (4 << 20)
    return pl.pallas_call(
        functools.partial(_ffn_down_norm_kernel, col_pieces=col_pieces),
        grid=(b, l // tm),
        in_specs=[
            pl.BlockSpec((1, tm, f), lambda bi, m: (bi, m, 0)),
            pl.BlockSpec((f, d), lambda bi, m: (0, 0)),
            pl.BlockSpec((1, tm, d), lambda bi, m: (bi, m, 0)),
            pl.BlockSpec((1, 6, d), _mod_index(mod_row, 0)),
            pl.BlockSpec((1, d), lambda bi, m: (0, 0)),
        ],
        out_specs=pl.BlockSpec((1, tm, d), lambda bi, m: (bi, m, 0)),
        out_shape=jax.ShapeDtypeStruct((b, l, d), F32),
        compiler_params=_cparams(("parallel", "parallel"), vmem),
        name="ffn_down_norm",
    )(act, w_bf16, x, mod, g)


def _pool_kernel(xm_ref, xp_ref, xn_ref, mod_ref, g_ref, w_ref, ps_ref, g2_ref, x_out, h_out, x1_scr, w_scr,
                 *, tm, n_m, seq_len):
    m = pl.program_id(1)

    @pl.when((pl.program_id(0) == 0) & (m == 0))
    def _():
        w_scr[...] = w_ref[...].astype(BF16)

    g = g_ref[...]
    shift = mod_ref[0, SHIFT1:SHIFT1 + 1, :]
    scale = mod_ref[0, SCALE1:SCALE1 + 1, :]
    xm = xm_ref[0]
    hp = _norm_mod(xp_ref[0], g, shift, scale) * (m > 0).astype(F32)
    hn = _norm_mod(xn_ref[0], g, shift, scale) * (m < n_m - 1).astype(F32)
    hh = jnp.concatenate([hp, _norm_mod(xm, g, shift, scale), hn], axis=0)
    rows = tm + 2 * HALO
    t = (m * tm + lax.broadcasted_iota(jnp.int32, (tm, 1), 0)).astype(F32)
    gw = hh.shape[1] // len(POOL_WINDOWS)
    out_gain = mod_ref[0, GATE1:GATE1 + 1, :] * ps_ref[...]
    for gi, w in enumerate(POOL_WINDOWS):
        sl = slice(gi * gw, (gi + 1) * gw)
        hg = hh[:, sl]
        acc = hg
        span = 1
        while span < w // 2:
            acc = acc + pltpu.roll(acc, rows - span, 0)
            span *= 2
        acc = acc + pltpu.roll(acc, w // 2, 0)
        win = acc[HALO:HALO + tm]
        cnt = jnp.minimum(t + (w // 2 - 1), seq_len - 1.0) - jnp.maximum(t - w // 2, 0.0) + 1.0
        p = (win / cnt - hg[HALO:HALO + tm]).astype(BF16)
        x1_scr[:, sl] = xm[:, sl] + out_gain[:, sl] * _dot(p, w_scr[gi])
    x1 = x1_scr[...]
    x_out[0] = x1
    h_out[0] = _norm_mod(x1, g2_ref[...], mod_ref[0, SHIFT2:SHIFT2 + 1, :],
                         mod_ref[0, SCALE2:SCALE2 + 1, :]).astype(BF16)


def _pool_mix(x, mod, mod_row, g, pool_w, pool_scale, g2):
    b, l, d = x.shape
    tm = _tile("pool_tm", l)
    n_m = l // tm
    hb = tm // HALO
    ng, gw, _ = pool_w.shape
    row = pl.BlockSpec((1, tm, d), lambda bi, m: (bi, m, 0))
    vec = pl.BlockSpec((1, d), lambda bi, m: (0, 0))
    vmem = 7 * tm * d * 4 + 2 * ng * gw * gw * 4 + 8 * (tm + 2 * HALO) * d * 4 + (6 << 20)
    return pl.pallas_call(
        functools.partial(_pool_kernel, tm=tm, n_m=n_m, seq_len=l),
        grid=(b, n_m),
        in_specs=[
            row,
            pl.BlockSpec((1, HALO, d), lambda bi, m: (bi, jnp.maximum(m * hb - 1, 0), 0)),
            pl.BlockSpec((1, HALO, d), lambda bi, m: (bi, jnp.minimum((m + 1) * hb, l // HALO - 1), 0)),
            pl.BlockSpec((1, 6, d), _mod_index(mod_row, 0)),
            vec,
            pl.BlockSpec((ng, gw, gw), lambda bi, m: (0, 0, 0)),
            vec, vec,
        ],
        out_specs=[row, row],
        out_shape=[jax.ShapeDtypeStruct((b, l, d), F32), jax.ShapeDtypeStruct((b, l, d), BF16)],
        scratch_shapes=[pltpu.VMEM((tm, d), F32), pltpu.VMEM((ng, gw, gw), BF16)],
        compiler_params=_cparams(("arbitrary", "arbitrary"), vmem),
        name="pool_mix",
    )(x, x, x, mod, g, pool_w, pool_scale, g2)


def _final_norm_kernel(x_ref, g_ref, o_ref):
    x = x_ref[0]
    o_ref[0] = x * lax.rsqrt(jnp.mean(x * x, axis=-1, keepdims=True) + EPS) * g_ref[...]


def _final_norm(x, g):
    b, l, d = x.shape
    tm = _tile("final_tm", l)
    return pl.pallas_call(
        _final_norm_kernel,
        grid=(b, l // tm),
        in_specs=[pl.BlockSpec((1, tm, d), lambda bi, m: (bi, m, 0)), pl.BlockSpec((1, d), lambda bi, m: (0, 0))],
        out_specs=pl.BlockSpec((1, tm, d), lambda bi, m: (bi, m, 0)),
        out_shape=jax.ShapeDtypeStruct((b, l, d), F32),
        compiler_params=_cparams(("parallel", "parallel"), 6 * tm * d * 4 + (4 << 20)),
        name="final_norm",
    )(x, g)


def _axial_angles(rows, dim):
    row = np.repeat(np.arange(rows, dtype=np.float32), GRID_W)
    col = np.tile(np.arange(GRID_W, dtype=np.float32), rows)
    n_freq = dim // 4
    inv = np.float32(ROPE_BASE) ** (-np.arange(n_freq, dtype=np.float32) / np.float32(n_freq))
    return np.concatenate([row[:, None] * inv, col[:, None] * inv], axis=-1).astype(np.float32)


def _rope_tables(rows, dim):
    ang = _axial_angles(rows, dim)
    cos, sin = np.cos(ang), np.sin(ang)
    cos_t = np.concatenate([cos, cos], axis=-1)
    sin_t = np.concatenate([-sin, sin], axis=-1)
    pad = LANES - dim
    if pad:
        cos_t = np.pad(cos_t, ((0, 0), (0, pad)))
        sin_t = np.pad(sin_t, ((0, 0), (0, pad)))
    return jnp.asarray(cos_t, F32), jnp.asarray(sin_t, F32)


def _mla_weights(g_q, w_uq, g_kv, w_ukv, heads, ret_w):
    q_rank, kv_rank = g_q.shape[0], g_kv.shape[0]
    wq = w_uq.reshape(q_rank, heads, HEAD_DIM + MLA_DR)
    wq = jnp.pad(wq, ((0, 0), (0, 0), (0, HEAD_DIM - MLA_DR))).reshape(q_rank, heads * 2 * HEAD_DIM).astype(BF16)
    wkv = w_ukv.reshape(kv_rank, heads, 2 * HEAD_DIM)
    wkn = wkv[:, :, :HEAD_DIM].reshape(kv_rank, ret_w).astype(BF16)
    wv = wkv[:, :, HEAD_DIM:].reshape(kv_rank, ret_w).astype(BF16)
    return g_q[None, :], wq, g_kv[None, :], wkn, wv


def _conv_ffn(h, x, mod, mod_row, w_up, conv_w, conv_b, w_down, layer, final_g=None):
    act, w_down_bf16 = _ffn_up(h, w_up, conv_w, conv_b[:, None, :], w_down, layer)
    if final_g is None:
        return _ffn_down(act, w_down_bf16, x, mod, mod_row)
    return _ffn_down_norm(act, w_down_bf16, x, mod, mod_row, final_g)


def kernel(x, c, ctx, c_ctx, ada_w, ada_b, norm1_g, norm2_g, ffn_w_up, ffn_conv_w, ffn_conv_b, ffn_w_down, mix_w_in, mla_q_norm_g, mla_w_uq, mla_kv_norm_g, mla_w_ukv, ret_decay_f, ret_decay_b, mix_w_out, pool_w, pool_scale, final_g):
    b, l, d = x.shape
    lc = ctx.shape[1]
    depth = ada_w.shape[0]
    heads = ret_decay_f.shape[1]
    ret_w = heads * HEAD_DIM
    rows = l // GRID_W
    rope_ret = _rope_tables(rows, HEAD_DIM)
    rope_mla = _rope_tables(rows, MLA_DR)

    ctx_row = b
    cc = jnp.concatenate([c, c_ctx[None, :], jnp.zeros((-(b + 1) % 8, d), F32)], axis=0)
    mods = _ada(cc, ada_w, ada_b).reshape(depth, cc.shape[0], 6, d)

    x_lat, x_ctx = x, ctx
    for layer in range(depth):
        j = layer // 2
        with_ctx = layer < depth - 1
        mod = mods[layer]
        g1 = norm1_g[layer][None, :]
        g2 = norm2_g[layer][None, :]
        h_ctx = None
        if layer % 2 == 0:
            w_in_t = jnp.swapaxes(mix_w_in[j], 0, 1)
            mla_w = _mla_weights(mla_q_norm_g[j], mla_w_uq[j], mla_kv_norm_g[j], mla_w_ukv[j], heads, ret_w)
            dec = jnp.stack([ret_decay_f[j], ret_decay_b[j]]).astype(F32)

            h1_lat, q_lat, k_lat, v_lat = _inproj_mla(x_lat, mod, None, g1, w_in_t, 4 * ret_w, *mla_w, rope_mla, heads)
            h1_ctx, q_ctx, k_ctx, v_ctx = _inproj_mla(x_ctx.reshape(1, b * lc, d), mod, ctx_row, g1, w_in_t,
                                                      4 * ret_w, *mla_w, None, heads)
            q_ctx, k_ctx, v_ctx = [a.reshape(b, lc, -1) for a in (q_ctx, k_ctx, v_ctx)]
            ret_lat, w_out_bf16 = _inproj_ret(h1_lat, w_in_t, rope_ret, ret_w, cast_extra=mix_w_out[j])
            ret_ctx = _inproj_ret(h1_ctx, w_in_t, None, ret_w).reshape(b, lc, 4 * ret_w)

            o_ret_lat, o_ret_ctx = _retention(dec, ret_lat, ret_ctx, heads)
            o_mla_lat = _attention(q_lat, k_ctx, v_ctx, k_lat, v_lat, heads)
            x_lat, h_lat = _mix_out(o_ret_lat, o_mla_lat, w_out_bf16, x_lat, mod, None, g2)
            if with_ctx:
                o_mla_ctx = _attention(q_ctx, k_ctx, v_ctx, None, None, heads)
                x_ctx, h_ctx = _mix_out(o_ret_ctx, o_mla_ctx, w_out_bf16, x_ctx, mod, ctx_row, g2)
        else:
            ps = pool_scale[j][None, :]
            if with_ctx:
                x_ctx, h_ctx = _pool_mix(x_ctx, mod, ctx_row, g1, pool_w[j], ps, g2)
            x_lat, h_lat = _pool_mix(x_lat, mod, None, g1, pool_w[j], ps, g2)
        last = layer == depth - 1
        x_lat = _conv_ffn(h_lat, x_lat, mod, None, ffn_w_up, ffn_conv_w, ffn_conv_b, ffn_w_down, layer,
                          final_g=final_g[None, :] if last else None)
        if with_ctx:
            x_ctx = _conv_ffn(h_ctx, x_ctx, mod, ctx_row, ffn_w_up, ffn_conv_w, ffn_conv_b, ffn_w_down, layer)
    return x_lat
```
